```python
import math
import jax, jax.numpy as jnp
from jax import lax
import numpy as np

D_MODEL = 2048
BATCH = 2
SEQ = 4096
DEPTH = 1

N_META = 16
BLOCK = 128
EPS = 1e-6
ATT_HEADS = 8
ATT_SUB_DIM = 64
ATT_V_DIM = 2 * ATT_SUB_DIM
ATT_WIDTH = ATT_HEADS * 2 * ATT_SUB_DIM
ROPE_THETA = 10000.0
SSM_WIDTH = D_MODEL // 2
SSM_GROUP = 16
SSM_GROUPS = SSM_WIDTH // SSM_GROUP
SSM_STATE = 64
PEER_HEADS = 8
PEER_KEYS = 128
PEER_EXPERTS = PEER_KEYS * PEER_KEYS
PEER_QDIM = 256
PEER_HALF = PEER_QDIM // 2
PEER_TOPK = 16
IN_WIDTH = 3 * ATT_WIDTH + SSM_WIDTH + 2 * D_MODEL
IN_SPLITS = [ATT_WIDTH, 2 * ATT_WIDTH, 3 * ATT_WIDTH, 3 * ATT_WIDTH + SSM_WIDTH, 3 * ATT_WIDTH + SSM_WIDTH + D_MODEL]

kernel_name = "hybrid_diffattn_s5_peer_block"


def rms_norm(x, g):
    xf = x.astype(jnp.float32)
    y = xf * lax.rsqrt(jnp.mean(xf * xf, axis=-1, keepdims=True) + EPS)
    return (y * g.astype(jnp.float32)).astype(x.dtype)


def rotary(t, pos):
    half = t.shape[-1] // 2
    inv_freq = jnp.power(ROPE_THETA, -jnp.arange(half, dtype=jnp.float32) / half)
    ang = pos.astype(jnp.float32)[:, None] * inv_freq[None, :]
    cos = jnp.cos(ang)[:, None, None, :]
    sin = jnp.sin(ang)[:, None, None, :]
    tf = t.astype(jnp.float32)
    t1, t2 = tf[..., :half], tf[..., half:]
    return jnp.concatenate([t1 * cos - t2 * sin, t2 * cos + t1 * sin], axis=-1).astype(t.dtype)


def diff_attention(q, k, v, lam):
    B, Lp = q.shape[0], q.shape[1]
    nb = Lp // BLOCK
    scale = ATT_SUB_DIM ** -0.5
    qb = q.reshape(B, nb, BLOCK, ATT_HEADS, 2, ATT_SUB_DIM).transpose(1, 0, 2, 3, 4, 5)
    kpos = jnp.arange(Lp)

    def one_block(args):
        qi, i = args
        s = jnp.einsum('bqhcd,bkhcd->bhcqk', qi, k, preferred_element_type=jnp.float32) * scale
        qpos = i * BLOCK + jnp.arange(BLOCK)
        mask = kpos[None, :] <= qpos[:, None]
        s = jnp.where(mask, s, -1e30)
        p = jax.nn.softmax(s, axis=-1)
        w = p[:, :, 0] - lam * p[:, :, 1]
        return jnp.einsum('bhqk,bkhd->bqhd', w.astype(v.dtype), v)

    out = lax.map(one_block, (qb, jnp.arange(nb)))
    return out.transpose(1, 0, 2, 3, 4).reshape(B, Lp, ATT_HEADS, ATT_V_DIM)


def s5_ssm(u, a_re, a_im, log_dt, b_re, b_im, c_re, c_im, d_skip):
    B, L = u.shape[0], u.shape[1]
    uf = u.astype(jnp.float32).reshape(B, L, SSM_GROUPS, SSM_GROUP)
    lam = lax.complex(a_re.astype(jnp.float32), a_im.astype(jnp.float32))
    dt = jnp.exp(log_dt.astype(jnp.float32))[:, None]
    a_bar = jnp.exp(lam * dt)
    b = lax.complex(b_re.astype(jnp.float32), b_im.astype(jnp.float32))
    b_bar = ((a_bar - 1.0) / lam)[..., None] * b
    bu = jnp.einsum('gpc,blgc->blgp', b_bar, uf.astype(jnp.complex64))
    a_elems = jnp.broadcast_to(a_bar, bu.shape)

    def combine(e1, e2):
        a1, s1 = e1
        a2, s2 = e2
        return a1 * a2, a2 * s1 + s2

    _, states = lax.associative_scan(combine, (a_elems, bu), axis=1)
    c = lax.complex(c_re.astype(jnp.float32), c_im.astype(jnp.float32))
    y = jnp.einsum('gcp,blgp->blgc', c, states).real.reshape(B, L, SSM_WIDTH)
    y = y + d_skip.astype(jnp.float32) * u.astype(jnp.float32)
    return y.astype(u.dtype)


def peer(h, w_q, sub_k1, sub_k2, u_emb, v_emb):
    B, L, D = h.shape
    q = (h @ w_q).reshape(B, L, PEER_HEADS, 2, PEER_HALF)
    s1 = jnp.einsum('blhd,hnd->blhn', q[..., 0, :], sub_k1, preferred_element_type=jnp.float32)
    s2 = jnp.einsum('blhd,hnd->blhn', q[..., 1, :], sub_k2, preferred_element_type=jnp.float32)
    v1, i1 = lax.top_k(s1, PEER_TOPK)
    v2, i2 = lax.top_k(s2, PEER_TOPK)
    cand = (v1[..., :, None] + v2[..., None, :]).reshape(B, L, PEER_HEADS, PEER_TOPK * PEER_TOPK)
    cand_idx = (i1[..., :, None] * PEER_KEYS + i2[..., None, :]).reshape(B, L, PEER_HEADS, PEER_TOPK * PEER_TOPK)
    vals, sel = lax.top_k(cand, PEER_TOPK)
    idx = jnp.take_along_axis(cand_idx, sel, axis=-1)
    g = jax.nn.softmax(vals, axis=-1)
    T = B * L
    nb = T // BLOCK
    hb = h.reshape(nb, BLOCK, D)
    ib = idx.reshape(nb, BLOCK, PEER_HEADS * PEER_TOPK)
    gb = g.reshape(nb, BLOCK, PEER_HEADS * PEER_TOPK).astype(h.dtype)

    def one_block(args):
        hi, ii, gi = args
        u = u_emb[ii]
        a = jnp.einsum('td,tkd->tk', hi, u)
        w = gi * jax.nn.gelu(a, approximate=False)
        return jnp.einsum('tk,tkd->td', w, v_emb[ii])

    out = lax.map(one_block, (hb, ib, gb))
    return out.reshape(B, L, D)


def setup_inputs(seed: int = 0) -> dict:
    key = jax.random.key(seed)
    ks = jax.random.split(key, 32)
    f32 = jnp.float32
    nrm = lambda k, shape, s: jax.random.normal(k, shape, f32) * s
    gain = lambda k, shape: 1.0 + 0.01 * jax.random.normal(k, shape, f32)
    Dp = DEPTH
    a_im_base = jnp.pi * jnp.arange(SSM_STATE, dtype=f32)
    return {
        "x": nrm(ks[0], (BATCH, SEQ, D_MODEL), 1.0),
        "meta_tokens": nrm(ks[1], (N_META, D_MODEL), 1.0),
        "norm1_g": gain(ks[2], (Dp, D_MODEL)),
        "w_in": nrm(ks[3], (Dp, D_MODEL, IN_WIDTH), D_MODEL ** -0.5),
        "q_norm_g": gain(ks[4], (Dp, ATT_SUB_DIM)),
        "k_norm_g": gain(ks[5], (Dp, ATT_SUB_DIM)),
        "lambda_q1": nrm(ks[6], (Dp, ATT_SUB_DIM), 0.1),
        "lambda_k1": nrm(ks[7], (Dp, ATT_SUB_DIM), 0.1),
        "lambda_q2": nrm(ks[8], (Dp, ATT_SUB_DIM), 0.1),
        "lambda_k2": nrm(ks[9], (Dp, ATT_SUB_DIM), 0.1),
        "subln_g": gain(ks[10], (Dp, ATT_V_DIM)),
        "w_attn_branch": nrm(ks[11], (Dp, ATT_WIDTH, D_MODEL), ATT_WIDTH ** -0.5),
        "ssm_a_re": -0.5 + 0.01 * jax.random.normal(ks[12], (Dp, SSM_GROUPS, SSM_STATE), f32),
        "ssm_a_im": a_im_base + 0.01 * jax.random.normal(ks[13], (Dp, SSM_GROUPS, SSM_STATE), f32),
        "ssm_log_dt": jax.random.uniform(ks[14], (Dp, SSM_GROUPS), f32, math.log(0.001), math.log(0.1)),
        "ssm_b_re": nrm(ks[15], (Dp, SSM_GROUPS, SSM_STATE, SSM_GROUP), (2 * SSM_GROUP) ** -0.5),
        "ssm_b_im": nrm(ks[16], (Dp, SSM_GROUPS, SSM_STATE, SSM_GROUP), (2 * SSM_GROUP) ** -0.5),
        "ssm_c_re": nrm(ks[17], (Dp, SSM_GROUPS, SSM_GROUP, SSM_STATE), (2 * SSM_STATE) ** -0.5),
        "ssm_c_im": nrm(ks[18], (Dp, SSM_GROUPS, SSM_GROUP, SSM_STATE), (2 * SSM_STATE) ** -0.5),
        "ssm_d": nrm(ks[19], (Dp, SSM_WIDTH), 1.0),
        "w_glu": nrm(ks[20], (Dp, SSM_WIDTH, 2 * D_MODEL), SSM_WIDTH ** -0.5),
        "w_out": nrm(ks[21], (Dp, D_MODEL, D_MODEL), D_MODEL ** -0.5),
        "norm2_g": gain(ks[22], (Dp, D_MODEL)),
        "peer_w_q": nrm(ks[23], (Dp, D_MODEL, PEER_HEADS * PEER_QDIM), D_MODEL ** -0.5),
        "peer_k1": nrm(ks[24], (Dp, PEER_HEADS, PEER_KEYS, PEER_HALF), PEER_HALF ** -0.5),
        "peer_k2": nrm(ks[25], (Dp, PEER_HEADS, PEER_KEYS, PEER_HALF), PEER_HALF ** -0.5),
        "peer_u": nrm(ks[26], (Dp, PEER_EXPERTS, D_MODEL), D_MODEL ** -0.5),
        "peer_v": nrm(ks[27], (Dp, PEER_EXPERTS, D_MODEL), PEER_HEADS ** -0.5),
    }


def reference(x, meta_tokens, norm1_g, w_in, q_norm_g, k_norm_g, lambda_q1, lambda_k1, lambda_q2, lambda_k2,
              subln_g, w_attn_branch, ssm_a_re, ssm_a_im, ssm_log_dt, ssm_b_re, ssm_b_im, ssm_c_re, ssm_c_im,
              ssm_d, w_glu, w_out, norm2_g, peer_w_q, peer_k1, peer_k2, peer_u, peer_v):
    B, S = x.shape[0], x.shape[1]
    Lp = -(-(S + N_META) // BLOCK) * BLOCK
    meta = jnp.broadcast_to(meta_tokens.astype(x.dtype)[None], (B, N_META, D_MODEL))
    pad = jnp.zeros((B, Lp - S - N_META, D_MODEL), x.dtype)
    h = jnp.concatenate([meta, x, pad], axis=1)
    pos = jnp.arange(Lp)
    for l in range(DEPTH):
        lam_init = 0.8 - 0.6 * math.exp(-0.3 * l)
        hn = rms_norm(h, norm1_g[l])
        proj = hn @ w_in[l]
        q, k, v, u, gate_a, gate_b = jnp.split(proj, IN_SPLITS, axis=-1)
        q = q.reshape(B, Lp, ATT_HEADS, 2, ATT_SUB_DIM)
        k = k.reshape(B, Lp, ATT_HEADS, 2, ATT_SUB_DIM)
        v = v.reshape(B, Lp, ATT_HEADS, ATT_V_DIM)
        q = rotary(rms_norm(q, q_norm_g[l]), pos)
        k = rotary(rms_norm(k, k_norm_g[l]), pos)
        lam = (jnp.exp(jnp.sum(lambda_q1[l].astype(jnp.float32) * lambda_k1[l].astype(jnp.float32)))
               - jnp.exp(jnp.sum(lambda_q2[l].astype(jnp.float32) * lambda_k2[l].astype(jnp.float32)))
               + lam_init)
        att = diff_attention(q, k, v, lam)
        att = rms_norm(att, subln_g[l]) * (1.0 - lam_init)
        y_a = att.reshape(B, Lp, ATT_WIDTH) @ w_attn_branch[l]
        y_s = s5_ssm(u, ssm_a_re[l], ssm_a_im[l], ssm_log_dt[l], ssm_b_re[l], ssm_b_im[l],
                     ssm_c_re[l], ssm_c_im[l], ssm_d[l])
        glu = jax.nn.gelu(y_s, approximate=False) @ w_glu[l]
        y_b = glu[..., :D_MODEL] * jax.nn.sigmoid(glu[..., D_MODEL:])
        mix = jax.nn.sigmoid(gate_a) * y_a + jax.nn.sigmoid(gate_b) * y_b
        h = h + mix @ w_out[l]
        h = h + peer(rms_norm(h, norm2_g[l]), peer_w_q[l], peer_k1[l], peer_k2[l], peer_u[l], peer_v[l])
    return h[:, N_META:N_META + S]
```

```python
import functools
import math

import jax
import jax.numpy as jnp
from jax import lax
from jax.experimental import pallas as pl
from jax.experimental.pallas import tpu as pltpu

EPS = 1e-6
ROPE_THETA = 10000.0
LAM_INIT = 0.8 - 0.6 * math.exp(-0.3 * 0)
TOPK = 16
LANES = 128
SUBLANES = 8
NEG_BIG = -1e30
VMEM_LIMIT = 56 * 1024 * 1024

bf16 = jnp.bfloat16
f32 = jnp.float32


def _cparams(sem):
    return pltpu.CompilerParams(dimension_semantics=sem, vmem_limit_bytes=VMEM_LIMIT)


def _dot(a, b):
    return jnp.dot(a, b, preferred_element_type=f32)


def _dot_nt(a, b):
    return lax.dot_general(a, b, (((1,), (1,)), ((), ())), preferred_element_type=f32)


def _sigmoid(x):
    return 1.0 / (1.0 + jnp.exp(-x))


def _gelu(x):
    return 0.5 * x * (1.0 + lax.erf(x * (1.0 / math.sqrt(2.0))))


def _rms_matmul_kernel(x_ref, g_ref, w_ref, o_ref, xn_ref):
    @pl.when(pl.program_id(1) == 0)
    def _():
        x = x_ref[...]
        ms = jnp.mean(x * x, axis=-1, keepdims=True)
        xn_ref[...] = (x * lax.rsqrt(ms + EPS) * g_ref[...]).astype(bf16)

    o_ref[...] = _dot(xn_ref[...], w_ref[...]).astype(o_ref.dtype)


def _rms_matmul(x, g, w, tm, tn, out_dtype, name):
    m, k = x.shape
    n = w.shape[1]
    return pl.pallas_call(
        _rms_matmul_kernel,
        grid=(m // tm, n // tn),
        in_specs=[pl.BlockSpec((tm, k), lambda i, j: (i, 0)),
                  pl.BlockSpec((1, k), lambda i, j: (0, 0)),
                  pl.BlockSpec((k, tn), lambda i, j: (0, j))],
        out_specs=pl.BlockSpec((tm, tn), lambda i, j: (i, j)),
        out_shape=jax.ShapeDtypeStruct((m, n), out_dtype),
        scratch_shapes=[pltpu.VMEM((tm, k), bf16)],
        compiler_params=_cparams(("parallel", "arbitrary")),
        name=name,
    )(x, g, w)


def _normrot_kernel(x_ref, g_ref, cos_ref, sin_ref, j_ref, o_ref, *, scale, sub):
    width = x_ref.shape[1]
    cos = cos_ref[...]
    sin = sin_ref[...]
    jmat = j_ref[...]
    lane = lax.broadcasted_iota(jnp.int32, cos.shape, 1)
    first_half = (lane % sub) < (sub // 2)
    for t in range(width // LANES):
        sl = slice(t * LANES, (t + 1) * LANES)
        x = x_ref[:, sl].astype(f32)
        x2 = x * x
        hi = x2.astype(bf16)
        lo = (x2 - hi.astype(f32)).astype(bf16)
        ssq = _dot(hi, jmat) + _dot(lo, jmat)
        xn = x * lax.rsqrt(ssq * (1.0 / sub) + EPS) * g_ref[:, sl]
        partner = jnp.where(first_half, pltpu.roll(xn, LANES - sub // 2, 1), pltpu.roll(xn, sub // 2, 1))
        o_ref[:, sl] = ((xn * cos + partner * sin) * scale).astype(o_ref.dtype)


def _normrot(x, col_block, width, g_t, cos, sin, jmat, tm, scale, sub, name):
    m = x.shape[0]
    nt = cos.shape[0] // tm
    return pl.pallas_call(
        functools.partial(_normrot_kernel, scale=scale, sub=sub),
        grid=(m // tm,),
        in_specs=[pl.BlockSpec((tm, width), lambda i: (i, col_block)),
                  pl.BlockSpec((1, width), lambda i: (0, 0)),
                  pl.BlockSpec((tm, LANES), lambda i: (i % nt, 0)),
                  pl.BlockSpec((tm, LANES), lambda i: (i % nt, 0)),
                  pl.BlockSpec((LANES, LANES), lambda i: (0, 0))],
        out_specs=pl.BlockSpec((tm, width), lambda i: (i, 0)),
        out_shape=jax.ShapeDtypeStruct((m, width), bf16),
        compiler_params=_cparams(("parallel",)),
        name=name,
    )(x, g_t, cos, sin, jmat)


def _attn_kernel(q_ref, k_ref, v_ref, km_ref, vm_ref, lq1_ref, lk1_ref, lq2_ref, lk2_ref, sg_ref,
                 o_ref, m_scr, l_scr, acc_scr, *, n_meta, sub):
    i = pl.program_id(2)
    tq = q_ref.shape[0]
    tk = tq
    q = q_ref[...]
    lane = lax.broadcasted_iota(jnp.int32, q.shape, 1)
    zero = jnp.zeros_like(q)
    qq = jnp.concatenate([jnp.where(lane < sub, q, zero), jnp.where(lane >= sub, q, zero)], axis=0)

    s = _dot_nt(qq, km_ref[...])
    col = lax.broadcasted_iota(jnp.int32, s.shape, 1)
    s = jnp.where(col < n_meta, s, NEG_BIG)
    m0 = jnp.max(s, axis=1, keepdims=True)
    p = jnp.exp(s - m0)
    m_scr[...] = m0
    l_scr[...] = jnp.sum(p, axis=1, keepdims=True)
    acc_scr[...] = _dot(p.astype(bf16), vm_ref[...])

    def block(j, masked):
        start = pl.multiple_of(j * tk, tk)
        kb = k_ref[pl.ds(start, tk), :]
        vb = v_ref[pl.ds(start, tk), :]
        s = _dot_nt(qq, kb)
        if masked:
            row = lax.broadcasted_iota(jnp.int32, s.shape, 0) % tq
            col = lax.broadcasted_iota(jnp.int32, s.shape, 1)
            s = jnp.where(col <= row, s, NEG_BIG)
        m_prev = m_scr[...]
        m_new = jnp.maximum(m_prev, jnp.max(s, axis=1, keepdims=True))
        alpha = jnp.exp(m_prev - m_new)
        p = jnp.exp(s - m_new)
        l_scr[...] = alpha * l_scr[...] + jnp.sum(p, axis=1, keepdims=True)
        acc_scr[...] = alpha * acc_scr[...] + _dot(p.astype(bf16), vb)
        m_scr[...] = m_new

    def body(j, c):
        block(j, False)
        return c

    lax.fori_loop(0, i, body, 0)
    block(i, True)

    lam = (jnp.exp(jnp.sum(lq1_ref[...] * lk1_ref[...], axis=1, keepdims=True))
           - jnp.exp(jnp.sum(lq2_ref[...] * lk2_ref[...], axis=1, keepdims=True)) + LAM_INIT)
    o = acc_scr[...] / l_scr[...]
    att = o[:tq] - lam * o[tq:]
    ms = jnp.mean(att * att, axis=-1, keepdims=True)
    att = att * lax.rsqrt(ms + EPS) * sg_ref[...] * (1.0 - LAM_INIT)
    o_ref[...] = att.astype(o_ref.dtype)


def _attention(q_rot, k_rot, proj, v_col0, km, vm, lq1, lk1, lq2, lk2, sg, batch, seq, heads, tq, n_meta, sub):
    t, aw = q_rot.shape
    nq = seq // tq
    hw = 2 * sub
    mp = km.shape[0]
    vec = lambda: pl.BlockSpec((1, sub), lambda b, h, i: (0, 0))
    return pl.pallas_call(
        functools.partial(_attn_kernel, n_meta=n_meta, sub=sub),
        grid=(batch, heads, nq),
        in_specs=[pl.BlockSpec((tq, hw), lambda b, h, i: (b * nq + i, h)),
                  pl.BlockSpec((seq, hw), lambda b, h, i: (b, h)),
                  pl.BlockSpec((seq, hw), lambda b, h, i: (b, v_col0 + h)),
                  pl.BlockSpec((mp, hw), lambda b, h, i: (0, h)),
                  pl.BlockSpec((mp, hw), lambda b, h, i: (0, h)),
                  vec(), vec(), vec(), vec(),
                  pl.BlockSpec((1, hw), lambda b, h, i: (0, 0))],
        out_specs=pl.BlockSpec((tq, hw), lambda b, h, i: (b * nq + i, h)),
        out_shape=jax.ShapeDtypeStruct((t, aw), bf16),
        scratch_shapes=[pltpu.VMEM((2 * tq, 1), f32), pltpu.VMEM((2 * tq, 1), f32),
                        pltpu.VMEM((2 * tq, hw), f32)],
        compiler_params=_cparams(("parallel", "parallel", "arbitrary")),
        name="diff_attention",
    )(q_rot, k_rot, proj, km, vm, lq1, lk1, lq2, lk2, sg)


def _cmul(ar, ai, br, bi):
    return ar * br - ai * bi, ar * bi + ai * br


def _ssm_prep_kernel(are_ref, aim_ref, ldt_ref, bre_ref, bim_ref, cre_ref, cim_ref,
                     pwr_ref, pwi_ref, bdr_ref, bdi_ref, bor_ref, boi_ref, *, n_state, n_chan):
    a_re = are_ref[...]
    a_im = aim_ref[...]
    dt = jnp.exp(ldt_ref[...])
    er = jnp.exp(a_re * dt)
    ab_re = er * jnp.cos(a_im * dt)
    ab_im = er * jnp.sin(a_im * dt)
    nr = ab_re - 1.0
    ni = ab_im
    den = a_re * a_re + a_im * a_im
    f_re = (nr * a_re + ni * a_im) / den
    f_im = (ni * a_re - nr * a_im) / den
    p1 = (ab_re, ab_im)
    p2 = _cmul(*p1, *p1)
    p3 = _cmul(*p2, *p1)
    p4 = _cmul(*p2, *p2)
    p5 = _cmul(*p4, *p1)
    p6 = _cmul(*p4, *p2)
    p7 = _cmul(*p4, *p3)
    p8 = _cmul(*p4, *p4)
    pw = (p1, p2, p3, p4, p5, p6, p7, p8)
    pwr_ref[...] = _stack_rows([p[0] for p in pw])
    pwi_ref[...] = _stack_rows([p[1] for p in pw])
    b_re = bre_ref[0]
    b_im = bim_ref[0]
    r = lax.broadcasted_iota(jnp.int32, b_re.shape, 0) // n_chan
    c = lax.broadcasted_iota(jnp.int32, b_re.shape, 1) // n_state
    keep = r == c
    bdr_ref[0] = jnp.where(keep, f_re * b_re - f_im * b_im, 0.0).astype(bf16)
    bdi_ref[0] = jnp.where(keep, f_re * b_im + f_im * b_re, 0.0).astype(bf16)
    c_re = cre_ref[0]
    c_im = cim_ref[0]
    r = lax.broadcasted_iota(jnp.int32, c_re.shape, 0) // n_state
    c = lax.broadcasted_iota(jnp.int32, c_re.shape, 1) // n_chan
    keep = r == c
    bor_ref[0] = jnp.where(keep, c_re, 0.0).astype(bf16)
    boi_ref[0] = jnp.where(keep, c_im, 0.0).astype(bf16)


def _ssm_prep(a_re, a_im, ldt, b_re_t, b_im_t, c_re_t, c_im_t, n_state, n_chan):
    ngt, _, sl = b_re_t.shape
    gp = a_re.shape[1]
    lane_vec = lambda: pl.BlockSpec((1, sl), lambda g: (0, g))
    bin_spec = lambda: pl.BlockSpec((1, LANES, sl), lambda g: (g, 0, 0))
    bout_spec = lambda: pl.BlockSpec((1, sl, LANES), lambda g: (g, 0, 0))
    return pl.pallas_call(
        functools.partial(_ssm_prep_kernel, n_state=n_state, n_chan=n_chan),
        grid=(ngt,),
        in_specs=[lane_vec(), lane_vec(), lane_vec(), bin_spec(), bin_spec(), bout_spec(), bout_spec()],
        out_specs=[pl.BlockSpec((SUBLANES, sl), lambda g: (0, g)), pl.BlockSpec((SUBLANES, sl), lambda g: (0, g)),
                   bin_spec(), bin_spec(), bout_spec(), bout_spec()],
        out_shape=[jax.ShapeDtypeStruct((SUBLANES, gp), f32), jax.ShapeDtypeStruct((SUBLANES, gp), f32),
                   jax.ShapeDtypeStruct(b_re_t.shape, bf16), jax.ShapeDtypeStruct(b_re_t.shape, bf16),
                   jax.ShapeDtypeStruct(c_re_t.shape, bf16), jax.ShapeDtypeStruct(c_re_t.shape, bf16)],
        compiler_params=_cparams(("parallel",)),
        name="ssm_prep",
    )(a_re, a_im, ldt, b_re_t, b_im_t, c_re_t, c_im_t)


def _ssm_scan_kernel(u_ref, bdr_ref, bdi_ref, bor_ref, boi_ref, pwr_ref, pwi_ref, d_ref, x0r_ref, x0i_ref,
                     y_ref, xfr_ref, xfi_ref, xr_scr, xi_scr, cr_scr, ci_scr):
    i = pl.program_id(2)
    tm = u_ref.shape[0]

    @pl.when(i == 0)
    def _():
        cr_scr[...] = x0r_ref[...]
        ci_scr[...] = x0i_ref[...]

    u = u_ref[...]
    xr = _dot(u, bdr_ref[0])
    xi = _dot(u, bdi_ref[0])
    pwr = pwr_ref[...]
    pwi = pwi_ref[...]
    rowmod = lax.broadcasted_iota(jnp.int32, xr.shape, 0) % SUBLANES
    for k in (1, 2, 4):
        tr, ti = _cmul(pwr[k - 1:k], pwi[k - 1:k], pltpu.roll(xr, k, 0), pltpu.roll(xi, k, 0))
        ok = rowmod >= k
        xr = xr + jnp.where(ok, tr, 0.0)
        xi = xi + jnp.where(ok, ti, 0.0)
    xr_scr[...] = xr
    xi_scr[...] = xi

    def body(r, carry):
        cr, ci = carry
        rows = pl.ds(pl.multiple_of(r * SUBLANES, SUBLANES), SUBLANES)
        tr, ti = _cmul(pwr, pwi, cr, ci)
        nr = xr_scr[rows, :] + tr
        ni = xi_scr[rows, :] + ti
        xr_scr[rows, :] = nr
        xi_scr[rows, :] = ni
        return nr[SUBLANES - 1:SUBLANES], ni[SUBLANES - 1:SUBLANES]

    cr, ci = lax.fori_loop(0, tm // SUBLANES, body, (cr_scr[...], ci_scr[...]))
    cr_scr[...] = cr
    ci_scr[...] = ci
    xfr_ref[0] = cr
    xfi_ref[0] = ci

    y = _dot(xr_scr[...].astype(bf16), bor_ref[0]) - _dot(xi_scr[...].astype(bf16), boi_ref[0])
    y = y + d_ref[...] * u.astype(f32)
    y_ref[...] = _gelu(y).astype(y_ref.dtype)


def _ssm_scan(u_arr, u_col0, prep, d, x0r, x0i, batch, seq, tm):
    pwr, pwi, bdr, bdi, bor, boi = prep
    ngt, _, sl = bdr.shape
    nt = seq // tm
    bin_spec = lambda: pl.BlockSpec((1, LANES, sl), lambda g, b, i: (g, 0, 0))
    bout_spec = lambda: pl.BlockSpec((1, sl, LANES), lambda g, b, i: (g, 0, 0))
    pw_spec = lambda: pl.BlockSpec((SUBLANES, sl), lambda g, b, i: (0, g))
    x0_spec = lambda: pl.BlockSpec((1, sl), lambda g, b, i: (0, g))
    xf_spec = lambda: pl.BlockSpec((1, 1, sl), lambda g, b, i: (b, 0, g))
    return pl.pallas_call(
        _ssm_scan_kernel,
        grid=(ngt, batch, nt),
        in_specs=[pl.BlockSpec((tm, LANES), lambda g, b, i: (b * nt + i, u_col0 + g)),
                  bin_spec(), bin_spec(), bout_spec(), bout_spec(), pw_spec(), pw_spec(),
                  pl.BlockSpec((1, LANES), lambda g, b, i: (0, g)), x0_spec(), x0_spec()],
        out_specs=[pl.BlockSpec((tm, LANES), lambda g, b, i: (b * nt + i, g)), xf_spec(), xf_spec()],
        out_shape=[jax.ShapeDtypeStruct((batch * seq, ngt * LANES), bf16),
                   jax.ShapeDtypeStruct((batch, 1, ngt * sl), f32),
                   jax.ShapeDtypeStruct((batch, 1, ngt * sl), f32)],
        scratch_shapes=[pltpu.VMEM((tm, sl), f32), pltpu.VMEM((tm, sl), f32),
                        pltpu.VMEM((1, sl), f32), pltpu.VMEM((1, sl), f32)],
        compiler_params=_cparams(("parallel", "arbitrary", "arbitrary")),
        name="ssm_scan",
    )(u_arr, bdr, bdi, bor, boi, pwr, pwi, d, x0r, x0i)


def _attn_out_kernel(a_ref, w_ref, g_ref, o_ref):
    y = _dot(a_ref[...], w_ref[...])
    o_ref[...] = (_sigmoid(g_ref[...].astype(f32)) * y).astype(o_ref.dtype)


def _glu_mix_kernel(s_ref, wa_ref, wb_ref, g_ref, ya_ref, o_ref):
    s = s_ref[...]
    ga = _dot(s, wa_ref[...])
    gb = _dot(s, wb_ref[...])
    yb = ga * _sigmoid(gb)
    o_ref[...] = (ya_ref[...].astype(f32) + _sigmoid(g_ref[...].astype(f32)) * yb).astype(o_ref.dtype)


def _out_proj_kernel(m_ref, w_ref, x_ref, o_ref):
    o_ref[...] = x_ref[...] + _dot(m_ref[...], w_ref[...])


def _peer_query_kernel(h_ref, g_ref, wq_ref, kk_ref, hn_ref, st_ref):
    x = h_ref[...]
    ms = jnp.mean(x * x, axis=-1, keepdims=True)
    hn = (x * lax.rsqrt(ms + EPS) * g_ref[...]).astype(bf16)
    hn_ref[...] = hn
    q = _dot(hn, wq_ref[...]).astype(bf16)
    n_heads = kk_ref.shape[0]
    half = kk_ref.shape[3]
    for h in range(n_heads):
        for side in range(2):
            c0 = (2 * h + side) * half
            st_ref[h, side] = _dot_nt(kk_ref[h, side], q[:, c0:c0 + half])


def _peer_query(h1, g, wq, kk, tm):
    t, d = h1.shape
    nh, _, nk, half = kk.shape
    return pl.pallas_call(
        _peer_query_kernel,
        grid=(t // tm,),
        in_specs=[pl.BlockSpec((tm, d), lambda i: (i, 0)),
                  pl.BlockSpec((1, d), lambda i: (0, 0)),
                  pl.BlockSpec(wq.shape, lambda i: (0, 0)),
                  pl.BlockSpec(kk.shape, lambda i: (0, 0, 0, 0))],
        out_specs=[pl.BlockSpec((tm, d), lambda i: (i, 0)),
                   pl.BlockSpec((nh, 2, nk, tm), lambda i: (0, 0, 0, i))],
        out_shape=[jax.ShapeDtypeStruct((t, d), bf16), jax.ShapeDtypeStruct((nh, 2, nk, t), f32)],
        compiler_params=_cparams(("parallel",)),
        name="peer_query",
    )(h1, g, wq, kk)


def _stack_rows(rows):
    k = len(rows)
    idx = lax.broadcasted_iota(jnp.int32, (k, rows[0].shape[1]), 0)
    out = jnp.broadcast_to(rows[0], idx.shape)
    for j in range(1, k):
        out = jnp.where(idx == j, rows[j], out)
    return out


def _top_desc(s, n):
    vals = []
    cur = s
    for _ in range(n):
        v = jnp.max(cur, axis=0, keepdims=True)
        vals.append(v)
        cur = jnp.where(cur == v, -jnp.inf, cur)
    return _stack_rows(vals)


def _peer_topk_kernel(st_ref, th_ref, e1_ref, e2_ref):
    s1 = st_ref[0, 0]
    s2 = st_ref[0, 1]
    v1 = _top_desc(s1, TOPK)
    v2 = _top_desc(s2, TOPK)
    n_b = lambda a: -(-(TOPK // (a + 1)) // SUBLANES) * SUBLANES
    cand = jnp.concatenate([v1[a:a + 1] + v2[:n_b(a)] for a in range(TOPK)], axis=0)
    best = _top_desc(cand, TOPK)
    tau = best[TOPK - 1:TOPK]
    z = jnp.sum(jnp.exp(best - best[0:1]), axis=0, keepdims=True)
    th = jnp.full(s1.shape, jnp.inf, f32)
    for a in range(TOPK):
        sel = (v1[a:a + 1] + v2) >= tau
        th_a = jnp.min(jnp.where(sel, v2, jnp.inf), axis=0, keepdims=True)
        th = jnp.where(s1 == v1[a:a + 1], th_a, th)
    th_ref[0] = th
    e1_ref[0] = jnp.exp(s1 - v1[0:1]) / z
    e2_ref[0] = jnp.exp(s2 - v2[0:1])


def _peer_topk(st, tl):
    nh, _, nk, t = st.shape
    spec = lambda: pl.BlockSpec((1, nk, tl), lambda h, i: (h, 0, i))
    shp = jax.ShapeDtypeStruct((nh, nk, t), f32)
    return pl.pallas_call(
        _peer_topk_kernel,
        grid=(nh, t // tl),
        in_specs=[pl.BlockSpec((1, 2, nk, tl), lambda h, i: (h, 0, 0, i))],
        out_specs=[spec(), spec(), spec()],
        out_shape=[shp, shp, shp],
        compiler_params=_cparams(("parallel", "parallel")),
        name="peer_topk",
    )(st)


def _peer_dense_kernel(hn_ref, u_ref, vt_ref, st_ref, e2_ref, th_ref, e1_ref, h1_ref, o_ref, acc_scr, w_scr):
    e = pl.program_id(1)
    n_heads = st_ref.shape[0]
    nk = st_ref.shape[2]
    eb = u_ref.shape[0]

    @pl.when(e == 0)
    def _():
        acc_scr[...] = jnp.zeros_like(acc_scr)

    a_t = _dot_nt(u_ref[...], hn_ref[...])
    for ii in range(eb // nk):
        i1 = e * (eb // nk) + ii
        gate = None
        for h in range(n_heads):
            thr = th_ref[h, pl.ds(i1, 1), :]
            e1 = e1_ref[h, pl.ds(i1, 1), :]
            term = jnp.where(st_ref[h, 1] >= thr, e2_ref[h], 0.0) * e1
            gate = term if gate is None else gate + term
        w_scr[ii * nk:(ii + 1) * nk, :] = (gate * _gelu(a_t[ii * nk:(ii + 1) * nk])).astype(bf16)
    acc_scr[...] += _dot(vt_ref[...], w_scr[...])

    @pl.when(e == pl.num_programs(1) - 1)
    def _():
        o_ref[...] = h1_ref[...] + acc_scr[...].T


def _peer_dense(hn, u_b, vt_b, st, e2, th, e1, h1, tm, eb):
    t, d = hn.shape
    ne = u_b.shape[0]
    nh, _, nk, _ = st.shape
    tok = lambda: pl.BlockSpec((nh, nk, tm), lambda i, e: (0, 0, i))
    return pl.pallas_call(
        _peer_dense_kernel,
        grid=(t // tm, ne // eb),
        in_specs=[pl.BlockSpec((tm, d), lambda i, e: (i, 0)),
                  pl.BlockSpec((eb, d), lambda i, e: (e, 0)),
                  pl.BlockSpec((d, eb), lambda i, e: (0, e)),
                  pl.BlockSpec((nh, 2, nk, tm), lambda i, e: (0, 0, 0, i)),
                  tok(), tok(), tok(),
                  pl.BlockSpec((tm, d), lambda i, e: (i, 0))],
        out_specs=pl.BlockSpec((tm, d), lambda i, e: (i, 0)),
        out_shape=jax.ShapeDtypeStruct((t, d), f32),
        scratch_shapes=[pltpu.VMEM((d, tm), f32), pltpu.VMEM((eb, tm), bf16)],
        compiler_params=_cparams(("parallel", "arbitrary")),
        name="peer_dense",
    )(hn, u_b, vt_b, st, e2, th, e1, h1)


def _tile(n, pref):
    if n <= pref:
        return n
    t = pref // LANES * LANES
    while n % t:
        t -= LANES
    assert t > 0, (n, pref)
    return t


def kernel(x, meta_tokens, norm1_g, w_in, q_norm_g, k_norm_g, lambda_q1, lambda_k1, lambda_q2, lambda_k2, subln_g, w_attn_branch, ssm_a_re, ssm_a_im, ssm_log_dt, ssm_b_re, ssm_b_im, ssm_c_re, ssm_c_im, ssm_d, w_glu, w_out, norm2_g, peer_w_q, peer_k1, peer_k2, peer_u, peer_v):
    assert norm1_g.shape[0] == 1, "single-layer block only"
    batch, seq, d = x.shape
    n_meta = meta_tokens.shape[0]
    sub = q_norm_g.shape[-1]
    aw = w_attn_branch.shape[1]
    heads = aw // (2 * sub)
    n_groups, n_state, n_chan = ssm_b_re.shape[1:]
    sw = n_groups * n_chan
    assert 2 * sub == LANES and LANES % n_chan == 0 and sw % LANES == 0
    assert n_meta % SUBLANES == 0 and n_meta <= LANES
    t = batch * seq
    n_in = w_in.shape[2]
    assert n_in == 3 * aw + sw + 2 * d

    x2 = x.reshape(t, d)
    w_in_b = w_in[0].astype(bf16)

    proj = _rms_matmul(x2, norm1_g, w_in_b, _tile(t, 1024), _tile(n_in, 1024), bf16, "in_proj")
    proj_m = _rms_matmul(meta_tokens, norm1_g, w_in_b, n_meta, _tile(n_in, 1024), bf16, "in_proj_meta")

    half = sub // 2
    inv_freq = jnp.power(ROPE_THETA, -jnp.arange(half, dtype=f32) / half)
    ang = jnp.arange(n_meta + seq, dtype=f32)[:, None] * inv_freq[None, :]
    cos_t = jnp.tile(jnp.cos(ang), (1, LANES // half))
    sin_t = jnp.tile(jnp.concatenate([-jnp.sin(ang), jnp.sin(ang)], axis=1), (1, LANES // sub))
    lane = jnp.arange(LANES)
    jmat = (lane[:, None] // sub == lane[None, :] // sub).astype(bf16)
    gq = jnp.tile(q_norm_g, (1, aw // sub))
    gk = jnp.tile(k_norm_g, (1, aw // sub))
    tr = _tile(seq, 512)
    q_rot = _normrot(proj, 0, aw, gq, cos_t[n_meta:], sin_t[n_meta:], jmat, tr, sub ** -0.5, sub, "q_normrot")
    k_rot = _normrot(proj, 1, aw, gk, cos_t[n_meta:], sin_t[n_meta:], jmat, tr, 1.0, sub, "k_normrot")
    km_rot = _normrot(proj_m, 1, aw, gk, cos_t[:n_meta], sin_t[:n_meta], jmat, n_meta, 1.0, sub, "k_normrot_meta")
    km = jnp.pad(km_rot, ((0, LANES - n_meta), (0, 0)))
    vm = jnp.pad(proj_m[:, 2 * aw:3 * aw], ((0, LANES - n_meta), (0, 0)))

    att = _attention(q_rot, k_rot, proj, 2 * aw // LANES, km, vm, lambda_q1, lambda_k1, lambda_q2, lambda_k2,
                     subln_g, batch, seq, heads, _tile(seq, 256), n_meta, sub)

    gpt = LANES // n_chan
    ngt = sw // LANES
    sl = gpt * n_state
    gp = n_groups * n_state
    a_re = ssm_a_re.reshape(1, gp)
    a_im = ssm_a_im.reshape(1, gp)
    ldt = jnp.repeat(ssm_log_dt[0], n_state).reshape(1, gp)
    lay_b = lambda b: jnp.tile(b[0].transpose(0, 2, 1).reshape(ngt, LANES, n_state), (1, 1, gpt))
    lay_c = lambda c: jnp.tile(c[0].transpose(0, 2, 1).reshape(ngt, sl, n_chan), (1, 1, gpt))
    prep = _ssm_prep(a_re, a_im, ldt, lay_b(ssm_b_re), lay_b(ssm_b_im), lay_c(ssm_c_re), lay_c(ssm_c_im),
                     n_state, n_chan)
    zeros = jnp.zeros((1, gp), f32)
    u_col0 = 3 * aw // LANES
    _, xmr, xmi = _ssm_scan(proj_m, u_col0, prep, ssm_d, zeros, zeros, 1, n_meta, n_meta)
    ys, _, _ = _ssm_scan(proj, u_col0, prep, ssm_d, xmr[0], xmi[0], batch, seq, _tile(seq, 512))

    tm = _tile(t, 512)
    tn = _tile(d, 1024)
    nj = d // tn
    ga_col0 = (3 * aw + sw) // tn
    gb_col0 = (3 * aw + sw + d) // tn
    grid = (t // tm, nj)
    sem = _cparams(("parallel", "arbitrary"))
    row_full = lambda k: pl.BlockSpec((tm, k), lambda i, j: (i, 0))
    w_col = lambda k, off=0: pl.BlockSpec((k, tn), lambda i, j: (0, j + off))
    out_tile = pl.BlockSpec((tm, tn), lambda i, j: (i, j))
    yag = pl.pallas_call(
        _attn_out_kernel, grid=grid,
        in_specs=[row_full(aw), w_col(aw), pl.BlockSpec((tm, tn), lambda i, j: (i, ga_col0 + j))],
        out_specs=out_tile, out_shape=jax.ShapeDtypeStruct((t, d), bf16), compiler_params=sem, name="attn_out",
    )(att, w_attn_branch[0].astype(bf16), proj)
    w_glu_b = w_glu[0].astype(bf16)
    mix = pl.pallas_call(
        _glu_mix_kernel, grid=grid,
        in_specs=[row_full(sw), w_col(sw), w_col(sw, nj), pl.BlockSpec((tm, tn), lambda i, j: (i, gb_col0 + j)),
                  out_tile],
        out_specs=out_tile, out_shape=jax.ShapeDtypeStruct((t, d), bf16), compiler_params=sem, name="glu_mix",
    )(ys, w_glu_b, w_glu_b, proj, yag)
    h1 = pl.pallas_call(
        _out_proj_kernel, grid=grid,
        in_specs=[row_full(d), w_col(d), out_tile],
        out_specs=out_tile, out_shape=jax.ShapeDtypeStruct((t, d), f32), compiler_params=sem, name="out_proj",
    )(mix, w_out[0].astype(bf16), x2)

    kk = jnp.stack([peer_k1[0], peer_k2[0]], axis=1).astype(bf16)
    assert kk.shape[2] == LANES and kk.shape[3] == LANES
    hn2, st = _peer_query(h1, norm2_g, peer_w_q[0].astype(bf16), kk, _tile(t, 512))
    th, e1, e2 = _peer_topk(st, LANES)
    u_b = peer_u[0].astype(bf16)
    vt_b = peer_v[0].T.astype(bf16)
    out = _peer_dense(hn2, u_b, vt_b, st, e2, th, e1, h1, _tile(t, 512), _tile(u_b.shape[0], 512))
    return out.reshape(batch, seq, d)
```

```python
import functools
import math

import jax
import jax.numpy as jnp
from jax import lax
from jax.experimental import pallas as pl
from jax.experimental.pallas import tpu as pltpu

EPS = 1e-6
ROPE_THETA = 10000.0
LAM_INIT = 0.8 - 0.6 * math.exp(-0.3 * 0)
TOPK = 16
LANES = 128
SUBLANES = 8
NEG_BIG = -1e30
VMEM_LIMIT = 56 * 1024 * 1024

bf16 = jnp.bfloat16
f32 = jnp.float32


def _cparams(sem):
    return pltpu.CompilerParams(dimension_semantics=sem, vmem_limit_bytes=VMEM_LIMIT)


def _dot(a, b):
    return jnp.dot(a, b, preferred_element_type=f32)


def _dot_nt(a, b):
    return lax.dot_general(a, b, (((1,), (1,)), ((), ())), preferred_element_type=f32)


def _sigmoid(x):
    return 1.0 / (1.0 + jnp.exp(-x))


def _gelu(x):
    return 0.5 * x * (1.0 + lax.erf(x * (1.0 / math.sqrt(2.0))))


def _stack_rows(rows):
    k = len(rows)
    idx = lax.broadcasted_iota(jnp.int32, (k, rows[0].shape[1]), 0)
    out = jnp.broadcast_to(rows[0], idx.shape)
    for j in range(1, k):
        out = jnp.where(idx == j, rows[j], out)
    return out


def _rms_matmul_kernel(x_ref, g_ref, w_ref, o_ref, xn_ref):
    @pl.when(pl.program_id(1) == 0)
    def _():
        x = x_ref[...]
        ms = jnp.mean(x * x, axis=-1, keepdims=True)
        xn_ref[...] = (x * lax.rsqrt(ms + EPS) * g_ref[...]).astype(bf16)

    o_ref[...] = _dot(xn_ref[...], w_ref[...]).astype(o_ref.dtype)


def _rms_matmul(x, g, w, tm, tn, out_dtype, name):
    m, k = x.shape
    n = w.shape[1]
    return pl.pallas_call(
        _rms_matmul_kernel,
        grid=(m // tm, n // tn),
        in_specs=[pl.BlockSpec((tm, k), lambda i, j: (i, 0)),
                  pl.BlockSpec((1, k), lambda i, j: (0, 0)),
                  pl.BlockSpec((k, tn), lambda i, j: (0, j))],
        out_specs=pl.BlockSpec((tm, tn), lambda i, j: (i, j)),
        out_shape=jax.ShapeDtypeStruct((m, n), out_dtype),
        scratch_shapes=[pltpu.VMEM((tm, k), bf16)],
        compiler_params=_cparams(("parallel", "arbitrary")),
        name=name,
    )(x, g, w)


def _normrot_kernel(x_ref, g_ref, cos_ref, sin_ref, j_ref, o_ref, *, scale, sub):
    width = x_ref.shape[1]
    cos = cos_ref[...]
    sin = sin_ref[...]
    jmat = j_ref[...]
    lane = lax.broadcasted_iota(jnp.int32, cos.shape, 1)
    first_half = (lane % sub) < (sub // 2)
    for t in range(width // LANES):
        sl = slice(t * LANES, (t + 1) * LANES)
        x = x_ref[:, sl].astype(f32)
        x2 = x * x
        hi = x2.astype(bf16)
        lo = (x2 - hi.astype(f32)).astype(bf16)
        ssq = _dot(hi, jmat) + _dot(lo, jmat)
        xn = x * lax.rsqrt(ssq * (1.0 / sub) + EPS) * g_ref[:, sl]
        partner = jnp.where(first_half, pltpu.roll(xn, LANES - sub // 2, 1), pltpu.roll(xn, sub // 2, 1))
        o_ref[:, sl] = ((xn * cos + partner * sin) * scale).astype(o_ref.dtype)


def _normrot(x, col_block, width, g_t, cos, sin, jmat, tm, scale, sub, name):
    m = x.shape[0]
    nt = cos.shape[0] // tm
    return pl.pallas_call(
        functools.partial(_normrot_kernel, scale=scale, sub=sub),
        grid=(m // tm,),
        in_specs=[pl.BlockSpec((tm, width), lambda i: (i, col_block)),
                  pl.BlockSpec((1, width), lambda i: (0, 0)),
                  pl.BlockSpec((tm, LANES), lambda i: (i % nt, 0)),
                  pl.BlockSpec((tm, LANES), lambda i: (i % nt, 0)),
                  pl.BlockSpec((LANES, LANES), lambda i: (0, 0))],
        out_specs=pl.BlockSpec((tm, width), lambda i: (i, 0)),
        out_shape=jax.ShapeDtypeStruct((m, width), bf16),
        compiler_params=_cparams(("parallel",)),
        name=name,
    )(x, g_t, cos, sin, jmat)


def _attn_kernel(q_ref, k_ref, v_ref, km_ref, vm_ref, lq1_ref, lk1_ref, lq2_ref, lk2_ref, sg_ref,
                 o_ref, qq_scr, m_scr, acc_scr, *, n_meta, sub, hpb):
    i = pl.program_id(2)
    tq = q_ref.shape[0]
    tk = tq
    hw = 2 * sub
    mp = km_ref.shape[0]
    lane = lax.broadcasted_iota(jnp.int32, (tq, hw), 1)
    head = lambda hh: slice(hh * hw, (hh + 1) * hw)

    def softmax_step(hh, s, v_aug, first):
        smax = jnp.max(s, axis=1, keepdims=True)
        if first:
            m_new = jnp.broadcast_to(smax, (2 * tq, hw))
        else:
            m_prev = m_scr[hh]
            m_new = jnp.maximum(m_prev, smax)
        p = jnp.concatenate([jnp.exp(s[:, c * hw:(c + 1) * hw] - m_new) for c in range(s.shape[1] // hw)], axis=1)
        pv = _dot(p.astype(bf16), v_aug)
        if first:
            acc_scr[hh] = pv
        else:
            alpha = jnp.exp(m_prev - m_new)
            acc_scr[hh] = jnp.concatenate([alpha, alpha], axis=1) * acc_scr[hh] + pv
        m_scr[hh] = m_new

    for hh in range(hpb):
        q = q_ref[:, head(hh)]
        zero = jnp.zeros_like(q)
        qq_scr[hh] = jnp.concatenate([jnp.where(lane < sub, q, zero), jnp.where(lane >= sub, q, zero)], axis=0)
        s = _dot_nt(qq_scr[hh], km_ref[:, head(hh)])
        col = lax.broadcasted_iota(jnp.int32, s.shape, 1)
        s = jnp.where(col < n_meta, s, NEG_BIG)
        softmax_step(hh, s, jnp.concatenate([vm_ref[:, head(hh)], jnp.ones((mp, hw), bf16)], axis=1), True)

    def block(j, masked):
        start = pl.multiple_of(j * tk, tk)
        for hh in range(hpb):
            kb = k_ref[pl.ds(start, tk), head(hh)]
            vb = v_ref[pl.ds(start, tk), head(hh)]
            s = _dot_nt(qq_scr[hh], kb)
            if masked:
                row = lax.broadcasted_iota(jnp.int32, s.shape, 0) % tq
                col = lax.broadcasted_iota(jnp.int32, s.shape, 1)
                s = jnp.where(col <= row, s, NEG_BIG)
            softmax_step(hh, s, jnp.concatenate([vb, jnp.ones((tk, hw), bf16)], axis=1), False)

    def body(j, c):
        block(j, False)
        return c

    lax.fori_loop(0, i, body, 0)
    block(i, True)

    lam = (jnp.exp(jnp.sum(lq1_ref[...] * lk1_ref[...], axis=1, keepdims=True))
           - jnp.exp(jnp.sum(lq2_ref[...] * lk2_ref[...], axis=1, keepdims=True)) + LAM_INIT)
    for hh in range(hpb):
        acc = acc_scr[hh]
        o = acc[:, :hw] / acc[:, hw:]
        att = o[:tq] - lam * o[tq:]
        ms = jnp.mean(att * att, axis=-1, keepdims=True)
        att = att * lax.rsqrt(ms + EPS) * sg_ref[...] * (1.0 - LAM_INIT)
        o_ref[:, head(hh)] = att.astype(o_ref.dtype)


def _attention(q_rot, k_rot, proj, v_col0, km, vm, lq1, lk1, lq2, lk2, sg, batch, seq, heads, tq, n_meta, sub):
    t, aw = q_rot.shape
    nq = seq // tq
    hw = 2 * sub
    hpb = 2 if heads % 2 == 0 else 1
    bw = hpb * hw
    mp = km.shape[0]
    vec = lambda: pl.BlockSpec((1, sub), lambda b, h, i: (0, 0))
    return pl.pallas_call(
        functools.partial(_attn_kernel, n_meta=n_meta, sub=sub, hpb=hpb),
        grid=(batch, heads // hpb, nq),
        in_specs=[pl.BlockSpec((tq, bw), lambda b, h, i: (b * nq + i, h)),
                  pl.BlockSpec((seq, bw), lambda b, h, i: (b, h)),
                  pl.BlockSpec((seq, bw), lambda b, h, i: (b, v_col0 // hpb + h)),
                  pl.BlockSpec((mp, bw), lambda b, h, i: (0, h)),
                  pl.BlockSpec((mp, bw), lambda b, h, i: (0, h)),
                  vec(), vec(), vec(), vec(),
                  pl.BlockSpec((1, hw), lambda b, h, i: (0, 0))],
        out_specs=pl.BlockSpec((tq, bw), lambda b, h, i: (b * nq + i, h)),
        out_shape=jax.ShapeDtypeStruct((t, aw), bf16),
        scratch_shapes=[pltpu.VMEM((hpb, 2 * tq, hw), bf16), pltpu.VMEM((hpb, 2 * tq, hw), f32),
                        pltpu.VMEM((hpb, 2 * tq, 2 * hw), f32)],
        compiler_params=_cparams(("parallel", "parallel", "arbitrary")),
        name="diff_attention",
    )(q_rot, k_rot, proj, km, vm, lq1, lk1, lq2, lk2, sg)


def _cmul(ar, ai, br, bi):
    return ar * br - ai * bi, ar * bi + ai * br


def _ssm_prep_kernel(are_ref, aim_ref, ldt_ref, bre_ref, bim_ref, cre_ref, cim_ref,
                     pwr_ref, pwi_ref, bdr_ref, bdi_ref, bor_ref, boi_ref, *, n_state, n_chan):
    a_re = are_ref[...]
    a_im = aim_ref[...]
    dt = jnp.exp(ldt_ref[...])
    er = jnp.exp(a_re * dt)
    ab_re = er * jnp.cos(a_im * dt)
    ab_im = er * jnp.sin(a_im * dt)
    nr = ab_re - 1.0
    ni = ab_im
    den = a_re * a_re + a_im * a_im
    f_re = (nr * a_re + ni * a_im) / den
    f_im = (ni * a_re - nr * a_im) / den
    p1 = (ab_re, ab_im)
    p2 = _cmul(*p1, *p1)
    p3 = _cmul(*p2, *p1)
    p4 = _cmul(*p2, *p2)
    p5 = _cmul(*p4, *p1)
    p6 = _cmul(*p4, *p2)
    p7 = _cmul(*p4, *p3)
    p8 = _cmul(*p4, *p4)
    pw = (p1, p2, p3, p4, p5, p6, p7, p8)
    pwr_ref[...] = _stack_rows([p[0] for p in pw])
    pwi_ref[...] = _stack_rows([p[1] for p in pw])
    b_re = bre_ref[0]
    b_im = bim_ref[0]
    r = lax.broadcasted_iota(jnp.int32, b_re.shape, 0) // n_chan
    c = lax.broadcasted_iota(jnp.int32, b_re.shape, 1) // n_state
    keep = r == c
    bdr_ref[0] = jnp.where(keep, f_re * b_re - f_im * b_im, 0.0).astype(bf16)
    bdi_ref[0] = jnp.where(keep, f_re * b_im + f_im * b_re, 0.0).astype(bf16)
    c_re = cre_ref[0]
    c_im = cim_ref[0]
    r = lax.broadcasted_iota(jnp.int32, c_re.shape, 0) // n_state
    c = lax.broadcasted_iota(jnp.int32, c_re.shape, 1) // n_chan
    keep = r == c
    bor_ref[0] = jnp.where(keep, c_re, 0.0).astype(bf16)
    boi_ref[0] = jnp.where(keep, c_im, 0.0).astype(bf16)


def _ssm_prep(a_re, a_im, ldt, b_re_t, b_im_t, c_re_t, c_im_t, n_state, n_chan):
    ngt, _, sl = b_re_t.shape
    gp = a_re.shape[1]
    lane_vec = lambda: pl.BlockSpec((1, sl), lambda g: (0, g))
    bin_spec = lambda: pl.BlockSpec((1, LANES, sl), lambda g: (g, 0, 0))
    bout_spec = lambda: pl.BlockSpec((1, sl, LANES), lambda g: (g, 0, 0))
    return pl.pallas_call(
        functools.partial(_ssm_prep_kernel, n_state=n_state, n_chan=n_chan),
        grid=(ngt,),
        in_specs=[lane_vec(), lane_vec(), lane_vec(), bin_spec(), bin_spec(), bout_spec(), bout_spec()],
        out_specs=[pl.BlockSpec((SUBLANES, sl), lambda g: (0, g)), pl.BlockSpec((SUBLANES, sl), lambda g: (0, g)),
                   bin_spec(), bin_spec(), bout_spec(), bout_spec()],
        out_shape=[jax.ShapeDtypeStruct((SUBLANES, gp), f32), jax.ShapeDtypeStruct((SUBLANES, gp), f32),
                   jax.ShapeDtypeStruct(b_re_t.shape, bf16), jax.ShapeDtypeStruct(b_re_t.shape, bf16),
                   jax.ShapeDtypeStruct(c_re_t.shape, bf16), jax.ShapeDtypeStruct(c_re_t.shape, bf16)],
        compiler_params=_cparams(("parallel",)),
        name="ssm_prep",
    )(a_re, a_im, ldt, b_re_t, b_im_t, c_re_t, c_im_t)


def _ssm_scan_kernel(u_ref, bdr_ref, bdi_ref, bor_ref, boi_ref, pwr_ref, pwi_ref, d_ref, x0r_ref, x0i_ref,
                     y_ref, xfr_ref, xfi_ref, xr_scr, xi_scr, cr_scr, ci_scr):
    i = pl.program_id(2)
    tm = u_ref.shape[0]

    @pl.when(i == 0)
    def _():
        cr_scr[...] = x0r_ref[...]
        ci_scr[...] = x0i_ref[...]

    u = u_ref[...]
    xr = _dot(u, bdr_ref[0])
    xi = _dot(u, bdi_ref[0])
    pwr = pwr_ref[...]
    pwi = pwi_ref[...]
    rowmod = lax.broadcasted_iota(jnp.int32, xr.shape, 0) % SUBLANES
    for k in (1, 2, 4):
        tr, ti = _cmul(pwr[k - 1:k], pwi[k - 1:k], pltpu.roll(xr, k, 0), pltpu.roll(xi, k, 0))
        ok = rowmod >= k
        xr = xr + jnp.where(ok, tr, 0.0)
        xi = xi + jnp.where(ok, ti, 0.0)
    xr_scr[...] = xr
    xi_scr[...] = xi

    def body(r, carry):
        cr, ci = carry
        rows = pl.ds(pl.multiple_of(r * SUBLANES, SUBLANES), SUBLANES)
        tr, ti = _cmul(pwr, pwi, cr, ci)
        nr = xr_scr[rows, :] + tr
        ni = xi_scr[rows, :] + ti
        xr_scr[rows, :] = nr
        xi_scr[rows, :] = ni
        return nr[SUBLANES - 1:SUBLANES], ni[SUBLANES - 1:SUBLANES]

    cr, ci = lax.fori_loop(0, tm // SUBLANES, body, (cr_scr[...], ci_scr[...]))
    cr_scr[...] = cr
    ci_scr[...] = ci
    xfr_ref[0] = cr
    xfi_ref[0] = ci

    y = _dot(xr_scr[...].astype(bf16), bor_ref[0]) - _dot(xi_scr[...].astype(bf16), boi_ref[0])
    y = y + d_ref[...] * u.astype(f32)
    y_ref[...] = _gelu(y).astype(y_ref.dtype)


def _ssm_scan(u_arr, u_col0, prep, d, x0r, x0i, batch, seq, tm):
    pwr, pwi, bdr, bdi, bor, boi = prep
    ngt, _, sl = bdr.shape
    nt = seq // tm
    bin_spec = lambda: pl.BlockSpec((1, LANES, sl), lambda g, b, i: (g, 0, 0))
    bout_spec = lambda: pl.BlockSpec((1, sl, LANES), lambda g, b, i: (g, 0, 0))
    pw_spec = lambda: pl.BlockSpec((SUBLANES, sl), lambda g, b, i: (0, g))
    x0_spec = lambda: pl.BlockSpec((1, sl), lambda g, b, i: (0, g))
    xf_spec = lambda: pl.BlockSpec((1, 1, sl), lambda g, b, i: (b, 0, g))
    return pl.pallas_call(
        _ssm_scan_kernel,
        grid=(ngt, batch, nt),
        in_specs=[pl.BlockSpec((tm, LANES), lambda g, b, i: (b * nt + i, u_col0 + g)),
                  bin_spec(), bin_spec(), bout_spec(), bout_spec(), pw_spec(), pw_spec(),
                  pl.BlockSpec((1, LANES), lambda g, b, i: (0, g)), x0_spec(), x0_spec()],
        out_specs=[pl.BlockSpec((tm, LANES), lambda g, b, i: (b * nt + i, g)), xf_spec(), xf_spec()],
        out_shape=[jax.ShapeDtypeStruct((batch * seq, ngt * LANES), bf16),
                   jax.ShapeDtypeStruct((batch, 1, ngt * sl), f32),
                   jax.ShapeDtypeStruct((batch, 1, ngt * sl), f32)],
        scratch_shapes=[pltpu.VMEM((tm, sl), f32), pltpu.VMEM((tm, sl), f32),
                        pltpu.VMEM((1, sl), f32), pltpu.VMEM((1, sl), f32)],
        compiler_params=_cparams(("parallel", "arbitrary", "arbitrary")),
        name="ssm_scan",
    )(u_arr, bdr, bdi, bor, boi, pwr, pwi, d, x0r, x0i)


def _attn_out_kernel(a_ref, w_ref, g_ref, o_ref):
    y = _dot(a_ref[...], w_ref[...])
    o_ref[...] = (_sigmoid(g_ref[...].astype(f32)) * y).astype(o_ref.dtype)


def _glu_mix_kernel(s_ref, wa_ref, wb_ref, g_ref, ya_ref, o_ref):
    s = s_ref[...]
    ga = _dot(s, wa_ref[...])
    gb = _dot(s, wb_ref[...])
    yb = ga * _sigmoid(gb)
    o_ref[...] = (ya_ref[...].astype(f32) + _sigmoid(g_ref[...].astype(f32)) * yb).astype(o_ref.dtype)


def _out_proj_kernel(m_ref, w_ref, x_ref, o_ref):
    o_ref[...] = x_ref[...] + _dot(m_ref[...], w_ref[...])


def _peer_query_kernel(h_ref, g_ref, wq_ref, kk_ref, hn_ref, st_ref):
    x = h_ref[...]
    ms = jnp.mean(x * x, axis=-1, keepdims=True)
    hn = (x * lax.rsqrt(ms + EPS) * g_ref[...]).astype(bf16)
    hn_ref[...] = hn
    q = _dot(hn, wq_ref[...]).astype(bf16)
    n_heads = kk_ref.shape[0]
    half = kk_ref.shape[3]
    for h in range(n_heads):
        for side in range(2):
            c0 = (2 * h + side) * half
            st_ref[h, side] = _dot_nt(kk_ref[h, side], q[:, c0:c0 + half])


def _peer_query(h1, g, wq, kk, tm):
    t, d = h1.shape
    nh, _, nk, half = kk.shape
    return pl.pallas_call(
        _peer_query_kernel,
        grid=(t // tm,),
        in_specs=[pl.BlockSpec((tm, d), lambda i: (i, 0)),
                  pl.BlockSpec((1, d), lambda i: (0, 0)),
                  pl.BlockSpec(wq.shape, lambda i: (0, 0)),
                  pl.BlockSpec(kk.shape, lambda i: (0, 0, 0, 0))],
        out_specs=[pl.BlockSpec((tm, d), lambda i: (i, 0)),
                   pl.BlockSpec((nh, 2, nk, tm), lambda i: (0, 0, 0, i))],
        out_shape=[jax.ShapeDtypeStruct((t, d), bf16), jax.ShapeDtypeStruct((nh, 2, nk, t), f32)],
        compiler_params=_cparams(("parallel",)),
        name="peer_query",
    )(h1, g, wq, kk)


def _top_desc(s, n):
    vals = []
    cur = s
    rank = jnp.full(s.shape, float(n), f32)
    for a in range(n):
        v = jnp.max(cur, axis=0, keepdims=True)
        vals.append(v)
        hit = cur == v
        rank = jnp.where(hit, float(a), rank)
        cur = jnp.where(hit, -jnp.inf, cur)
    return _stack_rows(vals), rank


def _peer_topk_kernel(st_ref, r2_ref, e2_ref, n1_ref, e1_ref):
    chunk = LANES
    for c in range(st_ref.shape[3] // chunk):
        cols = slice(c * chunk, (c + 1) * chunk)
        s1 = st_ref[0, 0, :, cols]
        s2 = st_ref[0, 1, :, cols]
        v1, rank1 = _top_desc(s1, TOPK)
        v2, rank2 = _top_desc(s2, TOPK)
        n_b = lambda a: -(-(TOPK // (a + 1)) // SUBLANES) * SUBLANES
        cand = jnp.concatenate([v1[a:a + 1] + v2[:n_b(a)] for a in range(TOPK)], axis=0)
        best, _ = _top_desc(cand, TOPK)
        tau = best[TOPK - 1:TOPK]
        z = jnp.sum(jnp.exp(best - best[0:1]), axis=0, keepdims=True)
        n1 = jnp.zeros(s1.shape, f32)
        for a in range(TOPK):
            cnt = jnp.sum(jnp.where((v1[a:a + 1] + v2) >= tau, 1.0, 0.0), axis=0, keepdims=True)
            n1 = jnp.where(rank1 == float(a), cnt, n1)
        r2_ref[0, :, cols] = rank2.astype(bf16)
        e2_ref[0, :, cols] = jnp.exp(s2 - v2[0:1]).astype(bf16)
        n1_ref[0, :, cols] = n1
        e1_ref[0, :, cols] = jnp.exp(s1 - v1[0:1]) / z


def _peer_topk(st, tl):
    nh, _, nk, t = st.shape
    spec = lambda: pl.BlockSpec((1, nk, tl), lambda h, i: (h, 0, i))
    shp = lambda dt: jax.ShapeDtypeStruct((nh, nk, t), dt)
    return pl.pallas_call(
        _peer_topk_kernel,
        grid=(nh, t // tl),
        in_specs=[pl.BlockSpec((1, 2, nk, tl), lambda h, i: (h, 0, 0, i))],
        out_specs=[spec(), spec(), spec(), spec()],
        out_shape=[shp(bf16), shp(bf16), shp(f32), shp(f32)],
        compiler_params=_cparams(("parallel", "parallel")),
        name="peer_topk",
    )(st)


def _peer_dense_kernel(hn_ref, u_ref, vt_ref, r2_ref, e2_ref, n1_ref, e1_ref, h1_ref, o_ref, acc_scr, w_scr):
    e = pl.program_id(1)
    nb = pl.num_programs(1) - 1
    n_heads, nk, tm = r2_ref.shape
    eb = u_ref.shape[0]
    pk = 2 * SUBLANES

    @pl.when(e == 0)
    def _():
        acc_scr[...] = jnp.zeros_like(acc_scr)
        w_scr[1] = jnp.zeros((eb, tm), bf16)

    cur = e % 2
    acc_scr[...] += _dot(vt_ref[...], w_scr[1 - cur])

    blk = jnp.minimum(e, nb - 1)
    a_t = _dot_nt(u_ref[...], hn_ref[...])
    for ii in range(eb // nk):
        i1 = blk * (eb // nk) + ii
        gate = None
        for h in range(n_heads):
            n1 = jnp.broadcast_to(n1_ref[h, pl.ds(i1, 1), :], (pk, tm)).astype(bf16)
            e1 = jnp.broadcast_to(e1_ref[h, pl.ds(i1, 1), :], (pk, tm)).astype(bf16)
            r2 = r2_ref[h].reshape(nk // pk, pk, tm)
            e2 = e2_ref[h].reshape(nk // pk, pk, tm)
            term = jnp.where(r2 < n1[None], e2, jnp.zeros_like(e2)) * e1[None]
            gate = term if gate is None else gate + term
        g = _gelu(a_t[ii * nk:(ii + 1) * nk]).astype(bf16)
        w_scr[cur, ii * nk:(ii + 1) * nk, :] = gate.reshape(nk, tm) * g

    @pl.when(e == nb)
    def _():
        o_ref[...] = h1_ref[...] + acc_scr[...].T


def _peer_dense(hn, u_b, vt_b, r2, e2, n1, e1, h1, tm, eb):
    t, d = hn.shape
    nb = u_b.shape[0] // eb
    nh, nk, _ = r2.shape
    tok = lambda: pl.BlockSpec((nh, nk, tm), lambda i, e: (0, 0, i))
    return pl.pallas_call(
        _peer_dense_kernel,
        grid=(t // tm, nb + 1),
        in_specs=[pl.BlockSpec((tm, d), lambda i, e: (i, 0)),
                  pl.BlockSpec((eb, d), lambda i, e: (jnp.minimum(e, nb - 1), 0)),
                  pl.BlockSpec((d, eb), lambda i, e: (0, jnp.maximum(e - 1, 0))),
                  tok(), tok(), tok(), tok(),
                  pl.BlockSpec((tm, d), lambda i, e: (i, 0))],
        out_specs=pl.BlockSpec((tm, d), lambda i, e: (i, 0)),
        out_shape=jax.ShapeDtypeStruct((t, d), f32),
        scratch_shapes=[pltpu.VMEM((d, tm), f32), pltpu.VMEM((2, eb, tm), bf16)],
        compiler_params=_cparams(("parallel", "arbitrary")),
        name="peer_dense",
    )(hn, u_b, vt_b, r2, e2, n1, e1, h1)


def _tile(n, pref):
    if n <= pref:
        return n
    t = pref // LANES * LANES
    while n % t:
        t -= LANES
    assert t > 0, (n, pref)
    return t


def kernel(x, meta_tokens, norm1_g, w_in, q_norm_g, k_norm_g, lambda_q1, lambda_k1, lambda_q2, lambda_k2, subln_g, w_attn_branch, ssm_a_re, ssm_a_im, ssm_log_dt, ssm_b_re, ssm_b_im, ssm_c_re, ssm_c_im, ssm_d, w_glu, w_out, norm2_g, peer_w_q, peer_k1, peer_k2, peer_u, peer_v):
    assert norm1_g.shape[0] == 1, "single-layer block only"
    batch, seq, d = x.shape
    n_meta = meta_tokens.shape[0]
    sub = q_norm_g.shape[-1]
    aw = w_attn_branch.shape[1]
    heads = aw // (2 * sub)
    n_groups, n_state, n_chan = ssm_b_re.shape[1:]
    sw = n_groups * n_chan
    assert 2 * sub == LANES and LANES % n_chan == 0 and sw % LANES == 0
    assert n_meta % SUBLANES == 0 and n_meta <= LANES
    t = batch * seq
    n_in = w_in.shape[2]
    assert n_in == 3 * aw + sw + 2 * d

    x2 = x.reshape(t, d)
    w_in_b = w_in[0].astype(bf16)

    proj = _rms_matmul(x2, norm1_g, w_in_b, _tile(t, 1024), _tile(n_in, 1024), bf16, "in_proj")
    proj_m = _rms_matmul(meta_tokens, norm1_g, w_in_b, n_meta, _tile(n_in, 1024), bf16, "in_proj_meta")

    half = sub // 2
    inv_freq = jnp.power(ROPE_THETA, -jnp.arange(half, dtype=f32) / half)
    ang = jnp.arange(n_meta + seq, dtype=f32)[:, None] * inv_freq[None, :]
    cos_t = jnp.tile(jnp.cos(ang), (1, LANES // half))
    sin_t = jnp.tile(jnp.concatenate([-jnp.sin(ang), jnp.sin(ang)], axis=1), (1, LANES // sub))
    lane = jnp.arange(LANES)
    jmat = (lane[:, None] // sub == lane[None, :] // sub).astype(bf16)
    gq = jnp.tile(q_norm_g, (1, aw // sub))
    gk = jnp.tile(k_norm_g, (1, aw // sub))
    tr = _tile(seq, 512)
    q_rot = _normrot(proj, 0, aw, gq, cos_t[n_meta:], sin_t[n_meta:], jmat, tr, sub ** -0.5, sub, "q_normrot")
    k_rot = _normrot(proj, 1, aw, gk, cos_t[n_meta:], sin_t[n_meta:], jmat, tr, 1.0, sub, "k_normrot")
    km_rot = _normrot(proj_m, 1, aw, gk, cos_t[:n_meta], sin_t[:n_meta], jmat, n_meta, 1.0, sub, "k_normrot_meta")
    km = jnp.pad(km_rot, ((0, LANES - n_meta), (0, 0)))
    vm = jnp.pad(proj_m[:, 2 * aw:3 * aw], ((0, LANES - n_meta), (0, 0)))

    att = _attention(q_rot, k_rot, proj, 2 * aw // LANES, km, vm, lambda_q1, lambda_k1, lambda_q2, lambda_k2,
                     subln_g, batch, seq, heads, _tile(seq, 512), n_meta, sub)

    gpt = LANES // n_chan
    ngt = sw // LANES
    sl = gpt * n_state
    gp = n_groups * n_state
    a_re = ssm_a_re.reshape(1, gp)
    a_im = ssm_a_im.reshape(1, gp)
    ldt = jnp.repeat(ssm_log_dt[0], n_state).reshape(1, gp)
    lay_b = lambda b: jnp.tile(b[0].transpose(0, 2, 1).reshape(ngt, LANES, n_state), (1, 1, gpt))
    lay_c = lambda c: jnp.tile(c[0].transpose(0, 2, 1).reshape(ngt, sl, n_chan), (1, 1, gpt))
    prep = _ssm_prep(a_re, a_im, ldt, lay_b(ssm_b_re), lay_b(ssm_b_im), lay_c(ssm_c_re), lay_c(ssm_c_im),
                     n_state, n_chan)
    zeros = jnp.zeros((1, gp), f32)
    u_col0 = 3 * aw // LANES
    _, xmr, xmi = _ssm_scan(proj_m, u_col0, prep, ssm_d, zeros, zeros, 1, n_meta, n_meta)
    ys, _, _ = _ssm_scan(proj, u_col0, prep, ssm_d, xmr[0], xmi[0], batch, seq, _tile(seq, 512))

    tm = _tile(t, 512)
    tn = _tile(d, 1024)
    nj = d // tn
    ga_col0 = (3 * aw + sw) // tn
    gb_col0 = (3 * aw + sw + d) // tn
    grid = (t // tm, nj)
    sem = _cparams(("parallel", "arbitrary"))
    row_full = lambda k: pl.BlockSpec((tm, k), lambda i, j: (i, 0))
    w_col = lambda k, off=0: pl.BlockSpec((k, tn), lambda i, j: (0, j + off))
    out_tile = pl.BlockSpec((tm, tn), lambda i, j: (i, j))
    yag = pl.pallas_call(
        _attn_out_kernel, grid=grid,
        in_specs=[row_full(aw), w_col(aw), pl.BlockSpec((tm, tn), lambda i, j: (i, ga_col0 + j))],
        out_specs=out_tile, out_shape=jax.ShapeDtypeStruct((t, d), bf16), compiler_params=sem, name="attn_out",
    )(att, w_attn_branch[0].astype(bf16), proj)
    w_glu_b = w_glu[0].astype(bf16)
    mix = pl.pallas_call(
        _glu_mix_kernel, grid=grid,
        in_specs=[row_full(sw), w_col(sw), w_col(sw, nj), pl.BlockSpec((tm, tn), lambda i, j: (i, gb_col0 + j)),
                  out_tile],
        out_specs=out_tile, out_shape=jax.ShapeDtypeStruct((t, d), bf16), compiler_params=sem, name="glu_mix",
    )(ys, w_glu_b, w_glu_b, proj, yag)
    h1 = pl.pallas_call(
        _out_proj_kernel, grid=grid,
        in_specs=[row_full(d), w_col(d), out_tile],
        out_specs=out_tile, out_shape=jax.ShapeDtypeStruct((t, d), f32), compiler_params=sem, name="out_proj",
    )(mix, w_out[0].astype(bf16), x2)

    kk = jnp.stack([peer_k1[0], peer_k2[0]], axis=1).astype(bf16)
    assert kk.shape[2] == LANES and kk.shape[3] == LANES
    hn2, st = _peer_query(h1, norm2_g, peer_w_q[0].astype(bf16), kk, _tile(t, 512))
    r2, e2, n1, e1 = _peer_topk(st, _tile(t, 512))
    u_b = peer_u[0].astype(bf16)
    vt_b = peer_v[0].T.astype(bf16)
    out = _peer_dense(hn2, u_b, vt_b, r2, e2, n1, e1, h1, _tile(t, 512), _tile(u_b.shape[0], 512))
    return out.reshape(batch, seq, d)
```

```python
import functools
import math

import jax
import jax.numpy as jnp
from jax import lax
from jax.experimental import pallas as pl
from jax.experimental.pallas import tpu as pltpu

EPS = 1e-6
ROPE_THETA = 10000.0
LAM_INIT = 0.8 - 0.6 * math.exp(-0.3 * 0)
TOPK = 16
LANES = 128
SUBLANES = 8
NEG_BIG = -1e30
VMEM_LIMIT = 56 * 1024 * 1024

bf16 = jnp.bfloat16
f32 = jnp.float32


def _cparams(sem):
    return pltpu.CompilerParams(dimension_semantics=sem, vmem_limit_bytes=VMEM_LIMIT)


def _dot(a, b):
    return jnp.dot(a, b, preferred_element_type=f32)


def _dot_nt(a, b):
    return lax.dot_general(a, b, (((1,), (1,)), ((), ())), preferred_element_type=f32)


def _sigmoid(x):
    return 1.0 / (1.0 + jnp.exp(-x))


def _gelu(x):
    return 0.5 * x * (1.0 + lax.erf(x * (1.0 / math.sqrt(2.0))))


def _stack_rows(rows):
    k = len(rows)
    idx = lax.broadcasted_iota(jnp.int32, (k, rows[0].shape[1]), 0)
    out = jnp.broadcast_to(rows[0], idx.shape)
    for j in range(1, k):
        out = jnp.where(idx == j, rows[j], out)
    return out


def _rms_matmul_kernel(x_ref, g_ref, w_ref, o_ref, xn_ref):
    @pl.when(pl.program_id(1) == 0)
    def _():
        x = x_ref[...]
        ms = jnp.mean(x * x, axis=-1, keepdims=True)
        xn_ref[...] = (x * lax.rsqrt(ms + EPS) * g_ref[...]).astype(bf16)

    o_ref[...] = _dot(xn_ref[...], w_ref[...]).astype(o_ref.dtype)


def _rms_matmul(x, g, w, tm, tn, out_dtype, name):
    m, k = x.shape
    n = w.shape[1]
    return pl.pallas_call(
        _rms_matmul_kernel,
        grid=(m // tm, n // tn),
        in_specs=[pl.BlockSpec((tm, k), lambda i, j: (i, 0)),
                  pl.BlockSpec((1, k), lambda i, j: (0, 0)),
                  pl.BlockSpec((k, tn), lambda i, j: (0, j))],
        out_specs=pl.BlockSpec((tm, tn), lambda i, j: (i, j)),
        out_shape=jax.ShapeDtypeStruct((m, n), out_dtype),
        scratch_shapes=[pltpu.VMEM((tm, k), bf16)],
        compiler_params=_cparams(("parallel", "arbitrary")),
        name=name,
    )(x, g, w)


def _normrot_kernel(x_ref, g_ref, cos_ref, sin_ref, j_ref, o_ref, *, scale, sub):
    width = x_ref.shape[1]
    cos = cos_ref[...]
    sin = sin_ref[...]
    jmat = j_ref[...]
    lane = lax.broadcasted_iota(jnp.int32, cos.shape, 1)
    first_half = (lane % sub) < (sub // 2)
    for t in range(width // LANES):
        sl = slice(t * LANES, (t + 1) * LANES)
        x = x_ref[:, sl].astype(f32)
        x2 = x * x
        hi = x2.astype(bf16)
        lo = (x2 - hi.astype(f32)).astype(bf16)
        ssq = _dot(hi, jmat) + _dot(lo, jmat)
        xn = x * lax.rsqrt(ssq * (1.0 / sub) + EPS) * g_ref[:, sl]
        partner = jnp.where(first_half, pltpu.roll(xn, LANES - sub // 2, 1), pltpu.roll(xn, sub // 2, 1))
        o_ref[:, sl] = ((xn * cos + partner * sin) * scale).astype(o_ref.dtype)


def _normrot(x, col_block, width, g_t, cos, sin, jmat, tm, scale, sub, name):
    m = x.shape[0]
    nt = cos.shape[0] // tm
    return pl.pallas_call(
        functools.partial(_normrot_kernel, scale=scale, sub=sub),
        grid=(m // tm,),
        in_specs=[pl.BlockSpec((tm, width), lambda i: (i, col_block)),
                  pl.BlockSpec((1, width), lambda i: (0, 0)),
                  pl.BlockSpec((tm, LANES), lambda i: (i % nt, 0)),
                  pl.BlockSpec((tm, LANES), lambda i: (i % nt, 0)),
                  pl.BlockSpec((LANES, LANES), lambda i: (0, 0))],
        out_specs=pl.BlockSpec((tm, width), lambda i: (i, 0)),
        out_shape=jax.ShapeDtypeStruct((m, width), bf16),
        compiler_params=_cparams(("parallel",)),
        name=name,
    )(x, g_t, cos, sin, jmat)


def _attn_kernel(q_ref, k_ref, v_ref, km_ref, vm_ref, lq1_ref, lk1_ref, lq2_ref, lk2_ref, sg_ref,
                 o_ref, qq_scr, m_scr, acc_scr, *, n_meta, sub, hpb):
    i = pl.program_id(2)
    tq = q_ref.shape[0]
    tk = tq
    hw = 2 * sub
    mp = km_ref.shape[0]
    lane = lax.broadcasted_iota(jnp.int32, (tq, hw), 1)
    head = lambda hh: slice(hh * hw, (hh + 1) * hw)

    def softmax_step(hh, s, v_aug, first):
        smax = jnp.max(s, axis=1, keepdims=True)
        if first:
            m_new = jnp.broadcast_to(smax, (2 * tq, hw))
        else:
            m_prev = m_scr[hh]
            m_new = jnp.maximum(m_prev, smax)
        p = jnp.concatenate([jnp.exp(s[:, c * hw:(c + 1) * hw] - m_new) for c in range(s.shape[1] // hw)], axis=1)
        pv = _dot(p.astype(bf16), v_aug)
        if first:
            acc_scr[hh] = pv
        else:
            alpha = jnp.exp(m_prev - m_new)
            acc_scr[hh] = jnp.concatenate([alpha, alpha], axis=1) * acc_scr[hh] + pv
        m_scr[hh] = m_new

    for hh in range(hpb):
        q = q_ref[:, head(hh)]
        zero = jnp.zeros_like(q)
        qq_scr[hh] = jnp.concatenate([jnp.where(lane < sub, q, zero), jnp.where(lane >= sub, q, zero)], axis=0)
        s = _dot_nt(qq_scr[hh], km_ref[:, head(hh)])
        col = lax.broadcasted_iota(jnp.int32, s.shape, 1)
        s = jnp.where(col < n_meta, s, NEG_BIG)
        softmax_step(hh, s, jnp.concatenate([vm_ref[:, head(hh)], jnp.ones((mp, hw), bf16)], axis=1), True)

    def block(j, masked):
        start = pl.multiple_of(j * tk, tk)
        for hh in range(hpb):
            kb = k_ref[pl.ds(start, tk), head(hh)]
            vb = v_ref[pl.ds(start, tk), head(hh)]
            s = _dot_nt(qq_scr[hh], kb)
            if masked:
                row = lax.broadcasted_iota(jnp.int32, s.shape, 0) % tq
                col = lax.broadcasted_iota(jnp.int32, s.shape, 1)
                s = jnp.where(col <= row, s, NEG_BIG)
            softmax_step(hh, s, jnp.concatenate([vb, jnp.ones((tk, hw), bf16)], axis=1), False)

    def body(j, c):
        block(j, False)
        return c

    lax.fori_loop(0, i, body, 0)
    block(i, True)

    lam = (jnp.exp(jnp.sum(lq1_ref[...] * lk1_ref[...], axis=1, keepdims=True))
           - jnp.exp(jnp.sum(lq2_ref[...] * lk2_ref[...], axis=1, keepdims=True)) + LAM_INIT)
    for hh in range(hpb):
        acc = acc_scr[hh]
        o = acc[:, :hw] / acc[:, hw:]
        att = o[:tq] - lam * o[tq:]
        ms = jnp.mean(att * att, axis=-1, keepdims=True)
        att = att * lax.rsqrt(ms + EPS) * sg_ref[...] * (1.0 - LAM_INIT)
        o_ref[:, head(hh)] = att.astype(o_ref.dtype)


def _attention(q_rot, k_rot, proj, v_col0, km, vm, lq1, lk1, lq2, lk2, sg, batch, seq, heads, tq, n_meta, sub):
    t, aw = q_rot.shape
    nq = seq // tq
    hw = 2 * sub
    hpb = 2 if heads % 2 == 0 else 1
    bw = hpb * hw
    mp = km.shape[0]
    vec = lambda: pl.BlockSpec((1, sub), lambda b, h, i: (0, 0))
    return pl.pallas_call(
        functools.partial(_attn_kernel, n_meta=n_meta, sub=sub, hpb=hpb),
        grid=(batch, heads // hpb, nq),
        in_specs=[pl.BlockSpec((tq, bw), lambda b, h, i: (b * nq + i, h)),
                  pl.BlockSpec((seq, bw), lambda b, h, i: (b, h)),
                  pl.BlockSpec((seq, bw), lambda b, h, i: (b, v_col0 // hpb + h)),
                  pl.BlockSpec((mp, bw), lambda b, h, i: (0, h)),
                  pl.BlockSpec((mp, bw), lambda b, h, i: (0, h)),
                  vec(), vec(), vec(), vec(),
                  pl.BlockSpec((1, hw), lambda b, h, i: (0, 0))],
        out_specs=pl.BlockSpec((tq, bw), lambda b, h, i: (b * nq + i, h)),
        out_shape=jax.ShapeDtypeStruct((t, aw), bf16),
        scratch_shapes=[pltpu.VMEM((hpb, 2 * tq, hw), bf16), pltpu.VMEM((hpb, 2 * tq, hw), f32),
                        pltpu.VMEM((hpb, 2 * tq, 2 * hw), f32)],
        compiler_params=_cparams(("parallel", "parallel", "arbitrary")),
        name="diff_attention",
    )(q_rot, k_rot, proj, km, vm, lq1, lk1, lq2, lk2, sg)


def _cmul(ar, ai, br, bi):
    return ar * br - ai * bi, ar * bi + ai * br


def _ssm_prep_kernel(are_ref, aim_ref, ldt_ref, bre_ref, bim_ref, cre_ref, cim_ref,
                     pwr_ref, pwi_ref, bdr_ref, bdi_ref, bor_ref, boi_ref, *, n_state, n_chan):
    a_re = are_ref[...]
    a_im = aim_ref[...]
    dt = jnp.exp(ldt_ref[...])
    er = jnp.exp(a_re * dt)
    ab_re = er * jnp.cos(a_im * dt)
    ab_im = er * jnp.sin(a_im * dt)
    nr = ab_re - 1.0
    ni = ab_im
    den = a_re * a_re + a_im * a_im
    f_re = (nr * a_re + ni * a_im) / den
    f_im = (ni * a_re - nr * a_im) / den
    p1 = (ab_re, ab_im)
    p2 = _cmul(*p1, *p1)
    p3 = _cmul(*p2, *p1)
    p4 = _cmul(*p2, *p2)
    p5 = _cmul(*p4, *p1)
    p6 = _cmul(*p4, *p2)
    p7 = _cmul(*p4, *p3)
    p8 = _cmul(*p4, *p4)
    pw = (p1, p2, p3, p4, p5, p6, p7, p8)
    pwr_ref[...] = _stack_rows([p[0] for p in pw])
    pwi_ref[...] = _stack_rows([p[1] for p in pw])
    b_re = bre_ref[0]
    b_im = bim_ref[0]
    r = lax.broadcasted_iota(jnp.int32, b_re.shape, 0) // n_chan
    c = lax.broadcasted_iota(jnp.int32, b_re.shape, 1) // n_state
    keep = r == c
    bdr_ref[0] = jnp.where(keep, f_re * b_re - f_im * b_im, 0.0).astype(bf16)
    bdi_ref[0] = jnp.where(keep, f_re * b_im + f_im * b_re, 0.0).astype(bf16)
    c_re = cre_ref[0]
    c_im = cim_ref[0]
    r = lax.broadcasted_iota(jnp.int32, c_re.shape, 0) // n_state
    c = lax.broadcasted_iota(jnp.int32, c_re.shape, 1) // n_chan
    keep = r == c
    bor_ref[0] = jnp.where(keep, c_re, 0.0).astype(bf16)
    boi_ref[0] = jnp.where(keep, c_im, 0.0).astype(bf16)


def _ssm_prep(a_re, a_im, ldt, b_re_t, b_im_t, c_re_t, c_im_t, n_state, n_chan):
    ngt, _, sl = b_re_t.shape
    gp = a_re.shape[1]
    lane_vec = lambda: pl.BlockSpec((1, sl), lambda g: (0, g))
    bin_spec = lambda: pl.BlockSpec((1, LANES, sl), lambda g: (g, 0, 0))
    bout_spec = lambda: pl.BlockSpec((1, sl, LANES), lambda g: (g, 0, 0))
    return pl.pallas_call(
        functools.partial(_ssm_prep_kernel, n_state=n_state, n_chan=n_chan),
        grid=(ngt,),
        in_specs=[lane_vec(), lane_vec(), lane_vec(), bin_spec(), bin_spec(), bout_spec(), bout_spec()],
        out_specs=[pl.BlockSpec((SUBLANES, sl), lambda g: (0, g)), pl.BlockSpec((SUBLANES, sl), lambda g: (0, g)),
                   bin_spec(), bin_spec(), bout_spec(), bout_spec()],
        out_shape=[jax.ShapeDtypeStruct((SUBLANES, gp), f32), jax.ShapeDtypeStruct((SUBLANES, gp), f32),
                   jax.ShapeDtypeStruct(b_re_t.shape, bf16), jax.ShapeDtypeStruct(b_re_t.shape, bf16),
                   jax.ShapeDtypeStruct(c_re_t.shape, bf16), jax.ShapeDtypeStruct(c_re_t.shape, bf16)],
        compiler_params=_cparams(("parallel",)),
        name="ssm_prep",
    )(a_re, a_im, ldt, b_re_t, b_im_t, c_re_t, c_im_t)


def _ssm_scan_kernel(u_ref, bdr_ref, bdi_ref, bor_ref, boi_ref, pwr_ref, pwi_ref, d_ref, x0r_ref, x0i_ref,
                     y_ref, xfr_ref, xfi_ref, xr_scr, xi_scr, cr_scr, ci_scr):
    i = pl.program_id(2)
    tm = u_ref.shape[0]

    @pl.when(i == 0)
    def _():
        cr_scr[...] = x0r_ref[...]
        ci_scr[...] = x0i_ref[...]

    u = u_ref[...]
    xr = _dot(u, bdr_ref[0])
    xi = _dot(u, bdi_ref[0])
    pwr = pwr_ref[...]
    pwi = pwi_ref[...]
    rowmod = lax.broadcasted_iota(jnp.int32, xr.shape, 0) % SUBLANES
    for k in (1, 2, 4):
        tr, ti = _cmul(pwr[k - 1:k], pwi[k - 1:k], pltpu.roll(xr, k, 0), pltpu.roll(xi, k, 0))
        ok = rowmod >= k
        xr = xr + jnp.where(ok, tr, 0.0)
        xi = xi + jnp.where(ok, ti, 0.0)
    xr_scr[...] = xr
    xi_scr[...] = xi

    def body(r, carry):
        cr, ci = carry
        rows = pl.ds(pl.multiple_of(r * SUBLANES, SUBLANES), SUBLANES)
        tr, ti = _cmul(pwr, pwi, cr, ci)
        nr = xr_scr[rows, :] + tr
        ni = xi_scr[rows, :] + ti
        xr_scr[rows, :] = nr
        xi_scr[rows, :] = ni
        return nr[SUBLANES - 1:SUBLANES], ni[SUBLANES - 1:SUBLANES]

    cr, ci = lax.fori_loop(0, tm // SUBLANES, body, (cr_scr[...], ci_scr[...]))
    cr_scr[...] = cr
    ci_scr[...] = ci
    xfr_ref[0] = cr
    xfi_ref[0] = ci

    y = _dot(xr_scr[...].astype(bf16), bor_ref[0]) - _dot(xi_scr[...].astype(bf16), boi_ref[0])
    y = y + d_ref[...] * u.astype(f32)
    y_ref[...] = _gelu(y).astype(y_ref.dtype)


def _ssm_scan(u_arr, u_col0, prep, d, x0r, x0i, batch, seq, tm):
    pwr, pwi, bdr, bdi, bor, boi = prep
    ngt, _, sl = bdr.shape
    nt = seq // tm
    bin_spec = lambda: pl.BlockSpec((1, LANES, sl), lambda g, b, i: (g, 0, 0))
    bout_spec = lambda: pl.BlockSpec((1, sl, LANES), lambda g, b, i: (g, 0, 0))
    pw_spec = lambda: pl.BlockSpec((SUBLANES, sl), lambda g, b, i: (0, g))
    x0_spec = lambda: pl.BlockSpec((1, sl), lambda g, b, i: (0, g))
    xf_spec = lambda: pl.BlockSpec((1, 1, sl), lambda g, b, i: (b, 0, g))
    return pl.pallas_call(
        _ssm_scan_kernel,
        grid=(ngt, batch, nt),
        in_specs=[pl.BlockSpec((tm, LANES), lambda g, b, i: (b * nt + i, u_col0 + g)),
                  bin_spec(), bin_spec(), bout_spec(), bout_spec(), pw_spec(), pw_spec(),
                  pl.BlockSpec((1, LANES), lambda g, b, i: (0, g)), x0_spec(), x0_spec()],
        out_specs=[pl.BlockSpec((tm, LANES), lambda g, b, i: (b * nt + i, g)), xf_spec(), xf_spec()],
        out_shape=[jax.ShapeDtypeStruct((batch * seq, ngt * LANES), bf16),
                   jax.ShapeDtypeStruct((batch, 1, ngt * sl), f32),
                   jax.ShapeDtypeStruct((batch, 1, ngt * sl), f32)],
        scratch_shapes=[pltpu.VMEM((tm, sl), f32), pltpu.VMEM((tm, sl), f32),
                        pltpu.VMEM((1, sl), f32), pltpu.VMEM((1, sl), f32)],
        compiler_params=_cparams(("parallel", "arbitrary", "arbitrary")),
        name="ssm_scan",
    )(u_arr, bdr, bdi, bor, boi, pwr, pwi, d, x0r, x0i)


def _attn_out_kernel(a_ref, w_ref, g_ref, o_ref):
    y = _dot(a_ref[...], w_ref[...])
    o_ref[...] = (_sigmoid(g_ref[...].astype(f32)) * y).astype(o_ref.dtype)


def _glu_mix_kernel(s_ref, wa_ref, wb_ref, g_ref, ya_ref, o_ref):
    s = s_ref[...]
    ga = _dot(s, wa_ref[...])
    gb = _dot(s, wb_ref[...])
    yb = ga * _sigmoid(gb)
    o_ref[...] = (ya_ref[...].astype(f32) + _sigmoid(g_ref[...].astype(f32)) * yb).astype(o_ref.dtype)


def _out_proj_kernel(m_ref, w_ref, x_ref, o_ref):
    o_ref[...] = x_ref[...] + _dot(m_ref[...], w_ref[...])


def _peer_query_kernel(h_ref, g_ref, wq_ref, kk_ref, hn_ref, st_ref):
    x = h_ref[...]
    ms = jnp.mean(x * x, axis=-1, keepdims=True)
    hn = (x * lax.rsqrt(ms + EPS) * g_ref[...]).astype(bf16)
    hn_ref[...] = hn
    q = _dot(hn, wq_ref[...]).astype(bf16)
    n_heads = kk_ref.shape[0]
    half = kk_ref.shape[3]
    for h in range(n_heads):
        for side in range(2):
            c0 = (2 * h + side) * half
            st_ref[h, side] = _dot_nt(kk_ref[h, side], q[:, c0:c0 + half])


def _peer_query(h1, g, wq, kk, tm):
    t, d = h1.shape
    nh, _, nk, half = kk.shape
    return pl.pallas_call(
        _peer_query_kernel,
        grid=(t // tm,),
        in_specs=[pl.BlockSpec((tm, d), lambda i: (i, 0)),
                  pl.BlockSpec((1, d), lambda i: (0, 0)),
                  pl.BlockSpec(wq.shape, lambda i: (0, 0)),
                  pl.BlockSpec(kk.shape, lambda i: (0, 0, 0, 0))],
        out_specs=[pl.BlockSpec((tm, d), lambda i: (i, 0)),
                   pl.BlockSpec((nh, 2, nk, tm), lambda i: (0, 0, 0, i))],
        out_shape=[jax.ShapeDtypeStruct((t, d), bf16), jax.ShapeDtypeStruct((nh, 2, nk, t), f32)],
        compiler_params=_cparams(("parallel",)),
        name="peer_query",
    )(h1, g, wq, kk)


def _top_desc(s, n):
    vals = []
    cur = s
    rank = jnp.full(s.shape, float(n), f32)
    for a in range(n):
        v = jnp.max(cur, axis=0, keepdims=True)
        vals.append(v)
        hit = cur == v
        rank = jnp.where(hit, float(a), rank)
        cur = jnp.where(hit, -jnp.inf, cur)
    return _stack_rows(vals), rank


def _peer_topk_kernel(st_ref, r2_ref, e2_ref, n1_ref, e1_ref):
    chunk = LANES
    for c in range(st_ref.shape[3] // chunk):
        cols = slice(c * chunk, (c + 1) * chunk)
        s1 = st_ref[0, 0, :, cols]
        s2 = st_ref[0, 1, :, cols]
        v1, rank1 = _top_desc(s1, TOPK)
        v2, rank2 = _top_desc(s2, TOPK)
        n_b = lambda a: -(-(TOPK // (a + 1)) // SUBLANES) * SUBLANES
        cand = jnp.concatenate([v1[a:a + 1] + v2[:n_b(a)] for a in range(TOPK)], axis=0)
        best, _ = _top_desc(cand, TOPK)
        tau = best[TOPK - 1:TOPK]
        z = jnp.sum(jnp.exp(best - best[0:1]), axis=0, keepdims=True)
        n1 = jnp.zeros(s1.shape, f32)
        for a in range(TOPK):
            cnt = jnp.sum(jnp.where((v1[a:a + 1] + v2) >= tau, 1.0, 0.0), axis=0, keepdims=True)
            n1 = jnp.where(rank1 == float(a), cnt, n1)
        r2_ref[0, :, cols] = rank2.astype(bf16)
        e2_ref[0, :, cols] = jnp.exp(s2 - v2[0:1]).astype(bf16)
        n1_ref[0, :, cols] = n1
        e1_ref[0, :, cols] = jnp.exp(s1 - v1[0:1]) / z


def _peer_topk(st, tl):
    nh, _, nk, t = st.shape
    spec = lambda: pl.BlockSpec((1, nk, tl), lambda h, i: (h, 0, i))
    shp = lambda dt: jax.ShapeDtypeStruct((nh, nk, t), dt)
    return pl.pallas_call(
        _peer_topk_kernel,
        grid=(nh, t // tl),
        in_specs=[pl.BlockSpec((1, 2, nk, tl), lambda h, i: (h, 0, 0, i))],
        out_specs=[spec(), spec(), spec(), spec()],
        out_shape=[shp(bf16), shp(bf16), shp(f32), shp(f32)],
        compiler_params=_cparams(("parallel", "parallel")),
        name="peer_topk",
    )(st)


def _peer_dense_kernel(hn_ref, u_ref, vtp_ref, vtl_ref, r2_ref, e2_ref, n1_ref, e1_ref, h1_ref, o_ref,
                       acc_scr, wa_scr, wb_scr, *, nb):
    e = pl.program_id(1)
    n_heads, nk, tm = r2_ref.shape
    eb = u_ref.shape[0]
    pk = 2 * SUBLANES

    @pl.when(e == 0)
    def _():
        acc_scr[...] = jnp.zeros_like(acc_scr)
        wb_scr[...] = jnp.zeros_like(wb_scr)

    def step(w_cur, w_prev):
        a_t = _dot_nt(u_ref[...], hn_ref[...])
        acc_scr[...] += _dot(vtp_ref[...], w_prev[...])
        for ii in range(eb // nk):
            i1 = e * (eb // nk) + ii
            gate = None
            for h in range(n_heads):
                n1 = jnp.broadcast_to(n1_ref[h, pl.ds(i1, 1), :], (pk, tm)).astype(bf16)
                e1 = jnp.broadcast_to(e1_ref[h, pl.ds(i1, 1), :], (pk, tm)).astype(bf16)
                r2 = r2_ref[h].reshape(nk // pk, pk, tm)
                e2 = e2_ref[h].reshape(nk // pk, pk, tm)
                term = jnp.where(r2 < n1[None], e2, jnp.zeros_like(e2)) * e1[None]
                gate = term if gate is None else gate + term
            g = _gelu(a_t[ii * nk:(ii + 1) * nk]).astype(bf16)
            w_cur[ii * nk:(ii + 1) * nk, :] = gate.reshape(nk, tm) * g

    @pl.when(e % 2 == 0)
    def _():
        step(wa_scr, wb_scr)

    @pl.when(e % 2 == 1)
    def _():
        step(wb_scr, wa_scr)

    @pl.when(e == nb - 1)
    def _():
        w_last = wb_scr if (nb - 1) % 2 else wa_scr
        o_ref[...] = h1_ref[...] + (acc_scr[...] + _dot(vtl_ref[...], w_last[...])).T


def _peer_dense(hn, u_b, vt_b, r2, e2, n1, e1, h1, tm, eb):
    t, d = hn.shape
    nb = u_b.shape[0] // eb
    nh, nk, _ = r2.shape
    once = pl.Buffered(1)
    tok = lambda: pl.BlockSpec((nh, nk, tm), lambda i, e: (0, 0, i), pipeline_mode=once)
    return pl.pallas_call(
        functools.partial(_peer_dense_kernel, nb=nb),
        grid=(t // tm, nb),
        in_specs=[pl.BlockSpec((tm, d), lambda i, e: (i, 0)),
                  pl.BlockSpec((eb, d), lambda i, e: (e, 0)),
                  pl.BlockSpec((d, eb), lambda i, e: (0, jnp.maximum(e - 1, 0))),
                  pl.BlockSpec((d, eb), lambda i, e: (0, nb - 1), pipeline_mode=once),
                  tok(), tok(), tok(), tok(),
                  pl.BlockSpec((tm, d), lambda i, e: (i, 0), pipeline_mode=once)],
        out_specs=pl.BlockSpec((tm, d), lambda i, e: (i, 0)),
        out_shape=jax.ShapeDtypeStruct((t, d), f32),
        scratch_shapes=[pltpu.VMEM((d, tm), f32), pltpu.VMEM((eb, tm), bf16), pltpu.VMEM((eb, tm), bf16)],
        compiler_params=_cparams(("parallel", "arbitrary")),
        name="peer_dense",
    )(hn, u_b, vt_b, vt_b, r2, e2, n1, e1, h1)


def _tile(n, pref):
    if n <= pref:
        return n
    t = pref // LANES * LANES
    while n % t:
        t -= LANES
    assert t > 0, (n, pref)
    return t


def kernel(x, meta_tokens, norm1_g, w_in, q_norm_g, k_norm_g, lambda_q1, lambda_k1, lambda_q2, lambda_k2, subln_g, w_attn_branch, ssm_a_re, ssm_a_im, ssm_log_dt, ssm_b_re, ssm_b_im, ssm_c_re, ssm_c_im, ssm_d, w_glu, w_out, norm2_g, peer_w_q, peer_k1, peer_k2, peer_u, peer_v):
    assert norm1_g.shape[0] == 1, "single-layer block only"
    batch, seq, d = x.shape
    n_meta = meta_tokens.shape[0]
    sub = q_norm_g.shape[-1]
    aw = w_attn_branch.shape[1]
    heads = aw // (2 * sub)
    n_groups, n_state, n_chan = ssm_b_re.shape[1:]
    sw = n_groups * n_chan
    assert 2 * sub == LANES and LANES % n_chan == 0 and sw % LANES == 0
    assert n_meta % SUBLANES == 0 and n_meta <= LANES
    t = batch * seq
    n_in = w_in.shape[2]
    assert n_in == 3 * aw + sw + 2 * d

    x2 = x.reshape(t, d)
    w_in_b = w_in[0].astype(bf16)

    proj = _rms_matmul(x2, norm1_g, w_in_b, _tile(t, 1024), _tile(n_in, 1024), bf16, "in_proj")
    proj_m = _rms_matmul(meta_tokens, norm1_g, w_in_b, n_meta, _tile(n_in, 1024), bf16, "in_proj_meta")

    half = sub // 2
    inv_freq = jnp.power(ROPE_THETA, -jnp.arange(half, dtype=f32) / half)
    ang = jnp.arange(n_meta + seq, dtype=f32)[:, None] * inv_freq[None, :]
    cos_t = jnp.tile(jnp.cos(ang), (1, LANES // half))
    sin_t = jnp.tile(jnp.concatenate([-jnp.sin(ang), jnp.sin(ang)], axis=1), (1, LANES // sub))
    lane = jnp.arange(LANES)
    jmat = (lane[:, None] // sub == lane[None, :] // sub).astype(bf16)
    gq = jnp.tile(q_norm_g, (1, aw // sub))
    gk = jnp.tile(k_norm_g, (1, aw // sub))
    tr = _tile(seq, 512)
    q_rot = _normrot(proj, 0, aw, gq, cos_t[n_meta:], sin_t[n_meta:], jmat, tr, sub ** -0.5, sub, "q_normrot")
    k_rot = _normrot(proj, 1, aw, gk, cos_t[n_meta:], sin_t[n_meta:], jmat, tr, 1.0, sub, "k_normrot")
    km_rot = _normrot(proj_m, 1, aw, gk, cos_t[:n_meta], sin_t[:n_meta], jmat, n_meta, 1.0, sub, "k_normrot_meta")
    km = jnp.pad(km_rot, ((0, LANES - n_meta), (0, 0)))
    vm = jnp.pad(proj_m[:, 2 * aw:3 * aw], ((0, LANES - n_meta), (0, 0)))

    att = _attention(q_rot, k_rot, proj, 2 * aw // LANES, km, vm, lambda_q1, lambda_k1, lambda_q2, lambda_k2,
                     subln_g, batch, seq, heads, _tile(seq, 512), n_meta, sub)

    gpt = LANES // n_chan
    ngt = sw // LANES
    sl = gpt * n_state
    gp = n_groups * n_state
    a_re = ssm_a_re.reshape(1, gp)
    a_im = ssm_a_im.reshape(1, gp)
    ldt = jnp.repeat(ssm_log_dt[0], n_state).reshape(1, gp)
    lay_b = lambda b: jnp.tile(b[0].transpose(0, 2, 1).reshape(ngt, LANES, n_state), (1, 1, gpt))
    lay_c = lambda c: jnp.tile(c[0].transpose(0, 2, 1).reshape(ngt, sl, n_chan), (1, 1, gpt))
    prep = _ssm_prep(a_re, a_im, ldt, lay_b(ssm_b_re), lay_b(ssm_b_im), lay_c(ssm_c_re), lay_c(ssm_c_im),
                     n_state, n_chan)
    zeros = jnp.zeros((1, gp), f32)
    u_col0 = 3 * aw // LANES
    _, xmr, xmi = _ssm_scan(proj_m, u_col0, prep, ssm_d, zeros, zeros, 1, n_meta, n_meta)
    ys, _, _ = _ssm_scan(proj, u_col0, prep, ssm_d, xmr[0], xmi[0], batch, seq, _tile(seq, 512))

    tm = _tile(t, 512)
    tn = _tile(d, 1024)
    nj = d // tn
    ga_col0 = (3 * aw + sw) // tn
    gb_col0 = (3 * aw + sw + d) // tn
    grid = (t // tm, nj)
    sem = _cparams(("parallel", "arbitrary"))
    row_full = lambda k: pl.BlockSpec((tm, k), lambda i, j: (i, 0))
    w_col = lambda k, off=0: pl.BlockSpec((k, tn), lambda i, j: (0, j + off))
    out_tile = pl.BlockSpec((tm, tn), lambda i, j: (i, j))
    yag = pl.pallas_call(
        _attn_out_kernel, grid=grid,
        in_specs=[row_full(aw), w_col(aw), pl.BlockSpec((tm, tn), lambda i, j: (i, ga_col0 + j))],
        out_specs=out_tile, out_shape=jax.ShapeDtypeStruct((t, d), bf16), compiler_params=sem, name="attn_out",
    )(att, w_attn_branch[0].astype(bf16), proj)
    w_glu_b = w_glu[0].astype(bf16)
    mix = pl.pallas_call(
        _glu_mix_kernel, grid=grid,
        in_specs=[row_full(sw), w_col(sw), w_col(sw, nj), pl.BlockSpec((tm, tn), lambda i, j: (i, gb_col0 + j)),
                  out_tile],
        out_specs=out_tile, out_shape=jax.ShapeDtypeStruct((t, d), bf16), compiler_params=sem, name="glu_mix",
    )(ys, w_glu_b, w_glu_b, proj, yag)
    h1 = pl.pallas_call(
        _out_proj_kernel, grid=grid,
        in_specs=[row_full(d), w_col(d), out_tile],
        out_specs=out_tile, out_shape=jax.ShapeDtypeStruct((t, d), f32), compiler_params=sem, name="out_proj",
    )(mix, w_out[0].astype(bf16), x2)

    kk = jnp.stack([peer_k1[0], peer_k2[0]], axis=1).astype(bf16)
    assert kk.shape[2] == LANES and kk.shape[3] == LANES
    hn2, st = _peer_query(h1, norm2_g, peer_w_q[0].astype(bf16), kk, _tile(t, 512))
    r2, e2, n1, e1 = _peer_topk(st, _tile(t, 512))
    u_b = peer_u[0].astype(bf16)
    vt_b = peer_v[0].T.astype(bf16)
    out = _peer_dense(hn2, u_b, vt_b, r2, e2, n1, e1, h1, _tile(t, 512), _tile(u_b.shape[0], 1024))
    return out.reshape(batch, seq, d)
```

```python
import functools
import math

import jax
import jax.numpy as jnp
from jax import lax
from jax.experimental import pallas as pl
from jax.experimental.pallas import tpu as pltpu

EPS = 1e-6
ROPE_THETA = 10000.0
LAM_INIT = 0.8 - 0.6 * math.exp(-0.3 * 0)
TOPK = 16
LANES = 128
SUBLANES = 8
NEG_BIG = -1e30
VMEM_LIMIT = 56 * 1024 * 1024

bf16 = jnp.bfloat16
f32 = jnp.float32


def _cparams(sem):
    return pltpu.CompilerParams(dimension_semantics=sem, vmem_limit_bytes=VMEM_LIMIT)


def _dot(a, b):
    return jnp.dot(a, b, preferred_element_type=f32)


def _dot_nt(a, b):
    return lax.dot_general(a, b, (((1,), (1,)), ((), ())), preferred_element_type=f32)


def _sigmoid(x):
    return 1.0 / (1.0 + jnp.exp(-x))


def _gelu(x):
    return 0.5 * x * (1.0 + lax.erf(x * (1.0 / math.sqrt(2.0))))


def _stack_rows(rows):
    k = len(rows)
    idx = lax.broadcasted_iota(jnp.int32, (k, rows[0].shape[1]), 0)
    out = jnp.broadcast_to(rows[0], idx.shape)
    for j in range(1, k):
        out = jnp.where(idx == j, rows[j], out)
    return out


def _rms_matmul_kernel(x_ref, g_ref, w_ref, o_ref, xn_ref):
    @pl.when(pl.program_id(1) == 0)
    def _():
        x = x_ref[...]
        ms = jnp.mean(x * x, axis=-1, keepdims=True)
        xn_ref[...] = (x * lax.rsqrt(ms + EPS) * g_ref[...]).astype(bf16)

    o_ref[...] = _dot(xn_ref[...], w_ref[...]).astype(o_ref.dtype)


def _rms_matmul(x, g, w, tm, tn, out_dtype, name):
    m, k = x.shape
    n = w.shape[1]
    return pl.pallas_call(
        _rms_matmul_kernel,
        grid=(m // tm, n // tn),
        in_specs=[pl.BlockSpec((tm, k), lambda i, j: (i, 0)),
                  pl.BlockSpec((1, k), lambda i, j: (0, 0)),
                  pl.BlockSpec((k, tn), lambda i, j: (0, j))],
        out_specs=pl.BlockSpec((tm, tn), lambda i, j: (i, j)),
        out_shape=jax.ShapeDtypeStruct((m, n), out_dtype),
        scratch_shapes=[pltpu.VMEM((tm, k), bf16)],
        compiler_params=_cparams(("parallel", "arbitrary")),
        name=name,
    )(x, g, w)


def _normrot_kernel(x_ref, g_ref, cos_ref, sin_ref, j_ref, o_ref, *, scale, sub):
    width = x_ref.shape[1]
    cos = cos_ref[...]
    sin = sin_ref[...]
    jmat = j_ref[...]
    lane = lax.broadcasted_iota(jnp.int32, cos.shape, 1)
    first_half = (lane % sub) < (sub // 2)
    for t in range(width // LANES):
        sl = slice(t * LANES, (t + 1) * LANES)
        x = x_ref[:, sl].astype(f32)
        x2 = x * x
        hi = x2.astype(bf16)
        lo = (x2 - hi.astype(f32)).astype(bf16)
        ssq = _dot(hi, jmat) + _dot(lo, jmat)
        xn = x * lax.rsqrt(ssq * (1.0 / sub) + EPS) * g_ref[:, sl]
        partner = jnp.where(first_half, pltpu.roll(xn, LANES - sub // 2, 1), pltpu.roll(xn, sub // 2, 1))
        o_ref[:, sl] = ((xn * cos + partner * sin) * scale).astype(o_ref.dtype)


def _normrot(x, col_block, width, g_t, cos, sin, jmat, tm, scale, sub, name):
    m = x.shape[0]
    nt = cos.shape[0] // tm
    return pl.pallas_call(
        functools.partial(_normrot_kernel, scale=scale, sub=sub),
        grid=(m // tm,),
        in_specs=[pl.BlockSpec((tm, width), lambda i: (i, col_block)),
                  pl.BlockSpec((1, width), lambda i: (0, 0)),
                  pl.BlockSpec((tm, LANES), lambda i: (i % nt, 0)),
                  pl.BlockSpec((tm, LANES), lambda i: (i % nt, 0)),
                  pl.BlockSpec((LANES, LANES), lambda i: (0, 0))],
        out_specs=pl.BlockSpec((tm, width), lambda i: (i, 0)),
        out_shape=jax.ShapeDtypeStruct((m, width), bf16),
        compiler_params=_cparams(("parallel",)),
        name=name,
    )(x, g_t, cos, sin, jmat)


def _attn_kernel(q_ref, k_ref, v_ref, km_ref, vm_ref, lq1_ref, lk1_ref, lq2_ref, lk2_ref, sg_ref,
                 o_ref, qq_scr, m_scr, acc_scr, *, n_meta, sub, hpb):
    i = pl.program_id(2)
    tq = q_ref.shape[0]
    tk = tq
    hw = 2 * sub
    mp = km_ref.shape[0]
    lane = lax.broadcasted_iota(jnp.int32, (tq, hw), 1)
    head = lambda hh: slice(hh * hw, (hh + 1) * hw)

    def softmax_step(hh, s, v_aug, first):
        smax = jnp.max(s, axis=1, keepdims=True)
        if first:
            m_new = jnp.broadcast_to(smax, (2 * tq, hw))
        else:
            m_prev = m_scr[hh]
            m_new = jnp.maximum(m_prev, smax)
        p = jnp.concatenate([jnp.exp(s[:, c * hw:(c + 1) * hw] - m_new) for c in range(s.shape[1] // hw)], axis=1)
        pv = _dot(p.astype(bf16), v_aug)
        if first:
            acc_scr[hh] = pv
        else:
            alpha = jnp.exp(m_prev - m_new)
            acc_scr[hh] = jnp.concatenate([alpha, alpha], axis=1) * acc_scr[hh] + pv
        m_scr[hh] = m_new

    for hh in range(hpb):
        q = q_ref[:, head(hh)]
        zero = jnp.zeros_like(q)
        qq_scr[hh] = jnp.concatenate([jnp.where(lane < sub, q, zero), jnp.where(lane >= sub, q, zero)], axis=0)
        s = _dot_nt(qq_scr[hh], km_ref[:, head(hh)])
        col = lax.broadcasted_iota(jnp.int32, s.shape, 1)
        s = jnp.where(col < n_meta, s, NEG_BIG)
        softmax_step(hh, s, jnp.concatenate([vm_ref[:, head(hh)], jnp.ones((mp, hw), bf16)], axis=1), True)

    def block(j, masked):
        start = pl.multiple_of(j * tk, tk)
        for hh in range(hpb):
            kb = k_ref[pl.ds(start, tk), head(hh)]
            vb = v_ref[pl.ds(start, tk), head(hh)]
            s = _dot_nt(qq_scr[hh], kb)
            if masked:
                row = lax.broadcasted_iota(jnp.int32, s.shape, 0) % tq
                col = lax.broadcasted_iota(jnp.int32, s.shape, 1)
                s = jnp.where(col <= row, s, NEG_BIG)
            softmax_step(hh, s, jnp.concatenate([vb, jnp.ones((tk, hw), bf16)], axis=1), False)

    def body(j, c):
        block(j, False)
        return c

    lax.fori_loop(0, i, body, 0)
    block(i, True)

    lam = (jnp.exp(jnp.sum(lq1_ref[...] * lk1_ref[...], axis=1, keepdims=True))
           - jnp.exp(jnp.sum(lq2_ref[...] * lk2_ref[...], axis=1, keepdims=True)) + LAM_INIT)
    for hh in range(hpb):
        acc = acc_scr[hh]
        o = acc[:, :hw] / acc[:, hw:]
        att = o[:tq] - lam * o[tq:]
        ms = jnp.mean(att * att, axis=-1, keepdims=True)
        att = att * lax.rsqrt(ms + EPS) * sg_ref[...] * (1.0 - LAM_INIT)
        o_ref[:, head(hh)] = att.astype(o_ref.dtype)


def _attention(q_rot, k_rot, proj, v_col0, km, vm, lq1, lk1, lq2, lk2, sg, batch, seq, heads, tq, n_meta, sub):
    t, aw = q_rot.shape
    nq = seq // tq
    hw = 2 * sub
    hpb = 2 if heads % 2 == 0 else 1
    bw = hpb * hw
    mp = km.shape[0]
    vec = lambda: pl.BlockSpec((1, sub), lambda b, h, i: (0, 0))
    return pl.pallas_call(
        functools.partial(_attn_kernel, n_meta=n_meta, sub=sub, hpb=hpb),
        grid=(batch, heads // hpb, nq),
        in_specs=[pl.BlockSpec((tq, bw), lambda b, h, i: (b * nq + i, h)),
                  pl.BlockSpec((seq, bw), lambda b, h, i: (b, h)),
                  pl.BlockSpec((seq, bw), lambda b, h, i: (b, v_col0 // hpb + h)),
                  pl.BlockSpec((mp, bw), lambda b, h, i: (0, h)),
                  pl.BlockSpec((mp, bw), lambda b, h, i: (0, h)),
                  vec(), vec(), vec(), vec(),
                  pl.BlockSpec((1, hw), lambda b, h, i: (0, 0))],
        out_specs=pl.BlockSpec((tq, bw), lambda b, h, i: (b * nq + i, h)),
        out_shape=jax.ShapeDtypeStruct((t, aw), bf16),
        scratch_shapes=[pltpu.VMEM((hpb, 2 * tq, hw), bf16), pltpu.VMEM((hpb, 2 * tq, hw), f32),
                        pltpu.VMEM((hpb, 2 * tq, 2 * hw), f32)],
        compiler_params=_cparams(("parallel", "parallel", "arbitrary")),
        name="diff_attention",
    )(q_rot, k_rot, proj, km, vm, lq1, lk1, lq2, lk2, sg)


def _cmul(ar, ai, br, bi):
    return ar * br - ai * bi, ar * bi + ai * br


def _ssm_prep_kernel(are_ref, aim_ref, ldt_ref, bre_ref, bim_ref, cre_ref, cim_ref,
                     pwr_ref, pwi_ref, bdr_ref, bdi_ref, bor_ref, boi_ref, *, n_state, n_chan):
    a_re = are_ref[...]
    a_im = aim_ref[...]
    dt = jnp.exp(ldt_ref[...])
    er = jnp.exp(a_re * dt)
    ab_re = er * jnp.cos(a_im * dt)
    ab_im = er * jnp.sin(a_im * dt)
    nr = ab_re - 1.0
    ni = ab_im
    den = a_re * a_re + a_im * a_im
    f_re = (nr * a_re + ni * a_im) / den
    f_im = (ni * a_re - nr * a_im) / den
    p1 = (ab_re, ab_im)
    p2 = _cmul(*p1, *p1)
    p3 = _cmul(*p2, *p1)
    p4 = _cmul(*p2, *p2)
    p5 = _cmul(*p4, *p1)
    p6 = _cmul(*p4, *p2)
    p7 = _cmul(*p4, *p3)
    p8 = _cmul(*p4, *p4)
    pw = (p1, p2, p3, p4, p5, p6, p7, p8)
    pwr_ref[...] = _stack_rows([p[0] for p in pw])
    pwi_ref[...] = _stack_rows([p[1] for p in pw])
    b_re = bre_ref[0]
    b_im = bim_ref[0]
    r = lax.broadcasted_iota(jnp.int32, b_re.shape, 0) // n_chan
    c = lax.broadcasted_iota(jnp.int32, b_re.shape, 1) // n_state
    keep = r == c
    bdr_ref[0] = jnp.where(keep, f_re * b_re - f_im * b_im, 0.0).astype(bf16)
    bdi_ref[0] = jnp.where(keep, f_re * b_im + f_im * b_re, 0.0).astype(bf16)
    c_re = cre_ref[0]
    c_im = cim_ref[0]
    r = lax.broadcasted_iota(jnp.int32, c_re.shape, 0) // n_state
    c = lax.broadcasted_iota(jnp.int32, c_re.shape, 1) // n_chan
    keep = r == c
    bor_ref[0] = jnp.where(keep, c_re, 0.0).astype(bf16)
    boi_ref[0] = jnp.where(keep, c_im, 0.0).astype(bf16)


def _ssm_prep(a_re, a_im, ldt, b_re_t, b_im_t, c_re_t, c_im_t, n_state, n_chan):
    ngt, _, sl = b_re_t.shape
    gp = a_re.shape[1]
    lane_vec = lambda: pl.BlockSpec((1, sl), lambda g: (0, g))
    bin_spec = lambda: pl.BlockSpec((1, LANES, sl), lambda g: (g, 0, 0))
    bout_spec = lambda: pl.BlockSpec((1, sl, LANES), lambda g: (g, 0, 0))
    return pl.pallas_call(
        functools.partial(_ssm_prep_kernel, n_state=n_state, n_chan=n_chan),
        grid=(ngt,),
        in_specs=[lane_vec(), lane_vec(), lane_vec(), bin_spec(), bin_spec(), bout_spec(), bout_spec()],
        out_specs=[pl.BlockSpec((SUBLANES, sl), lambda g: (0, g)), pl.BlockSpec((SUBLANES, sl), lambda g: (0, g)),
                   bin_spec(), bin_spec(), bout_spec(), bout_spec()],
        out_shape=[jax.ShapeDtypeStruct((SUBLANES, gp), f32), jax.ShapeDtypeStruct((SUBLANES, gp), f32),
                   jax.ShapeDtypeStruct(b_re_t.shape, bf16), jax.ShapeDtypeStruct(b_re_t.shape, bf16),
                   jax.ShapeDtypeStruct(c_re_t.shape, bf16), jax.ShapeDtypeStruct(c_re_t.shape, bf16)],
        compiler_params=_cparams(("parallel",)),
        name="ssm_prep",
    )(a_re, a_im, ldt, b_re_t, b_im_t, c_re_t, c_im_t)


def _ssm_scan_kernel(u_ref, bdr_ref, bdi_ref, bor_ref, boi_ref, pwr_ref, pwi_ref, d_ref, x0r_ref, x0i_ref,
                     y_ref, xfr_ref, xfi_ref, xr_scr, xi_scr, cr_scr, ci_scr):
    i = pl.program_id(2)
    tm = u_ref.shape[0]

    @pl.when(i == 0)
    def _():
        cr_scr[...] = x0r_ref[...]
        ci_scr[...] = x0i_ref[...]

    u = u_ref[...]
    xr = _dot(u, bdr_ref[0])
    xi = _dot(u, bdi_ref[0])
    pwr = pwr_ref[...]
    pwi = pwi_ref[...]
    rowmod = lax.broadcasted_iota(jnp.int32, xr.shape, 0) % SUBLANES
    for k in (1, 2, 4):
        tr, ti = _cmul(pwr[k - 1:k], pwi[k - 1:k], pltpu.roll(xr, k, 0), pltpu.roll(xi, k, 0))
        ok = rowmod >= k
        xr = xr + jnp.where(ok, tr, 0.0)
        xi = xi + jnp.where(ok, ti, 0.0)
    xr_scr[...] = xr
    xi_scr[...] = xi

    def body(r, carry):
        cr, ci = carry
        rows = pl.ds(pl.multiple_of(r * SUBLANES, SUBLANES), SUBLANES)
        tr, ti = _cmul(pwr, pwi, cr, ci)
        nr = xr_scr[rows, :] + tr
        ni = xi_scr[rows, :] + ti
        xr_scr[rows, :] = nr
        xi_scr[rows, :] = ni
        return nr[SUBLANES - 1:SUBLANES], ni[SUBLANES - 1:SUBLANES]

    cr, ci = lax.fori_loop(0, tm // SUBLANES, body, (cr_scr[...], ci_scr[...]))
    cr_scr[...] = cr
    ci_scr[...] = ci
    xfr_ref[0] = cr
    xfi_ref[0] = ci

    y = _dot(xr_scr[...].astype(bf16), bor_ref[0]) - _dot(xi_scr[...].astype(bf16), boi_ref[0])
    y = y + d_ref[...] * u.astype(f32)
    y_ref[...] = _gelu(y).astype(y_ref.dtype)


def _ssm_scan(u_arr, u_col0, prep, d, x0r, x0i, batch, seq, tm):
    pwr, pwi, bdr, bdi, bor, boi = prep
    ngt, _, sl = bdr.shape
    nt = seq // tm
    bin_spec = lambda: pl.BlockSpec((1, LANES, sl), lambda g, b, i: (g, 0, 0))
    bout_spec = lambda: pl.BlockSpec((1, sl, LANES), lambda g, b, i: (g, 0, 0))
    pw_spec = lambda: pl.BlockSpec((SUBLANES, sl), lambda g, b, i: (0, g))
    x0_spec = lambda: pl.BlockSpec((1, sl), lambda g, b, i: (0, g))
    xf_spec = lambda: pl.BlockSpec((1, 1, sl), lambda g, b, i: (b, 0, g))
    return pl.pallas_call(
        _ssm_scan_kernel,
        grid=(ngt, batch, nt),
        in_specs=[pl.BlockSpec((tm, LANES), lambda g, b, i: (b * nt + i, u_col0 + g)),
                  bin_spec(), bin_spec(), bout_spec(), bout_spec(), pw_spec(), pw_spec(),
                  pl.BlockSpec((1, LANES), lambda g, b, i: (0, g)), x0_spec(), x0_spec()],
        out_specs=[pl.BlockSpec((tm, LANES), lambda g, b, i: (b * nt + i, g)), xf_spec(), xf_spec()],
        out_shape=[jax.ShapeDtypeStruct((batch * seq, ngt * LANES), bf16),
                   jax.ShapeDtypeStruct((batch, 1, ngt * sl), f32),
                   jax.ShapeDtypeStruct((batch, 1, ngt * sl), f32)],
        scratch_shapes=[pltpu.VMEM((tm, sl), f32), pltpu.VMEM((tm, sl), f32),
                        pltpu.VMEM((1, sl), f32), pltpu.VMEM((1, sl), f32)],
        compiler_params=_cparams(("parallel", "arbitrary", "arbitrary")),
        name="ssm_scan",
    )(u_arr, bdr, bdi, bor, boi, pwr, pwi, d, x0r, x0i)


def _attn_out_kernel(a_ref, w_ref, g_ref, o_ref):
    y = _dot(a_ref[...], w_ref[...])
    o_ref[...] = (_sigmoid(g_ref[...].astype(f32)) * y).astype(o_ref.dtype)


def _glu_mix_kernel(s_ref, wa_ref, wb_ref, g_ref, ya_ref, o_ref):
    s = s_ref[...]
    ga = _dot(s, wa_ref[...])
    gb = _dot(s, wb_ref[...])
    yb = ga * _sigmoid(gb)
    o_ref[...] = (ya_ref[...].astype(f32) + _sigmoid(g_ref[...].astype(f32)) * yb).astype(o_ref.dtype)


def _out_proj_kernel(m_ref, w_ref, x_ref, o_ref):
    o_ref[...] = x_ref[...] + _dot(m_ref[...], w_ref[...])


def _peer_query_kernel(h_ref, g_ref, wq_ref, kk_ref, hn_ref, st_ref):
    x = h_ref[...]
    ms = jnp.mean(x * x, axis=-1, keepdims=True)
    hn_f = x * lax.rsqrt(ms + EPS) * g_ref[...]
    hn_ref[...] = hn_f.T.astype(bf16)
    q = _dot(hn_f.astype(bf16), wq_ref[...]).astype(bf16)
    n_heads = kk_ref.shape[0]
    half = kk_ref.shape[3]
    for h in range(n_heads):
        for side in range(2):
            c0 = (2 * h + side) * half
            st_ref[h, side] = _dot_nt(kk_ref[h, side], q[:, c0:c0 + half])


def _peer_query(h1, g, wq, kk, tm):
    t, d = h1.shape
    nh, _, nk, half = kk.shape
    return pl.pallas_call(
        _peer_query_kernel,
        grid=(t // tm,),
        in_specs=[pl.BlockSpec((tm, d), lambda i: (i, 0)),
                  pl.BlockSpec((1, d), lambda i: (0, 0)),
                  pl.BlockSpec(wq.shape, lambda i: (0, 0)),
                  pl.BlockSpec(kk.shape, lambda i: (0, 0, 0, 0))],
        out_specs=[pl.BlockSpec((d, tm), lambda i: (0, i)),
                   pl.BlockSpec((nh, 2, nk, tm), lambda i: (0, 0, 0, i))],
        out_shape=[jax.ShapeDtypeStruct((d, t), bf16), jax.ShapeDtypeStruct((nh, 2, nk, t), f32)],
        compiler_params=_cparams(("parallel",)),
        name="peer_query",
    )(h1, g, wq, kk)


def _top_desc(s, n):
    vals = []
    cur = s
    rank = jnp.full(s.shape, float(n), f32)
    for a in range(n):
        v = jnp.max(cur, axis=0, keepdims=True)
        vals.append(v)
        hit = cur == v
        rank = jnp.where(hit, float(a), rank)
        cur = jnp.where(hit, -jnp.inf, cur)
    return _stack_rows(vals), rank


def _peer_topk_kernel(st_ref, r2_ref, e2_ref, n1_ref, e1_ref):
    chunk = LANES
    for c in range(st_ref.shape[3] // chunk):
        cols = slice(c * chunk, (c + 1) * chunk)
        s1 = st_ref[0, 0, :, cols]
        s2 = st_ref[0, 1, :, cols]
        v1, rank1 = _top_desc(s1, TOPK)
        v2, rank2 = _top_desc(s2, TOPK)
        n_b = lambda a: -(-(TOPK // (a + 1)) // SUBLANES) * SUBLANES
        cand = jnp.concatenate([v1[a:a + 1] + v2[:n_b(a)] for a in range(TOPK)], axis=0)
        best, _ = _top_desc(cand, TOPK)
        tau = best[TOPK - 1:TOPK]
        z = jnp.sum(jnp.exp(best - best[0:1]), axis=0, keepdims=True)
        n1 = jnp.zeros(s1.shape, f32)
        for a in range(TOPK):
            cnt = jnp.sum(jnp.where((v1[a:a + 1] + v2) >= tau, 1.0, 0.0), axis=0, keepdims=True)
            n1 = jnp.where(rank1 == float(a), cnt, n1)
        r2_ref[0, :, cols] = rank2.astype(bf16)
        e2_ref[0, :, cols] = jnp.exp(s2 - v2[0:1]).astype(bf16)
        n1_ref[0, :, cols] = n1
        e1_ref[0, :, cols] = jnp.exp(s1 - v1[0:1]) / z


def _peer_topk(st, tl):
    nh, _, nk, t = st.shape
    spec = lambda: pl.BlockSpec((1, nk, tl), lambda h, i: (h, 0, i))
    shp = lambda dt: jax.ShapeDtypeStruct((nh, nk, t), dt)
    return pl.pallas_call(
        _peer_topk_kernel,
        grid=(nh, t // tl),
        in_specs=[pl.BlockSpec((1, 2, nk, tl), lambda h, i: (h, 0, 0, i))],
        out_specs=[spec(), spec(), spec(), spec()],
        out_shape=[shp(bf16), shp(bf16), shp(f32), shp(f32)],
        compiler_params=_cparams(("parallel", "parallel")),
        name="peer_topk",
    )(st)


def _peer_dense_kernel(hn_ref, u_ref, vt_ref, r2_ref, e2_ref, n1_ref, e1_ref, h1_ref, o_ref,
                       acc_scr, a0_scr, a1_scr, *, nb, tn):
    e = pl.program_id(1)
    n_heads, nk, tm = r2_ref.shape
    eb = u_ref.shape[0]
    pk = 2 * SUBLANES

    @pl.when(e == 0)
    def _():
        acc_scr[...] = jnp.zeros_like(acc_scr)
        a1_scr[...] = jnp.zeros_like(a1_scr)

    def step(a_cur, a_prev):
        a_cur[...] = _dot(u_ref[...], hn_ref[...])
        blk = jnp.maximum(e - 1, 0)
        for c in range(tm // tn):
            cols = slice(c * tn, (c + 1) * tn)
            w_rows = []
            for ii in range(eb // nk):
                i1 = blk * (eb // nk) + ii
                gate = None
                for h in range(n_heads):
                    n1 = jnp.broadcast_to(n1_ref[h, pl.ds(i1, 1), cols], (pk, tn)).astype(bf16)
                    e1 = jnp.broadcast_to(e1_ref[h, pl.ds(i1, 1), cols], (pk, tn)).astype(bf16)
                    r2 = r2_ref[h, :, cols].reshape(nk // pk, pk, tn)
                    e2 = e2_ref[h, :, cols].reshape(nk // pk, pk, tn)
                    term = jnp.where(r2 < n1[None], e2, jnp.zeros_like(e2)) * e1[None]
                    gate = term if gate is None else gate + term
                g = _gelu(a_prev[ii * nk:(ii + 1) * nk, cols]).astype(bf16)
                w_rows.append(gate.reshape(nk, tn) * g)
            acc_scr[:, cols] += _dot(vt_ref[...], jnp.concatenate(w_rows, axis=0))

    @pl.when(e % 2 == 0)
    def _():
        step(a0_scr, a1_scr)

    @pl.when(e % 2 == 1)
    def _():
        step(a1_scr, a0_scr)

    @pl.when(e == nb)
    def _():
        o_ref[...] = h1_ref[...] + acc_scr[...].T


def _peer_dense(hn_t, u_b, vt_b, r2, e2, n1, e1, h1, tm, eb):
    d, t = hn_t.shape
    nb = u_b.shape[0] // eb
    nh, nk, _ = r2.shape
    tok = lambda: pl.BlockSpec((nh, nk, tm), lambda i, e: (0, 0, i))
    return pl.pallas_call(
        functools.partial(_peer_dense_kernel, nb=nb, tn=_tile(tm, 2 * LANES)),
        grid=(t // tm, nb + 1),
        in_specs=[pl.BlockSpec((d, tm), lambda i, e: (0, i)),
                  pl.BlockSpec((eb, d), lambda i, e: (jnp.minimum(e, nb - 1), 0)),
                  pl.BlockSpec((d, eb), lambda i, e: (0, jnp.maximum(e - 1, 0))),
                  tok(), tok(), tok(), tok(),
                  pl.BlockSpec((tm, d), lambda i, e: (i, 0), pipeline_mode=pl.Buffered(1))],
        out_specs=pl.BlockSpec((tm, d), lambda i, e: (i, 0)),
        out_shape=jax.ShapeDtypeStruct((t, d), f32),
        scratch_shapes=[pltpu.VMEM((d, tm), f32), pltpu.VMEM((eb, tm), f32), pltpu.VMEM((eb, tm), f32)],
        compiler_params=_cparams(("parallel", "arbitrary")),
        name="peer_dense",
    )(hn_t, u_b, vt_b, r2, e2, n1, e1, h1)


def _tile(n, pref):
    if n <= pref:
        return n
    t = pref // LANES * LANES
    while n % t:
        t -= LANES
    assert t > 0, (n, pref)
    return t


def kernel(x, meta_tokens, norm1_g, w_in, q_norm_g, k_norm_g, lambda_q1, lambda_k1, lambda_q2, lambda_k2, subln_g, w_attn_branch, ssm_a_re, ssm_a_im, ssm_log_dt, ssm_b_re, ssm_b_im, ssm_c_re, ssm_c_im, ssm_d, w_glu, w_out, norm2_g, peer_w_q, peer_k1, peer_k2, peer_u, peer_v):
    assert norm1_g.shape[0] == 1, "single-layer block only"
    batch, seq, d = x.shape
    n_meta = meta_tokens.shape[0]
    sub = q_norm_g.shape[-1]
    aw = w_attn_branch.shape[1]
    heads = aw // (2 * sub)
    n_groups, n_state, n_chan = ssm_b_re.shape[1:]
    sw = n_groups * n_chan
    assert 2 * sub == LANES and LANES % n_chan == 0 and sw % LANES == 0
    assert n_meta % SUBLANES == 0 and n_meta <= LANES
    t = batch * seq
    n_in = w_in.shape[2]
    assert n_in == 3 * aw + sw + 2 * d

    x2 = x.reshape(t, d)
    w_in_b = w_in[0].astype(bf16)

    proj = _rms_matmul(x2, norm1_g, w_in_b, _tile(t, 1024), _tile(n_in, 1024), bf16, "in_proj")
    proj_m = _rms_matmul(meta_tokens, norm1_g, w_in_b, n_meta, _tile(n_in, 1024), bf16, "in_proj_meta")

    half = sub // 2
    inv_freq = jnp.power(ROPE_THETA, -jnp.arange(half, dtype=f32) / half)
    ang = jnp.arange(n_meta + seq, dtype=f32)[:, None] * inv_freq[None, :]
    cos_t = jnp.tile(jnp.cos(ang), (1, LANES // half))
    sin_t = jnp.tile(jnp.concatenate([-jnp.sin(ang), jnp.sin(ang)], axis=1), (1, LANES // sub))
    lane = jnp.arange(LANES)
    jmat = (lane[:, None] // sub == lane[None, :] // sub).astype(bf16)
    gq = jnp.tile(q_norm_g, (1, aw // sub))
    gk = jnp.tile(k_norm_g, (1, aw // sub))
    tr = _tile(seq, 512)
    q_rot = _normrot(proj, 0, aw, gq, cos_t[n_meta:], sin_t[n_meta:], jmat, tr, sub ** -0.5, sub, "q_normrot")
    k_rot = _normrot(proj, 1, aw, gk, cos_t[n_meta:], sin_t[n_meta:], jmat, tr, 1.0, sub, "k_normrot")
    km_rot = _normrot(proj_m, 1, aw, gk, cos_t[:n_meta], sin_t[:n_meta], jmat, n_meta, 1.0, sub, "k_normrot_meta")
    km = jnp.pad(km_rot, ((0, LANES - n_meta), (0, 0)))
    vm = jnp.pad(proj_m[:, 2 * aw:3 * aw], ((0, LANES - n_meta), (0, 0)))

    att = _attention(q_rot, k_rot, proj, 2 * aw // LANES, km, vm, lambda_q1, lambda_k1, lambda_q2, lambda_k2,
                     subln_g, batch, seq, heads, _tile(seq, 512), n_meta, sub)

    gpt = LANES // n_chan
    ngt = sw // LANES
    sl = gpt * n_state
    gp = n_groups * n_state
    a_re = ssm_a_re.reshape(1, gp)
    a_im = ssm_a_im.reshape(1, gp)
    ldt = jnp.repeat(ssm_log_dt[0], n_state).reshape(1, gp)
    lay_b = lambda b: jnp.tile(b[0].transpose(0, 2, 1).reshape(ngt, LANES, n_state), (1, 1, gpt))
    lay_c = lambda c: jnp.tile(c[0].transpose(0, 2, 1).reshape(ngt, sl, n_chan), (1, 1, gpt))
    prep = _ssm_prep(a_re, a_im, ldt, lay_b(ssm_b_re), lay_b(ssm_b_im), lay_c(ssm_c_re), lay_c(ssm_c_im),
                     n_state, n_chan)
    zeros = jnp.zeros((1, gp), f32)
    u_col0 = 3 * aw // LANES
    _, xmr, xmi = _ssm_scan(proj_m, u_col0, prep, ssm_d, zeros, zeros, 1, n_meta, n_meta)
    ys, _, _ = _ssm_scan(proj, u_col0, prep, ssm_d, xmr[0], xmi[0], batch, seq, _tile(seq, 512))

    tm = _tile(t, 512)
    tn = _tile(d, 1024)
    nj = d // tn
    ga_col0 = (3 * aw + sw) // tn
    gb_col0 = (3 * aw + sw + d) // tn
    grid = (t // tm, nj)
    sem = _cparams(("parallel", "arbitrary"))
    row_full = lambda k: pl.BlockSpec((tm, k), lambda i, j: (i, 0))
    w_col = lambda k, off=0: pl.BlockSpec((k, tn), lambda i, j: (0, j + off))
    out_tile = pl.BlockSpec((tm, tn), lambda i, j: (i, j))
    yag = pl.pallas_call(
        _attn_out_kernel, grid=grid,
        in_specs=[row_full(aw), w_col(aw), pl.BlockSpec((tm, tn), lambda i, j: (i, ga_col0 + j))],
        out_specs=out_tile, out_shape=jax.ShapeDtypeStruct((t, d), bf16), compiler_params=sem, name="attn_out",
    )(att, w_attn_branch[0].astype(bf16), proj)
    w_glu_b = w_glu[0].astype(bf16)
    mix = pl.pallas_call(
        _glu_mix_kernel, grid=grid,
        in_specs=[row_full(sw), w_col(sw), w_col(sw, nj), pl.BlockSpec((tm, tn), lambda i, j: (i, gb_col0 + j)),
                  out_tile],
        out_specs=out_tile, out_shape=jax.ShapeDtypeStruct((t, d), bf16), compiler_params=sem, name="glu_mix",
    )(ys, w_glu_b, w_glu_b, proj, yag)
    h1 = pl.pallas_call(
        _out_proj_kernel, grid=grid,
        in_specs=[row_full(d), w_col(d), out_tile],
        out_specs=out_tile, out_shape=jax.ShapeDtypeStruct((t, d), f32), compiler_params=sem, name="out_proj",
    )(mix, w_out[0].astype(bf16), x2)

    kk = jnp.stack([peer_k1[0], peer_k2[0]], axis=1).astype(bf16)
    assert kk.shape[2] == LANES and kk.shape[3] == LANES
    hn2, st = _peer_query(h1, norm2_g, peer_w_q[0].astype(bf16), kk, _tile(t, 512))
    r2, e2, n1, e1 = _peer_topk(st, _tile(t, 512))
    u_b = peer_u[0].astype(bf16)
    vt_b = peer_v[0].T.astype(bf16)
    out = _peer_dense(hn2, u_b, vt_b, r2, e2, n1, e1, h1, _tile(t, 512), _tile(u_b.shape[0], 512))
    return out.reshape(batch, seq, d)
```

```python
import functools
import math

import jax
import jax.numpy as jnp
from jax import lax
from jax.experimental import pallas as pl
from jax.experimental.pallas import tpu as pltpu

EPS = 1e-6
ROPE_THETA = 10000.0
LAM_INIT = 0.8 - 0.6 * math.exp(-0.3 * 0)
TOPK = 16
LANES = 128
SUBLANES = 8
NEG_BIG = -1e30
VMEM_LIMIT = 56 * 1024 * 1024

bf16 = jnp.bfloat16
f32 = jnp.float32


def _cparams(sem):
    return pltpu.CompilerParams(dimension_semantics=sem, vmem_limit_bytes=VMEM_LIMIT)


def _dot(a, b):
    return jnp.dot(a, b, preferred_element_type=f32)


def _dot_nt(a, b):
    return lax.dot_general(a, b, (((1,), (1,)), ((), ())), preferred_element_type=f32)


def _sigmoid(x):
    return 1.0 / (1.0 + jnp.exp(-x))


def _gelu(x):
    return 0.5 * x * (1.0 + lax.erf(x * (1.0 / math.sqrt(2.0))))


def _stack_rows(rows):
    k = len(rows)
    idx = lax.broadcasted_iota(jnp.int32, (k, rows[0].shape[1]), 0)
    out = jnp.broadcast_to(rows[0], idx.shape)
    for j in range(1, k):
        out = jnp.where(idx == j, rows[j], out)
    return out


def _rms_matmul_kernel(x_ref, g_ref, w_ref, o_ref, xn_ref):
    @pl.when(pl.program_id(1) == 0)
    def _():
        x = x_ref[...]
        ms = jnp.mean(x * x, axis=-1, keepdims=True)
        xn_ref[...] = (x * lax.rsqrt(ms + EPS) * g_ref[...]).astype(bf16)

    o_ref[...] = _dot(xn_ref[...], w_ref[...]).astype(o_ref.dtype)


def _rms_matmul(x, g, w, tm, tn, out_dtype, name):
    m, k = x.shape
    n = w.shape[1]
    return pl.pallas_call(
        _rms_matmul_kernel,
        grid=(m // tm, n // tn),
        in_specs=[pl.BlockSpec((tm, k), lambda i, j: (i, 0)),
                  pl.BlockSpec((1, k), lambda i, j: (0, 0)),
                  pl.BlockSpec((k, tn), lambda i, j: (0, j))],
        out_specs=pl.BlockSpec((tm, tn), lambda i, j: (i, j)),
        out_shape=jax.ShapeDtypeStruct((m, n), out_dtype),
        scratch_shapes=[pltpu.VMEM((tm, k), bf16)],
        compiler_params=_cparams(("parallel", "arbitrary")),
        name=name,
    )(x, g, w)


def _normrot_kernel(x_ref, g_ref, cos_ref, sin_ref, j_ref, o_ref, *, scale, sub):
    width = x_ref.shape[1]
    cos = cos_ref[...]
    sin = sin_ref[...]
    jmat = j_ref[...]
    lane = lax.broadcasted_iota(jnp.int32, cos.shape, 1)
    first_half = (lane % sub) < (sub // 2)
    for t in range(width // LANES):
        sl = slice(t * LANES, (t + 1) * LANES)
        x = x_ref[:, sl].astype(f32)
        x2 = x * x
        hi = x2.astype(bf16)
        lo = (x2 - hi.astype(f32)).astype(bf16)
        ssq = _dot(hi, jmat) + _dot(lo, jmat)
        xn = x * lax.rsqrt(ssq * (1.0 / sub) + EPS) * g_ref[:, sl]
        partner = jnp.where(first_half, pltpu.roll(xn, LANES - sub // 2, 1), pltpu.roll(xn, sub // 2, 1))
        o_ref[:, sl] = ((xn * cos + partner * sin) * scale).astype(o_ref.dtype)


def _normrot(x, col_block, width, g_t, cos, sin, jmat, tm, scale, sub, name):
    m = x.shape[0]
    nt = cos.shape[0] // tm
    return pl.pallas_call(
        functools.partial(_normrot_kernel, scale=scale, sub=sub),
        grid=(m // tm,),
        in_specs=[pl.BlockSpec((tm, width), lambda i: (i, col_block)),
                  pl.BlockSpec((1, width), lambda i: (0, 0)),
                  pl.BlockSpec((tm, LANES), lambda i: (i % nt, 0)),
                  pl.BlockSpec((tm, LANES), lambda i: (i % nt, 0)),
                  pl.BlockSpec((LANES, LANES), lambda i: (0, 0))],
        out_specs=pl.BlockSpec((tm, width), lambda i: (i, 0)),
        out_shape=jax.ShapeDtypeStruct((m, width), bf16),
        compiler_params=_cparams(("parallel",)),
        name=name,
    )(x, g_t, cos, sin, jmat)


def _attn_kernel(q_ref, k_ref, v_ref, km_ref, vm_ref, lq1_ref, lk1_ref, lq2_ref, lk2_ref, sg_ref,
                 o_ref, qq_scr, m_scr, acc_scr, *, n_meta, sub, hpb):
    i = pl.program_id(2)
    tq = q_ref.shape[0]
    tk = tq
    hw = 2 * sub
    mp = km_ref.shape[0]
    lane = lax.broadcasted_iota(jnp.int32, (tq, hw), 1)
    head = lambda hh: slice(hh * hw, (hh + 1) * hw)

    def softmax_step(hh, s, v_aug, first):
        smax = jnp.max(s, axis=1, keepdims=True)
        if first:
            m_new = jnp.broadcast_to(smax, (2 * tq, hw))
        else:
            m_prev = m_scr[hh]
            m_new = jnp.maximum(m_prev, smax)
        p = jnp.concatenate([jnp.exp(s[:, c * hw:(c + 1) * hw] - m_new) for c in range(s.shape[1] // hw)], axis=1)
        pv = _dot(p.astype(bf16), v_aug)
        if first:
            acc_scr[hh] = pv
        else:
            alpha = jnp.exp(m_prev - m_new)
            acc_scr[hh] = jnp.concatenate([alpha, alpha], axis=1) * acc_scr[hh] + pv
        m_scr[hh] = m_new

    for hh in range(hpb):
        q = q_ref[:, head(hh)]
        zero = jnp.zeros_like(q)
        qq_scr[hh] = jnp.concatenate([jnp.where(lane < sub, q, zero), jnp.where(lane >= sub, q, zero)], axis=0)
        s = _dot_nt(qq_scr[hh], km_ref[:, head(hh)])
        col = lax.broadcasted_iota(jnp.int32, s.shape, 1)
        s = jnp.where(col < n_meta, s, NEG_BIG)
        softmax_step(hh, s, jnp.concatenate([vm_ref[:, head(hh)], jnp.ones((mp, hw), bf16)], axis=1), True)

    def block(j, masked):
        start = pl.multiple_of(j * tk, tk)
        for hh in range(hpb):
            kb = k_ref[pl.ds(start, tk), head(hh)]
            vb = v_ref[pl.ds(start, tk), head(hh)]
            s = _dot_nt(qq_scr[hh], kb)
            if masked:
                row = lax.broadcasted_iota(jnp.int32, s.shape, 0) % tq
                col = lax.broadcasted_iota(jnp.int32, s.shape, 1)
                s = jnp.where(col <= row, s, NEG_BIG)
            softmax_step(hh, s, jnp.concatenate([vb, jnp.ones((tk, hw), bf16)], axis=1), False)

    def body(j, c):
        block(j, False)
        return c

    lax.fori_loop(0, i, body, 0)
    block(i, True)

    lam = (jnp.exp(jnp.sum(lq1_ref[...] * lk1_ref[...], axis=1, keepdims=True))
           - jnp.exp(jnp.sum(lq2_ref[...] * lk2_ref[...], axis=1, keepdims=True)) + LAM_INIT)
    for hh in range(hpb):
        acc = acc_scr[hh]
        o = acc[:, :hw] / acc[:, hw:]
        att = o[:tq] - lam * o[tq:]
        ms = jnp.mean(att * att, axis=-1, keepdims=True)
        att = att * lax.rsqrt(ms + EPS) * sg_ref[...] * (1.0 - LAM_INIT)
        o_ref[:, head(hh)] = att.astype(o_ref.dtype)


def _attention(q_rot, k_rot, proj, v_col0, km, vm, lq1, lk1, lq2, lk2, sg, batch, seq, heads, tq, n_meta, sub):
    t, aw = q_rot.shape
    nq = seq // tq
    hw = 2 * sub
    hpb = 2 if heads % 2 == 0 else 1
    bw = hpb * hw
    mp = km.shape[0]
    vec = lambda: pl.BlockSpec((1, sub), lambda b, h, i: (0, 0))
    return pl.pallas_call(
        functools.partial(_attn_kernel, n_meta=n_meta, sub=sub, hpb=hpb),
        grid=(batch, heads // hpb, nq),
        in_specs=[pl.BlockSpec((tq, bw), lambda b, h, i: (b * nq + i, h)),
                  pl.BlockSpec((seq, bw), lambda b, h, i: (b, h)),
                  pl.BlockSpec((seq, bw), lambda b, h, i: (b, v_col0 // hpb + h)),
                  pl.BlockSpec((mp, bw), lambda b, h, i: (0, h)),
                  pl.BlockSpec((mp, bw), lambda b, h, i: (0, h)),
                  vec(), vec(), vec(), vec(),
                  pl.BlockSpec((1, hw), lambda b, h, i: (0, 0))],
        out_specs=pl.BlockSpec((tq, bw), lambda b, h, i: (b * nq + i, h)),
        out_shape=jax.ShapeDtypeStruct((t, aw), bf16),
        scratch_shapes=[pltpu.VMEM((hpb, 2 * tq, hw), bf16), pltpu.VMEM((hpb, 2 * tq, hw), f32),
                        pltpu.VMEM((hpb, 2 * tq, 2 * hw), f32)],
        compiler_params=_cparams(("parallel", "parallel", "arbitrary")),
        name="diff_attention",
    )(q_rot, k_rot, proj, km, vm, lq1, lk1, lq2, lk2, sg)


SSM_BLOCK = 8


def _cmul(ar, ai, br, bi):
    return ar * br - ai * bi, ar * bi + ai * br


def _dot3(a, b):
    ah = a.astype(bf16)
    al = (a - ah.astype(f32)).astype(bf16)
    bh = b.astype(bf16)
    bl = (b - bh.astype(f32)).astype(bf16)
    return _dot(ah, bh) + (_dot(ah, bl) + _dot(al, bh))


def _abar(a_re, a_im, log_dt):
    dt = jnp.exp(log_dt)
    er = jnp.exp(a_re * dt)
    return er * jnp.cos(a_im * dt), er * jnp.sin(a_im * dt)


def _powers(ar, ai, n):
    out = [(jnp.ones_like(ar), jnp.zeros_like(ai)), (ar, ai)]
    for _ in range(2, n + 1):
        out.append(_cmul(*out[-1], ar, ai))
    return out


def _select_blocks(idx, tiles):
    out = tiles[0]
    for k in range(1, len(tiles)):
        out = jnp.where(idx == k, tiles[k], out)
    return out


def _ssd_prep_kernel(ar_ref, ai_ref, ld_ref, arc_ref, aic_ref, ldc_ref, br_ref, bi_ref, cr_ref, ci_ref,
                     m_ref, w_ref, q_ref, pr_ref, ps_ref, *, n_state, n_chan):
    blk = SSM_BLOCK
    a_re, a_im = ar_ref[0], ai_ref[0]
    abr, abi = _abar(a_re, a_im, ld_ref[0])
    den = a_re * a_re + a_im * a_im
    nr, ni = abr - 1.0, abi
    f_re = (nr * a_re + ni * a_im) / den
    f_im = (ni * a_re - nr * a_im) / den
    pw = _powers(abr, abi, blk)
    b_re, b_im = br_ref[0], bi_ref[0]
    bbr, bbi = _cmul(f_re, f_im, b_re, b_im)
    rblk = lax.broadcasted_iota(jnp.int32, b_re.shape, 0) // n_chan
    lane = lax.broadcasted_iota(jnp.int32, b_re.shape, 1)
    is_re = lane < n_state
    full = lambda v: jnp.broadcast_to(v, b_re.shape)

    def b_times_power(order):
        pr = _select_blocks(rblk, [full(pw[k][0]) for k in order])
        pi = _select_blocks(rblk, [full(pw[k][1]) for k in order])
        xr, xi = _cmul(bbr, bbi, pr, pi)
        return jnp.where(is_re, xr, xi)

    w_ref[0] = b_times_power([blk - 1 - i for i in range(blk)]).astype(bf16)
    c_re, c_im = cr_ref[0], ci_ref[0]
    acr, aci = _abar(arc_ref[0], aic_ref[0], ldc_ref[0])
    pwc = _powers(acr, aci, blk)
    jblk = lax.broadcasted_iota(jnp.int32, c_re.shape, 1) // n_chan
    pcr = _select_blocks(jblk, [pwc[j + 1][0] for j in range(blk)])
    pci = _select_blocks(jblk, [pwc[j + 1][1] for j in range(blk)])
    qr, qi = _cmul(c_re, c_im, pcr, pci)
    q_ref[0] = jnp.concatenate([qr, -qi], axis=0).astype(bf16)
    ccat = jnp.concatenate([c_re, -c_im], axis=0)
    kwide = _dot3(b_times_power(list(range(blk))), ccat)
    jb = lax.broadcasted_iota(jnp.int32, (n_chan, kwide.shape[1]), 1) // n_chan
    m_rows = []
    for i in range(blk):
        mi = jnp.zeros((n_chan, kwide.shape[1]), f32)
        for dd in range(blk - i):
            mi = jnp.where(jb == i + dd, kwide[dd * n_chan:(dd + 1) * n_chan], mi)
        m_rows.append(mi)
    m_ref[0] = jnp.concatenate(m_rows, axis=0).astype(bf16)
    row_re = lax.broadcasted_iota(jnp.int32, a_re.shape, 1) < n_state
    al = pw[blk]
    pl_ = _powers(al[0], al[1], SUBLANES)[1:]
    pr_ref[0] = _stack_rows([p[0] for p in pl_])
    ps_ref[0] = _stack_rows([jnp.where(row_re, -p[1], p[1]) for p in pl_])


def _ssd_prep(rows, cols, b_t, c_t, n_state, n_chan):
    g = b_t[0].shape[0]
    row_spec = lambda: pl.BlockSpec((1, 1, LANES), lambda i: (i, 0, 0))
    col_spec = lambda: pl.BlockSpec((1, n_state, LANES), lambda i: (i, 0, 0))
    sq_spec = lambda: pl.BlockSpec((1, LANES, LANES), lambda i: (i, 0, 0))
    tab_spec = lambda: pl.BlockSpec((1, SUBLANES, LANES), lambda i: (i, 0, 0))
    sq = jax.ShapeDtypeStruct((g, LANES, LANES), bf16)
    tab = jax.ShapeDtypeStruct((g, SUBLANES, LANES), f32)
    return pl.pallas_call(
        functools.partial(_ssd_prep_kernel, n_state=n_state, n_chan=n_chan),
        grid=(g,),
        in_specs=[row_spec(), row_spec(), row_spec(), col_spec(), col_spec(), col_spec(),
                  sq_spec(), sq_spec(), col_spec(), col_spec()],
        out_specs=[sq_spec(), sq_spec(), sq_spec(), tab_spec(), tab_spec()],
        out_shape=[sq, sq, sq, tab, tab],
        compiler_params=_cparams(("parallel",)),
        name="ssm_prep",
    )(*rows, *cols, *b_t, *c_t)


def _ssd_kernel(u_ref, um_ref, m_ref, w_ref, q_ref, pr_ref, ps_ref, d_ref, y_ref, s_scr, sw_scr, xp_scr,
                *, rows_per_seq, meta_rows, n_state):
    gb, r, _ = u_ref.shape
    rowmod = lax.broadcasted_iota(jnp.int32, (r, LANES), 0) % SUBLANES
    row8 = lax.broadcasted_iota(jnp.int32, (SUBLANES, LANES), 0)

    def times(x, mr, ms):
        return x * mr + pltpu.roll(x, n_state, 1) * ms

    x0 = []
    for g in range(gb):
        w = w_ref[g]
        pr, ps = pr_ref[g], ps_ref[g]
        s = _dot(u_ref[g], w)
        for k in (1, 2, 4):
            t = times(pltpu.roll(s, k, 0), pr[k - 1:k], ps[k - 1:k])
            s = s + jnp.where(rowmod >= k, t, 0.0)
        s_scr[g] = s
        sw_scr[g] = pltpu.roll(s, n_state, 1)
        sm = _dot(um_ref[g], w)
        x = sm[0:1]
        for mrow in range(1, meta_rows):
            x = times(x, pr[0:1], ps[0:1]) + sm[mrow:mrow + 1]
        x0 += [x, pltpu.roll(x, n_state, 1)]

    def body(b, carry):
        start = b % (rows_per_seq // SUBLANES) == 0
        rows = pl.ds(pl.multiple_of(b * SUBLANES, SUBLANES), SUBLANES)
        out = []
        for g in range(gb):
            c = jnp.broadcast_to(jnp.where(start, x0[2 * g], carry[2 * g]), (SUBLANES, LANES))
            cw = jnp.broadcast_to(jnp.where(start, x0[2 * g + 1], carry[2 * g + 1]), (SUBLANES, LANES))
            pr, ps = pr_ref[g], ps_ref[g]
            xs = s_scr[g, rows, :] + (c * pr + cw * ps)
            xw = sw_scr[g, rows, :] + (cw * pr - c * ps)
            xp_scr[g, rows, :] = jnp.where(row8 == 0, c, pltpu.roll(xs, 1, 0))
            out += [xs[SUBLANES - 1:SUBLANES], xw[SUBLANES - 1:SUBLANES]]
        return tuple(out)

    lax.fori_loop(0, r // SUBLANES, body, tuple(x0))

    for g in range(gb):
        u = u_ref[g]
        y = _dot(u, m_ref[g]) + _dot(xp_scr[g].astype(bf16), q_ref[g]) + d_ref[g] * u.astype(f32)
        y_ref[g] = _gelu(y).astype(y_ref.dtype)


def _ssd(u_t, um_t, prep, d_t, rows_per_seq, meta_rows, n_state, gb):
    m, w, q, pr, ps = prep
    g, r, _ = u_t.shape
    blk3 = lambda a: pl.BlockSpec((gb,) + a.shape[1:], lambda i: (i, 0, 0))
    return pl.pallas_call(
        functools.partial(_ssd_kernel, rows_per_seq=rows_per_seq, meta_rows=meta_rows, n_state=n_state),
        grid=(g // gb,),
        in_specs=[blk3(u_t), blk3(um_t), blk3(m), blk3(w), blk3(q), blk3(pr), blk3(ps), blk3(d_t)],
        out_specs=blk3(u_t),
        out_shape=jax.ShapeDtypeStruct(u_t.shape, bf16),
        scratch_shapes=[pltpu.VMEM((gb, r, LANES), f32)] * 3,
        compiler_params=_cparams(("parallel",)),
        name="ssm_blocked",
    )(u_t, um_t, m, w, q, pr, ps, d_t)


def _attn_out_kernel(a_ref, w_ref, g_ref, o_ref):
    y = _dot(a_ref[...], w_ref[...])
    o_ref[...] = (_sigmoid(g_ref[...].astype(f32)) * y).astype(o_ref.dtype)


def _glu_mix_kernel(s_ref, wa_ref, wb_ref, g_ref, ya_ref, o_ref):
    s = s_ref[...]
    ga = _dot(s, wa_ref[...])
    gb = _dot(s, wb_ref[...])
    yb = ga * _sigmoid(gb)
    o_ref[...] = (ya_ref[...].astype(f32) + _sigmoid(g_ref[...].astype(f32)) * yb).astype(o_ref.dtype)


def _out_proj_kernel(m_ref, w_ref, x_ref, o_ref):
    o_ref[...] = x_ref[...] + _dot(m_ref[...], w_ref[...])


def _peer_query_kernel(h_ref, g_ref, wq_ref, kk_ref, hn_ref, st_ref):
    x = h_ref[...]
    ms = jnp.mean(x * x, axis=-1, keepdims=True)
    hn_f = x * lax.rsqrt(ms + EPS) * g_ref[...]
    hn_ref[...] = hn_f.T.astype(bf16)
    q = _dot(hn_f.astype(bf16), wq_ref[...]).astype(bf16)
    n_heads = kk_ref.shape[0]
    half = kk_ref.shape[3]
    for h in range(n_heads):
        for side in range(2):
            c0 = (2 * h + side) * half
            st_ref[h, side] = _dot_nt(kk_ref[h, side], q[:, c0:c0 + half])


def _peer_query(h1, g, wq, kk, tm):
    t, d = h1.shape
    nh, _, nk, half = kk.shape
    return pl.pallas_call(
        _peer_query_kernel,
        grid=(t // tm,),
        in_specs=[pl.BlockSpec((tm, d), lambda i: (i, 0)),
                  pl.BlockSpec((1, d), lambda i: (0, 0)),
                  pl.BlockSpec(wq.shape, lambda i: (0, 0)),
                  pl.BlockSpec(kk.shape, lambda i: (0, 0, 0, 0))],
        out_specs=[pl.BlockSpec((d, tm), lambda i: (0, i)),
                   pl.BlockSpec((nh, 2, nk, tm), lambda i: (0, 0, 0, i))],
        out_shape=[jax.ShapeDtypeStruct((d, t), bf16), jax.ShapeDtypeStruct((nh, 2, nk, t), f32)],
        compiler_params=_cparams(("parallel",)),
        name="peer_query",
    )(h1, g, wq, kk)


def _top_desc(s, n):
    vals = []
    cur = s
    rank = jnp.full(s.shape, float(n), f32)
    for a in range(n):
        v = jnp.max(cur, axis=0, keepdims=True)
        vals.append(v)
        hit = cur == v
        rank = jnp.where(hit, float(a), rank)
        cur = jnp.where(hit, -jnp.inf, cur)
    return _stack_rows(vals), rank


def _peer_topk_kernel(st_ref, r2_ref, e2_ref, n1_ref, e1_ref):
    chunk = LANES
    for c in range(st_ref.shape[3] // chunk):
        cols = slice(c * chunk, (c + 1) * chunk)
        s1 = st_ref[0, 0, :, cols]
        s2 = st_ref[0, 1, :, cols]
        v1, rank1 = _top_desc(s1, TOPK)
        v2, rank2 = _top_desc(s2, TOPK)
        n_b = lambda a: -(-(TOPK // (a + 1)) // SUBLANES) * SUBLANES
        cand = jnp.concatenate([v1[a:a + 1] + v2[:n_b(a)] for a in range(TOPK)], axis=0)
        best, _ = _top_desc(cand, TOPK)
        tau = best[TOPK - 1:TOPK]
        z = jnp.sum(jnp.exp(best - best[0:1]), axis=0, keepdims=True)
        n1 = jnp.zeros(s1.shape, f32)
        for a in range(TOPK):
            cnt = jnp.sum(jnp.where((v1[a:a + 1] + v2) >= tau, 1.0, 0.0), axis=0, keepdims=True)
            n1 = jnp.where(rank1 == float(a), cnt, n1)
        r2_ref[0, :, cols] = rank2.astype(bf16)
        e2_ref[0, :, cols] = jnp.exp(s2 - v2[0:1]).astype(bf16)
        n1_ref[0, :, cols] = n1
        e1_ref[0, :, cols] = jnp.exp(s1 - v1[0:1]) / z


def _peer_topk(st, tl):
    nh, _, nk, t = st.shape
    spec = lambda: pl.BlockSpec((1, nk, tl), lambda h, i: (h, 0, i))
    shp = lambda dt: jax.ShapeDtypeStruct((nh, nk, t), dt)
    return pl.pallas_call(
        _peer_topk_kernel,
        grid=(nh, t // tl),
        in_specs=[pl.BlockSpec((1, 2, nk, tl), lambda h, i: (h, 0, 0, i))],
        out_specs=[spec(), spec(), spec(), spec()],
        out_shape=[shp(bf16), shp(bf16), shp(f32), shp(f32)],
        compiler_params=_cparams(("parallel", "parallel")),
        name="peer_topk",
    )(st)


def _peer_dense_kernel(hn_ref, u_ref, vt_ref, r2_ref, e2_ref, n1_ref, e1_ref, h1_ref, o_ref,
                       acc_scr, a0_scr, a1_scr, *, nb, tn):
    e = pl.program_id(1)
    n_heads, nk, tm = r2_ref.shape
    eb = u_ref.shape[0]
    pk = 2 * SUBLANES

    @pl.when(e == 0)
    def _():
        acc_scr[...] = jnp.zeros_like(acc_scr)
        a1_scr[...] = jnp.zeros_like(a1_scr)

    def step(a_cur, a_prev):
        a_cur[...] = _dot(u_ref[...], hn_ref[...])
        blk = jnp.maximum(e - 1, 0)
        for c in range(tm // tn):
            cols = slice(c * tn, (c + 1) * tn)
            w_rows = []
            for ii in range(eb // nk):
                i1 = blk * (eb // nk) + ii
                gate = None
                for h in range(n_heads):
                    n1 = jnp.broadcast_to(n1_ref[h, pl.ds(i1, 1), cols], (pk, tn)).astype(bf16)
                    e1 = jnp.broadcast_to(e1_ref[h, pl.ds(i1, 1), cols], (pk, tn)).astype(bf16)
                    r2 = r2_ref[h, :, cols].reshape(nk // pk, pk, tn)
                    e2 = e2_ref[h, :, cols].reshape(nk // pk, pk, tn)
                    term = jnp.where(r2 < n1[None], e2, jnp.zeros_like(e2)) * e1[None]
                    gate = term if gate is None else gate + term
                g = _gelu(a_prev[ii * nk:(ii + 1) * nk, cols]).astype(bf16)
                w_rows.append(gate.reshape(nk, tn) * g)
            acc_scr[:, cols] += _dot(vt_ref[...], jnp.concatenate(w_rows, axis=0))

    @pl.when(e % 2 == 0)
    def _():
        step(a0_scr, a1_scr)

    @pl.when(e % 2 == 1)
    def _():
        step(a1_scr, a0_scr)

    @pl.when(e == nb)
    def _():
        o_ref[...] = h1_ref[...] + acc_scr[...].T


def _peer_dense(hn_t, u_b, vt_b, r2, e2, n1, e1, h1, tm, eb):
    d, t = hn_t.shape
    nb = u_b.shape[0] // eb
    nh, nk, _ = r2.shape
    tok = lambda: pl.BlockSpec((nh, nk, tm), lambda i, e: (0, 0, i))
    return pl.pallas_call(
        functools.partial(_peer_dense_kernel, nb=nb, tn=_tile(tm, 2 * LANES)),
        grid=(t // tm, nb + 1),
        in_specs=[pl.BlockSpec((d, tm), lambda i, e: (0, i)),
                  pl.BlockSpec((eb, d), lambda i, e: (jnp.minimum(e, nb - 1), 0)),
                  pl.BlockSpec((d, eb), lambda i, e: (0, jnp.maximum(e - 1, 0))),
                  tok(), tok(), tok(), tok(),
                  pl.BlockSpec((tm, d), lambda i, e: (i, 0), pipeline_mode=pl.Buffered(1))],
        out_specs=pl.BlockSpec((tm, d), lambda i, e: (i, 0)),
        out_shape=jax.ShapeDtypeStruct((t, d), f32),
        scratch_shapes=[pltpu.VMEM((d, tm), f32), pltpu.VMEM((eb, tm), f32), pltpu.VMEM((eb, tm), f32)],
        compiler_params=_cparams(("parallel", "arbitrary")),
        name="peer_dense",
    )(hn_t, u_b, vt_b, r2, e2, n1, e1, h1)


def _tile(n, pref):
    if n <= pref:
        return n
    t = pref // LANES * LANES
    while n % t:
        t -= LANES
    assert t > 0, (n, pref)
    return t


def kernel(x, meta_tokens, norm1_g, w_in, q_norm_g, k_norm_g, lambda_q1, lambda_k1, lambda_q2, lambda_k2, subln_g, w_attn_branch, ssm_a_re, ssm_a_im, ssm_log_dt, ssm_b_re, ssm_b_im, ssm_c_re, ssm_c_im, ssm_d, w_glu, w_out, norm2_g, peer_w_q, peer_k1, peer_k2, peer_u, peer_v):
    assert norm1_g.shape[0] == 1, "single-layer block only"
    batch, seq, d = x.shape
    n_meta = meta_tokens.shape[0]
    sub = q_norm_g.shape[-1]
    aw = w_attn_branch.shape[1]
    heads = aw // (2 * sub)
    n_groups, n_state, n_chan = ssm_b_re.shape[1:]
    sw = n_groups * n_chan
    assert 2 * sub == LANES and LANES % n_chan == 0 and sw % LANES == 0
    assert n_meta % SUBLANES == 0 and n_meta <= LANES
    t = batch * seq
    n_in = w_in.shape[2]
    assert n_in == 3 * aw + sw + 2 * d

    x2 = x.reshape(t, d)
    w_in_b = w_in[0].astype(bf16)

    proj = _rms_matmul(x2, norm1_g, w_in_b, _tile(t, 1024), _tile(n_in, 1024), bf16, "in_proj")
    proj_m = _rms_matmul(meta_tokens, norm1_g, w_in_b, n_meta, _tile(n_in, 1024), bf16, "in_proj_meta")

    half = sub // 2
    inv_freq = jnp.power(ROPE_THETA, -jnp.arange(half, dtype=f32) / half)
    ang = jnp.arange(n_meta + seq, dtype=f32)[:, None] * inv_freq[None, :]
    cos_t = jnp.tile(jnp.cos(ang), (1, LANES // half))
    sin_t = jnp.tile(jnp.concatenate([-jnp.sin(ang), jnp.sin(ang)], axis=1), (1, LANES // sub))
    lane = jnp.arange(LANES)
    jmat = (lane[:, None] // sub == lane[None, :] // sub).astype(bf16)
    gq = jnp.tile(q_norm_g, (1, aw // sub))
    gk = jnp.tile(k_norm_g, (1, aw // sub))
    tr = _tile(seq, 512)
    q_rot = _normrot(proj, 0, aw, gq, cos_t[n_meta:], sin_t[n_meta:], jmat, tr, sub ** -0.5, sub, "q_normrot")
    k_rot = _normrot(proj, 1, aw, gk, cos_t[n_meta:], sin_t[n_meta:], jmat, tr, 1.0, sub, "k_normrot")
    km_rot = _normrot(proj_m, 1, aw, gk, cos_t[:n_meta], sin_t[:n_meta], jmat, n_meta, 1.0, sub, "k_normrot_meta")
    km = jnp.pad(km_rot, ((0, LANES - n_meta), (0, 0)))
    vm = jnp.pad(proj_m[:, 2 * aw:3 * aw], ((0, LANES - n_meta), (0, 0)))

    att = _attention(q_rot, k_rot, proj, 2 * aw // LANES, km, vm, lambda_q1, lambda_k1, lambda_q2, lambda_k2,
                     subln_g, batch, seq, heads, _tile(seq, 512), n_meta, sub)

    blk = SSM_BLOCK
    assert blk * n_chan == LANES and 2 * n_state == LANES and seq % (blk * SUBLANES) == 0 and n_meta % blk == 0
    u0 = 3 * aw
    to_groups = lambda a: a.reshape(-1, blk, n_groups, n_chan).transpose(2, 0, 1, 3).reshape(n_groups, -1, LANES)
    u_t = to_groups(proj[:, u0:u0 + sw])
    meta_rows = n_meta // blk
    um_t = jnp.pad(to_groups(proj_m[:, u0:u0 + sw]), ((0, 0), (0, SUBLANES - meta_rows), (0, 0)))
    row_l = lambda a: jnp.tile(a[0], (1, 2)).reshape(n_groups, 1, LANES)
    col_l = lambda a: jnp.broadcast_to(a[0][:, :, None], (n_groups, n_state, LANES))
    ldt2 = jnp.broadcast_to(ssm_log_dt[0][:, None], (n_groups, n_state))[None]
    b_l = lambda b: jnp.tile(b[0].transpose(0, 2, 1), (1, blk, 2))
    c_l = lambda c: jnp.tile(c[0].transpose(0, 2, 1), (1, 1, blk))
    prep = _ssd_prep((row_l(ssm_a_re), row_l(ssm_a_im), row_l(ldt2)), (col_l(ssm_a_re), col_l(ssm_a_im), col_l(ldt2)),
                     (b_l(ssm_b_re), b_l(ssm_b_im)), (c_l(ssm_c_re), c_l(ssm_c_im)), n_state, n_chan)
    d_t = jnp.tile(ssm_d[0].reshape(n_groups, 1, n_chan), (1, 1, blk))
    y_t = _ssd(u_t, um_t, prep, d_t, seq // blk, meta_rows, n_state, math.gcd(n_groups, 8))
    ys = y_t.reshape(n_groups, -1, blk, n_chan).transpose(1, 2, 0, 3).reshape(t, sw)

    tm = _tile(t, 512)
    tn = _tile(d, 1024)
    nj = d // tn
    ga_col0 = (3 * aw + sw) // tn
    gb_col0 = (3 * aw + sw + d) // tn
    grid = (t // tm, nj)
    sem = _cparams(("parallel", "arbitrary"))
    row_full = lambda k: pl.BlockSpec((tm, k), lambda i, j: (i, 0))
    w_col = lambda k, off=0: pl.BlockSpec((k, tn), lambda i, j: (0, j + off))
    out_tile = pl.BlockSpec((tm, tn), lambda i, j: (i, j))
    yag = pl.pallas_call(
        _attn_out_kernel, grid=grid,
        in_specs=[row_full(aw), w_col(aw), pl.BlockSpec((tm, tn), lambda i, j: (i, ga_col0 + j))],
        out_specs=out_tile, out_shape=jax.ShapeDtypeStruct((t, d), bf16), compiler_params=sem, name="attn_out",
    )(att, w_attn_branch[0].astype(bf16), proj)
    w_glu_b = w_glu[0].astype(bf16)
    mix = pl.pallas_call(
        _glu_mix_kernel, grid=grid,
        in_specs=[row_full(sw), w_col(sw), w_col(sw, nj), pl.BlockSpec((tm, tn), lambda i, j: (i, gb_col0 + j)),
                  out_tile],
        out_specs=out_tile, out_shape=jax.ShapeDtypeStruct((t, d), bf16), compiler_params=sem, name="glu_mix",
    )(ys, w_glu_b, w_glu_b, proj, yag)
    h1 = pl.pallas_call(
        _out_proj_kernel, grid=grid,
        in_specs=[row_full(d), w_col(d), out_tile],
        out_specs=out_tile, out_shape=jax.ShapeDtypeStruct((t, d), f32), compiler_params=sem, name="out_proj",
    )(mix, w_out[0].astype(bf16), x2)

    kk = jnp.stack([peer_k1[0], peer_k2[0]], axis=1).astype(bf16)
    assert kk.shape[2] == LANES and kk.shape[3] == LANES
    hn2, st = _peer_query(h1, norm2_g, peer_w_q[0].astype(bf16), kk, _tile(t, 512))
    r2, e2, n1, e1 = _peer_topk(st, _tile(t, 512))
    u_b = peer_u[0].astype(bf16)
    vt_b = peer_v[0].T.astype(bf16)
    out = _peer_dense(hn2, u_b, vt_b, r2, e2, n1, e1, h1, _tile(t, 512), _tile(u_b.shape[0], 512))
    return out.reshape(batch, seq, d)
```

```python
import functools
import math

import jax
import jax.numpy as jnp
from jax import lax
from jax.experimental import pallas as pl
from jax.experimental.pallas import tpu as pltpu

EPS = 1e-6
ROPE_THETA = 10000.0
LAM_INIT = 0.8 - 0.6 * math.exp(-0.3 * 0)
TOPK = 16
LANES = 128
SUBLANES = 8
NEG_BIG = -1e30
VMEM_LIMIT = 56 * 1024 * 1024

bf16 = jnp.bfloat16
f32 = jnp.float32


def _cparams(sem):
    return pltpu.CompilerParams(dimension_semantics=sem, vmem_limit_bytes=VMEM_LIMIT)


def _dot(a, b):
    return jnp.dot(a, b, preferred_element_type=f32)


def _dot_nt(a, b):
    return lax.dot_general(a, b, (((1,), (1,)), ((), ())), preferred_element_type=f32)


def _sigmoid(x):
    return 1.0 / (1.0 + jnp.exp(-x))


def _gelu(x):
    return 0.5 * x * (1.0 + lax.erf(x * (1.0 / math.sqrt(2.0))))


def _stack_rows(rows):
    k = len(rows)
    idx = lax.broadcasted_iota(jnp.int32, (k, rows[0].shape[1]), 0)
    out = jnp.broadcast_to(rows[0], idx.shape)
    for j in range(1, k):
        out = jnp.where(idx == j, rows[j], out)
    return out


def _rms_matmul_kernel(x_ref, g_ref, w_ref, o_ref, xn_ref):
    @pl.when(pl.program_id(1) == 0)
    def _():
        x = x_ref[...]
        ms = jnp.mean(x * x, axis=-1, keepdims=True)
        xn_ref[...] = (x * lax.rsqrt(ms + EPS) * g_ref[...]).astype(bf16)

    o_ref[...] = _dot(xn_ref[...], w_ref[...]).astype(o_ref.dtype)


def _rms_matmul(x, g, w, tm, tn, out_dtype, name):
    m, k = x.shape
    n = w.shape[1]
    return pl.pallas_call(
        _rms_matmul_kernel,
        grid=(m // tm, n // tn),
        in_specs=[pl.BlockSpec((tm, k), lambda i, j: (i, 0)),
                  pl.BlockSpec((1, k), lambda i, j: (0, 0)),
                  pl.BlockSpec((k, tn), lambda i, j: (0, j))],
        out_specs=pl.BlockSpec((tm, tn), lambda i, j: (i, j)),
        out_shape=jax.ShapeDtypeStruct((m, n), out_dtype),
        scratch_shapes=[pltpu.VMEM((tm, k), bf16)],
        compiler_params=_cparams(("parallel", "arbitrary")),
        name=name,
    )(x, g, w)


def _normrot_kernel(x_ref, g_ref, cos_ref, sin_ref, j_ref, o_ref, *, scale, sub):
    width = x_ref.shape[1]
    cos = cos_ref[...]
    sin = sin_ref[...]
    jmat = j_ref[...]
    lane = lax.broadcasted_iota(jnp.int32, cos.shape, 1)
    first_half = (lane % sub) < (sub // 2)
    for t in range(width // LANES):
        sl = slice(t * LANES, (t + 1) * LANES)
        x = x_ref[:, sl].astype(f32)
        x2 = x * x
        hi = x2.astype(bf16)
        lo = (x2 - hi.astype(f32)).astype(bf16)
        ssq = _dot(hi, jmat) + _dot(lo, jmat)
        xn = x * lax.rsqrt(ssq * (1.0 / sub) + EPS) * g_ref[:, sl]
        partner = jnp.where(first_half, pltpu.roll(xn, LANES - sub // 2, 1), pltpu.roll(xn, sub // 2, 1))
        o_ref[:, sl] = ((xn * cos + partner * sin) * scale).astype(o_ref.dtype)


def _normrot(x, col_block, width, g_t, cos, sin, jmat, tm, scale, sub, name):
    m = x.shape[0]
    nt = cos.shape[0] // tm
    return pl.pallas_call(
        functools.partial(_normrot_kernel, scale=scale, sub=sub),
        grid=(m // tm,),
        in_specs=[pl.BlockSpec((tm, width), lambda i: (i, col_block)),
                  pl.BlockSpec((1, width), lambda i: (0, 0)),
                  pl.BlockSpec((tm, LANES), lambda i: (i % nt, 0)),
                  pl.BlockSpec((tm, LANES), lambda i: (i % nt, 0)),
                  pl.BlockSpec((LANES, LANES), lambda i: (0, 0))],
        out_specs=pl.BlockSpec((tm, width), lambda i: (i, 0)),
        out_shape=jax.ShapeDtypeStruct((m, width), bf16),
        compiler_params=_cparams(("parallel",)),
        name=name,
    )(x, g_t, cos, sin, jmat)


def _attn_kernel(q_ref, k_ref, v_ref, km_ref, vm_ref, lq1_ref, lk1_ref, lq2_ref, lk2_ref, sg_ref,
                 o_ref, qq_scr, m_scr, acc_scr, *, n_meta, sub, hpb):
    i = pl.program_id(2)
    tq = q_ref.shape[0]
    tk = tq
    hw = 2 * sub
    mp = km_ref.shape[0]
    lane = lax.broadcasted_iota(jnp.int32, (tq, hw), 1)
    head = lambda hh: slice(hh * hw, (hh + 1) * hw)

    def softmax_step(hh, s, v_aug, first):
        smax = jnp.max(s, axis=1, keepdims=True)
        if first:
            m_new = jnp.broadcast_to(smax, (2 * tq, hw))
        else:
            m_prev = m_scr[hh]
            m_new = jnp.maximum(m_prev, smax)
        p = jnp.concatenate([jnp.exp2(s[:, c * hw:(c + 1) * hw] - m_new) for c in range(s.shape[1] // hw)], axis=1)
        pv = _dot(p.astype(bf16), v_aug)
        if first:
            acc_scr[hh] = pv
        else:
            alpha = jnp.exp2(m_prev - m_new)
            acc_scr[hh] = jnp.concatenate([alpha, alpha], axis=1) * acc_scr[hh] + pv
        m_scr[hh] = m_new

    for hh in range(hpb):
        q = q_ref[:, head(hh)]
        zero = jnp.zeros_like(q)
        qq_scr[hh] = jnp.concatenate([jnp.where(lane < sub, q, zero), jnp.where(lane >= sub, q, zero)], axis=0)
        s = _dot_nt(qq_scr[hh], km_ref[:, head(hh)])
        col = lax.broadcasted_iota(jnp.int32, s.shape, 1)
        s = jnp.where(col < n_meta, s, NEG_BIG)
        softmax_step(hh, s, jnp.concatenate([vm_ref[:, head(hh)], jnp.ones((mp, hw), bf16)], axis=1), True)

    def block(j, masked):
        start = pl.multiple_of(j * tk, tk)
        for hh in range(hpb):
            kb = k_ref[pl.ds(start, tk), head(hh)]
            vb = v_ref[pl.ds(start, tk), head(hh)]
            s = _dot_nt(qq_scr[hh], kb)
            if masked:
                row = lax.broadcasted_iota(jnp.int32, s.shape, 0) % tq
                col = lax.broadcasted_iota(jnp.int32, s.shape, 1)
                s = jnp.where(col <= row, s, NEG_BIG)
            softmax_step(hh, s, jnp.concatenate([vb, jnp.ones((tk, hw), bf16)], axis=1), False)

    def body(j, c):
        block(j, False)
        return c

    lax.fori_loop(0, i, body, 0)
    block(i, True)

    lam = (jnp.exp(jnp.sum(lq1_ref[...] * lk1_ref[...], axis=1, keepdims=True))
           - jnp.exp(jnp.sum(lq2_ref[...] * lk2_ref[...], axis=1, keepdims=True)) + LAM_INIT)
    for hh in range(hpb):
        acc = acc_scr[hh]
        o = acc[:, :hw] / acc[:, hw:]
        att = o[:tq] - lam * o[tq:]
        ms = jnp.mean(att * att, axis=-1, keepdims=True)
        att = att * lax.rsqrt(ms + EPS) * sg_ref[...] * (1.0 - LAM_INIT)
        o_ref[:, head(hh)] = att.astype(o_ref.dtype)


def _attention(q_rot, k_rot, proj, v_col0, km, vm, lq1, lk1, lq2, lk2, sg, batch, seq, heads, tq, n_meta, sub):
    t, aw = q_rot.shape
    nq = seq // tq
    hw = 2 * sub
    hpb = 2 if heads % 2 == 0 else 1
    bw = hpb * hw
    mp = km.shape[0]
    vec = lambda: pl.BlockSpec((1, sub), lambda b, h, i: (0, 0))
    return pl.pallas_call(
        functools.partial(_attn_kernel, n_meta=n_meta, sub=sub, hpb=hpb),
        grid=(batch, heads // hpb, nq),
        in_specs=[pl.BlockSpec((tq, bw), lambda b, h, i: (b * nq + i, h)),
                  pl.BlockSpec((seq, bw), lambda b, h, i: (b, h)),
                  pl.BlockSpec((seq, bw), lambda b, h, i: (b, v_col0 // hpb + h)),
                  pl.BlockSpec((mp, bw), lambda b, h, i: (0, h)),
                  pl.BlockSpec((mp, bw), lambda b, h, i: (0, h)),
                  vec(), vec(), vec(), vec(),
                  pl.BlockSpec((1, hw), lambda b, h, i: (0, 0))],
        out_specs=pl.BlockSpec((tq, bw), lambda b, h, i: (b * nq + i, h)),
        out_shape=jax.ShapeDtypeStruct((t, aw), bf16),
        scratch_shapes=[pltpu.VMEM((hpb, 2 * tq, hw), bf16), pltpu.VMEM((hpb, 2 * tq, hw), f32),
                        pltpu.VMEM((hpb, 2 * tq, 2 * hw), f32)],
        compiler_params=_cparams(("parallel", "parallel", "arbitrary")),
        name="diff_attention",
    )(q_rot, k_rot, proj, km, vm, lq1, lk1, lq2, lk2, sg)


def _cmul(ar, ai, br, bi):
    return ar * br - ai * bi, ar * bi + ai * br


def _ssm_prep_kernel(are_ref, aim_ref, ldt_ref, bre_ref, bim_ref, cre_ref, cim_ref,
                     pwr_ref, pwi_ref, bdr_ref, bdi_ref, bor_ref, boi_ref, *, n_state, n_chan):
    a_re = are_ref[...]
    a_im = aim_ref[...]
    dt = jnp.exp(ldt_ref[...])
    er = jnp.exp(a_re * dt)
    ab_re = er * jnp.cos(a_im * dt)
    ab_im = er * jnp.sin(a_im * dt)
    nr = ab_re - 1.0
    ni = ab_im
    den = a_re * a_re + a_im * a_im
    f_re = (nr * a_re + ni * a_im) / den
    f_im = (ni * a_re - nr * a_im) / den
    p1 = (ab_re, ab_im)
    p2 = _cmul(*p1, *p1)
    p3 = _cmul(*p2, *p1)
    p4 = _cmul(*p2, *p2)
    p5 = _cmul(*p4, *p1)
    p6 = _cmul(*p4, *p2)
    p7 = _cmul(*p4, *p3)
    p8 = _cmul(*p4, *p4)
    pw = (p1, p2, p3, p4, p5, p6, p7, p8)
    pwr_ref[...] = _stack_rows([p[0] for p in pw])
    pwi_ref[...] = _stack_rows([p[1] for p in pw])
    b_re = bre_ref[0]
    b_im = bim_ref[0]
    r = lax.broadcasted_iota(jnp.int32, b_re.shape, 0) // n_chan
    c = lax.broadcasted_iota(jnp.int32, b_re.shape, 1) // n_state
    keep = r == c
    bdr_ref[0] = jnp.where(keep, f_re * b_re - f_im * b_im, 0.0).astype(bf16)
    bdi_ref[0] = jnp.where(keep, f_re * b_im + f_im * b_re, 0.0).astype(bf16)
    c_re = cre_ref[0]
    c_im = cim_ref[0]
    r = lax.broadcasted_iota(jnp.int32, c_re.shape, 0) // n_state
    c = lax.broadcasted_iota(jnp.int32, c_re.shape, 1) // n_chan
    keep = r == c
    bor_ref[0] = jnp.where(keep, c_re, 0.0).astype(bf16)
    boi_ref[0] = jnp.where(keep, c_im, 0.0).astype(bf16)


def _ssm_prep(a_re, a_im, ldt, b_re_t, b_im_t, c_re_t, c_im_t, n_state, n_chan):
    ngt, _, sl = b_re_t.shape
    gp = a_re.shape[1]
    lane_vec = lambda: pl.BlockSpec((1, sl), lambda g: (0, g))
    bin_spec = lambda: pl.BlockSpec((1, LANES, sl), lambda g: (g, 0, 0))
    bout_spec = lambda: pl.BlockSpec((1, sl, LANES), lambda g: (g, 0, 0))
    return pl.pallas_call(
        functools.partial(_ssm_prep_kernel, n_state=n_state, n_chan=n_chan),
        grid=(ngt,),
        in_specs=[lane_vec(), lane_vec(), lane_vec(), bin_spec(), bin_spec(), bout_spec(), bout_spec()],
        out_specs=[pl.BlockSpec((SUBLANES, sl), lambda g: (0, g)), pl.BlockSpec((SUBLANES, sl), lambda g: (0, g)),
                   bin_spec(), bin_spec(), bout_spec(), bout_spec()],
        out_shape=[jax.ShapeDtypeStruct((SUBLANES, gp), f32), jax.ShapeDtypeStruct((SUBLANES, gp), f32),
                   jax.ShapeDtypeStruct(b_re_t.shape, bf16), jax.ShapeDtypeStruct(b_re_t.shape, bf16),
                   jax.ShapeDtypeStruct(c_re_t.shape, bf16), jax.ShapeDtypeStruct(c_re_t.shape, bf16)],
        compiler_params=_cparams(("parallel",)),
        name="ssm_prep",
    )(a_re, a_im, ldt, b_re_t, b_im_t, c_re_t, c_im_t)


def _ssm_scan_kernel(u_ref, bdr_ref, bdi_ref, bor_ref, boi_ref, pwr_ref, pwi_ref, d_ref, x0r_ref, x0i_ref,
                     y_ref, xfr_ref, xfi_ref, xr_scr, xi_scr, cr_scr, ci_scr):
    i = pl.program_id(2)
    tm = u_ref.shape[0]

    @pl.when(i == 0)
    def _():
        cr_scr[...] = x0r_ref[...]
        ci_scr[...] = x0i_ref[...]

    u = u_ref[...]
    xr = _dot(u, bdr_ref[0])
    xi = _dot(u, bdi_ref[0])
    pwr = pwr_ref[...]
    pwi = pwi_ref[...]
    sl = xr.shape[1]
    xr = xr.reshape(tm // SUBLANES, SUBLANES, sl)
    xi = xi.reshape(tm // SUBLANES, SUBLANES, sl)
    row = lax.broadcasted_iota(jnp.int32, (SUBLANES, sl), 0)
    for k in (1, 2, 4):
        mr = jnp.where(row >= k, pwr[k - 1:k], 0.0)[None]
        mi = jnp.where(row >= k, pwi[k - 1:k], 0.0)[None]
        tr, ti = _cmul(mr, mi, pltpu.roll(xr, k, 1), pltpu.roll(xi, k, 1))
        xr = xr + tr
        xi = xi + ti
    xr_scr[...] = xr.reshape(tm, sl)
    xi_scr[...] = xi.reshape(tm, sl)

    def body(r, carry):
        cr, ci = carry
        rows = pl.ds(pl.multiple_of(r * SUBLANES, SUBLANES), SUBLANES)
        tr, ti = _cmul(pwr, pwi, cr, ci)
        nr = xr_scr[rows, :] + tr
        ni = xi_scr[rows, :] + ti
        xr_scr[rows, :] = nr
        xi_scr[rows, :] = ni
        return nr[SUBLANES - 1:SUBLANES], ni[SUBLANES - 1:SUBLANES]

    cr, ci = lax.fori_loop(0, tm // SUBLANES, body, (cr_scr[...], ci_scr[...]))
    cr_scr[...] = cr
    ci_scr[...] = ci
    xfr_ref[0] = cr
    xfi_ref[0] = ci

    y = _dot(xr_scr[...].astype(bf16), bor_ref[0]) - _dot(xi_scr[...].astype(bf16), boi_ref[0])
    y = y + d_ref[...] * u.astype(f32)
    y_ref[...] = _gelu(y).astype(y_ref.dtype)


def _ssm_scan(u_arr, u_col0, prep, d, x0r, x0i, batch, seq, tm):
    pwr, pwi, bdr, bdi, bor, boi = prep
    ngt, _, sl = bdr.shape
    nt = seq // tm
    bin_spec = lambda: pl.BlockSpec((1, LANES, sl), lambda g, b, i: (g, 0, 0))
    bout_spec = lambda: pl.BlockSpec((1, sl, LANES), lambda g, b, i: (g, 0, 0))
    pw_spec = lambda: pl.BlockSpec((SUBLANES, sl), lambda g, b, i: (0, g))
    x0_spec = lambda: pl.BlockSpec((1, sl), lambda g, b, i: (0, g))
    xf_spec = lambda: pl.BlockSpec((1, 1, sl), lambda g, b, i: (b, 0, g))
    return pl.pallas_call(
        _ssm_scan_kernel,
        grid=(ngt, batch, nt),
        in_specs=[pl.BlockSpec((tm, LANES), lambda g, b, i: (b * nt + i, u_col0 + g)),
                  bin_spec(), bin_spec(), bout_spec(), bout_spec(), pw_spec(), pw_spec(),
                  pl.BlockSpec((1, LANES), lambda g, b, i: (0, g)), x0_spec(), x0_spec()],
        out_specs=[pl.BlockSpec((tm, LANES), lambda g, b, i: (b * nt + i, g)), xf_spec(), xf_spec()],
        out_shape=[jax.ShapeDtypeStruct((batch * seq, ngt * LANES), bf16),
                   jax.ShapeDtypeStruct((batch, 1, ngt * sl), f32),
                   jax.ShapeDtypeStruct((batch, 1, ngt * sl), f32)],
        scratch_shapes=[pltpu.VMEM((tm, sl), f32), pltpu.VMEM((tm, sl), f32),
                        pltpu.VMEM((1, sl), f32), pltpu.VMEM((1, sl), f32)],
        compiler_params=_cparams(("parallel", "arbitrary", "arbitrary")),
        name="ssm_scan",
    )(u_arr, bdr, bdi, bor, boi, pwr, pwi, d, x0r, x0i)


def _attn_out_kernel(a_ref, w_ref, g_ref, o_ref):
    y = _dot(a_ref[...], w_ref[...])
    o_ref[...] = (_sigmoid(g_ref[...].astype(f32)) * y).astype(o_ref.dtype)


def _glu_mix_kernel(s_ref, wa_ref, wb_ref, g_ref, ya_ref, o_ref):
    s = s_ref[...]
    ga = _dot(s, wa_ref[...])
    gb = _dot(s, wb_ref[...])
    yb = ga * _sigmoid(gb)
    o_ref[...] = (ya_ref[...].astype(f32) + _sigmoid(g_ref[...].astype(f32)) * yb).astype(o_ref.dtype)


def _out_proj_kernel(m_ref, w_ref, x_ref, o_ref):
    o_ref[...] = x_ref[...] + _dot(m_ref[...], w_ref[...])


def _peer_query_kernel(h_ref, g_ref, wq_ref, kk_ref, hn_ref, st_ref):
    x = h_ref[...]
    ms = jnp.mean(x * x, axis=-1, keepdims=True)
    hn_f = x * lax.rsqrt(ms + EPS) * g_ref[...]
    hn_ref[...] = hn_f.T.astype(bf16)
    q = _dot(hn_f.astype(bf16), wq_ref[...]).astype(bf16)
    n_heads = kk_ref.shape[0]
    half = kk_ref.shape[3]
    for h in range(n_heads):
        for side in range(2):
            c0 = (2 * h + side) * half
            st_ref[h, side] = _dot_nt(kk_ref[h, side], q[:, c0:c0 + half])


def _peer_query(h1, g, wq, kk, tm):
    t, d = h1.shape
    nh, _, nk, half = kk.shape
    return pl.pallas_call(
        _peer_query_kernel,
        grid=(t // tm,),
        in_specs=[pl.BlockSpec((tm, d), lambda i: (i, 0)),
                  pl.BlockSpec((1, d), lambda i: (0, 0)),
                  pl.BlockSpec(wq.shape, lambda i: (0, 0)),
                  pl.BlockSpec(kk.shape, lambda i: (0, 0, 0, 0))],
        out_specs=[pl.BlockSpec((d, tm), lambda i: (0, i)),
                   pl.BlockSpec((nh, 2, nk, tm), lambda i: (0, 0, 0, i))],
        out_shape=[jax.ShapeDtypeStruct((d, t), bf16), jax.ShapeDtypeStruct((nh, 2, nk, t), f32)],
        compiler_params=_cparams(("parallel",)),
        name="peer_query",
    )(h1, g, wq, kk)


def _top_desc(s, n, want_rank):
    vals = []
    cur = s
    rank = jnp.full(s.shape, float(n), f32) if want_rank else None
    for a in range(n):
        v = jnp.max(cur, axis=0, keepdims=True)
        vals.append(v)
        hit = cur == v
        if want_rank:
            rank = jnp.where(hit, float(a), rank)
        cur = jnp.where(hit, -jnp.inf, cur)
    return _stack_rows(vals), rank


def _peer_topk_kernel(st_ref, r2_ref, e2_ref, n1_ref, e1_ref):
    chunk = LANES
    for c in range(st_ref.shape[3] // chunk):
        cols = slice(c * chunk, (c + 1) * chunk)
        s1 = st_ref[0, 0, :, cols]
        s2 = st_ref[0, 1, :, cols]
        v1, _ = _top_desc(s1, TOPK, False)
        v2, rank2 = _top_desc(s2, TOPK, True)
        n_b = lambda a: -(-(TOPK // (a + 1)) // SUBLANES) * SUBLANES
        cand = jnp.concatenate([v1[a:a + 1] + v2[:n_b(a)] for a in range(TOPK)], axis=0)
        best, _ = _top_desc(cand, TOPK, False)
        tau = best[TOPK - 1:TOPK]
        z = jnp.sum(jnp.exp(best - best[0:1]), axis=0, keepdims=True)
        n1 = jnp.zeros(s1.shape, f32)
        for a in range(TOPK):
            cnt = jnp.sum(jnp.where((v1[a:a + 1] + v2) >= tau, 1.0, 0.0), axis=0, keepdims=True)
            n1 = jnp.where(s1 == v1[a:a + 1], cnt, n1)
        r2_ref[0, :, cols] = rank2.astype(bf16)
        e2_ref[0, :, cols] = jnp.exp(s2 - v2[0:1]).astype(bf16)
        n1_ref[0, :, cols] = n1
        e1_ref[0, :, cols] = jnp.exp(s1 - v1[0:1]) / z


def _peer_topk(st, tl):
    nh, _, nk, t = st.shape
    spec = lambda: pl.BlockSpec((1, nk, tl), lambda h, i: (h, 0, i))
    shp = lambda dt: jax.ShapeDtypeStruct((nh, nk, t), dt)
    return pl.pallas_call(
        _peer_topk_kernel,
        grid=(nh, t // tl),
        in_specs=[pl.BlockSpec((1, 2, nk, tl), lambda h, i: (h, 0, 0, i))],
        out_specs=[spec(), spec(), spec(), spec()],
        out_shape=[shp(bf16), shp(bf16), shp(f32), shp(f32)],
        compiler_params=_cparams(("parallel", "parallel")),
        name="peer_topk",
    )(st)


def _peer_dense_kernel(hn_ref, u_ref, vt_ref, r2_ref, e2_ref, n1_ref, e1_ref, h1_ref, o_ref,
                       acc_scr, a0_scr, a1_scr, *, nb, tn):
    e = pl.program_id(1)
    n_heads, nk, tm = r2_ref.shape
    eb = u_ref.shape[0]
    pk = 2 * SUBLANES

    @pl.when(e == 0)
    def _():
        acc_scr[...] = jnp.zeros_like(acc_scr)
        a1_scr[...] = jnp.zeros_like(a1_scr)

    def step(a_cur, a_prev):
        a_cur[...] = _dot(u_ref[...], hn_ref[...])
        blk = jnp.maximum(e - 1, 0)
        for c in range(tm // tn):
            cols = slice(c * tn, (c + 1) * tn)
            w_rows = []
            for ii in range(eb // nk):
                i1 = blk * (eb // nk) + ii
                gate = None
                for h in range(n_heads):
                    n1 = jnp.broadcast_to(n1_ref[h, pl.ds(i1, 1), cols], (pk, tn)).astype(bf16)
                    e1 = jnp.broadcast_to(e1_ref[h, pl.ds(i1, 1), cols], (pk, tn)).astype(bf16)
                    r2 = r2_ref[h, :, cols].reshape(nk // pk, pk, tn)
                    e2 = e2_ref[h, :, cols].reshape(nk // pk, pk, tn)
                    term = jnp.where(r2 < n1[None], e2, jnp.zeros_like(e2)) * e1[None]
                    gate = term if gate is None else gate + term
                g = _gelu(a_prev[ii * nk:(ii + 1) * nk, cols]).astype(bf16)
                w_rows.append(gate.reshape(nk, tn) * g)
            acc_scr[:, cols] += _dot(vt_ref[...], jnp.concatenate(w_rows, axis=0))

    @pl.when(e % 2 == 0)
    def _():
        step(a0_scr, a1_scr)

    @pl.when(e % 2 == 1)
    def _():
        step(a1_scr, a0_scr)

    @pl.when(e == nb)
    def _():
        o_ref[...] = h1_ref[...] + acc_scr[...].T


def _peer_dense(hn_t, u_b, vt_b, r2, e2, n1, e1, h1, tm, eb):
    d, t = hn_t.shape
    nb = u_b.shape[0] // eb
    nh, nk, _ = r2.shape
    tok = lambda: pl.BlockSpec((nh, nk, tm), lambda i, e: (0, 0, i))
    return pl.pallas_call(
        functools.partial(_peer_dense_kernel, nb=nb, tn=_tile(tm, 2 * LANES)),
        grid=(t // tm, nb + 1),
        in_specs=[pl.BlockSpec((d, tm), lambda i, e: (0, i)),
                  pl.BlockSpec((eb, d), lambda i, e: (jnp.minimum(e, nb - 1), 0)),
                  pl.BlockSpec((d, eb), lambda i, e: (0, jnp.maximum(e - 1, 0))),
                  tok(), tok(), tok(), tok(),
                  pl.BlockSpec((tm, d), lambda i, e: (i, 0), pipeline_mode=pl.Buffered(1))],
        out_specs=pl.BlockSpec((tm, d), lambda i, e: (i, 0)),
        out_shape=jax.ShapeDtypeStruct((t, d), f32),
        scratch_shapes=[pltpu.VMEM((d, tm), f32), pltpu.VMEM((eb, tm), f32), pltpu.VMEM((eb, tm), f32)],
        compiler_params=_cparams(("parallel", "arbitrary")),
        name="peer_dense",
    )(hn_t, u_b, vt_b, r2, e2, n1, e1, h1)


def _tile(n, pref):
    if n <= pref:
        return n
    t = pref // LANES * LANES
    while n % t:
        t -= LANES
    assert t > 0, (n, pref)
    return t


def kernel(x, meta_tokens, norm1_g, w_in, q_norm_g, k_norm_g, lambda_q1, lambda_k1, lambda_q2, lambda_k2, subln_g, w_attn_branch, ssm_a_re, ssm_a_im, ssm_log_dt, ssm_b_re, ssm_b_im, ssm_c_re, ssm_c_im, ssm_d, w_glu, w_out, norm2_g, peer_w_q, peer_k1, peer_k2, peer_u, peer_v):
    assert norm1_g.shape[0] == 1, "single-layer block only"
    batch, seq, d = x.shape
    n_meta = meta_tokens.shape[0]
    sub = q_norm_g.shape[-1]
    aw = w_attn_branch.shape[1]
    heads = aw // (2 * sub)
    n_groups, n_state, n_chan = ssm_b_re.shape[1:]
    sw = n_groups * n_chan
    assert 2 * sub == LANES and LANES % n_chan == 0 and sw % LANES == 0
    assert n_meta % SUBLANES == 0 and n_meta <= LANES
    t = batch * seq
    n_in = w_in.shape[2]
    assert n_in == 3 * aw + sw + 2 * d

    x2 = x.reshape(t, d)
    w_in_b = w_in[0].astype(bf16)

    proj = _rms_matmul(x2, norm1_g, w_in_b, _tile(t, 1024), _tile(n_in, 1024), bf16, "in_proj")
    proj_m = _rms_matmul(meta_tokens, norm1_g, w_in_b, n_meta, _tile(n_in, 1024), bf16, "in_proj_meta")

    half = sub // 2
    inv_freq = jnp.power(ROPE_THETA, -jnp.arange(half, dtype=f32) / half)
    ang = jnp.arange(n_meta + seq, dtype=f32)[:, None] * inv_freq[None, :]
    cos_t = jnp.tile(jnp.cos(ang), (1, LANES // half))
    sin_t = jnp.tile(jnp.concatenate([-jnp.sin(ang), jnp.sin(ang)], axis=1), (1, LANES // sub))
    lane = jnp.arange(LANES)
    jmat = (lane[:, None] // sub == lane[None, :] // sub).astype(bf16)
    gq = jnp.tile(q_norm_g, (1, aw // sub))
    gk = jnp.tile(k_norm_g, (1, aw // sub))
    tr = _tile(seq, 512)
    q_rot = _normrot(proj, 0, aw, gq, cos_t[n_meta:], sin_t[n_meta:], jmat, tr, sub ** -0.5 * math.log2(math.e), sub,
                      "q_normrot")
    k_rot = _normrot(proj, 1, aw, gk, cos_t[n_meta:], sin_t[n_meta:], jmat, tr, 1.0, sub, "k_normrot")
    km_rot = _normrot(proj_m, 1, aw, gk, cos_t[:n_meta], sin_t[:n_meta], jmat, n_meta, 1.0, sub, "k_normrot_meta")
    km = jnp.pad(km_rot, ((0, LANES - n_meta), (0, 0)))
    vm = jnp.pad(proj_m[:, 2 * aw:3 * aw], ((0, LANES - n_meta), (0, 0)))

    att = _attention(q_rot, k_rot, proj, 2 * aw // LANES, km, vm, lambda_q1, lambda_k1, lambda_q2, lambda_k2,
                     subln_g, batch, seq, heads, _tile(seq, 512), n_meta, sub)

    gpt = LANES // n_chan
    ngt = sw // LANES
    sl = gpt * n_state
    gp = n_groups * n_state
    a_re = ssm_a_re.reshape(1, gp)
    a_im = ssm_a_im.reshape(1, gp)
    ldt = jnp.repeat(ssm_log_dt[0], n_state).reshape(1, gp)
    lay_b = lambda b: jnp.tile(b[0].transpose(0, 2, 1).reshape(ngt, LANES, n_state), (1, 1, gpt))
    lay_c = lambda c: jnp.tile(c[0].transpose(0, 2, 1).reshape(ngt, sl, n_chan), (1, 1, gpt))
    prep = _ssm_prep(a_re, a_im, ldt, lay_b(ssm_b_re), lay_b(ssm_b_im), lay_c(ssm_c_re), lay_c(ssm_c_im),
                     n_state, n_chan)
    zeros = jnp.zeros((1, gp), f32)
    u_col0 = 3 * aw // LANES
    _, xmr, xmi = _ssm_scan(proj_m, u_col0, prep, ssm_d, zeros, zeros, 1, n_meta, n_meta)
    ys, _, _ = _ssm_scan(proj, u_col0, prep, ssm_d, xmr[0], xmi[0], batch, seq, _tile(seq, 512))

    tm = _tile(t, 1024)
    tn = _tile(d, 1024)
    nj = d // tn
    ga_col0 = (3 * aw + sw) // tn
    gb_col0 = (3 * aw + sw + d) // tn
    grid = (t // tm, nj)
    sem = _cparams(("parallel", "arbitrary"))
    row_full = lambda k: pl.BlockSpec((tm, k), lambda i, j: (i, 0))
    w_col = lambda k, off=0: pl.BlockSpec((k, tn), lambda i, j: (0, j + off))
    out_tile = pl.BlockSpec((tm, tn), lambda i, j: (i, j))
    yag = pl.pallas_call(
        _attn_out_kernel, grid=grid,
        in_specs=[row_full(aw), w_col(aw), pl.BlockSpec((tm, tn), lambda i, j: (i, ga_col0 + j))],
        out_specs=out_tile, out_shape=jax.ShapeDtypeStruct((t, d), bf16), compiler_params=sem, name="attn_out",
    )(att, w_attn_branch[0].astype(bf16), proj)
    w_glu_b = w_glu[0].astype(bf16)
    mix = pl.pallas_call(
        _glu_mix_kernel, grid=grid,
        in_specs=[row_full(sw), w_col(sw), w_col(sw, nj), pl.BlockSpec((tm, tn), lambda i, j: (i, gb_col0 + j)),
                  out_tile],
        out_specs=out_tile, out_shape=jax.ShapeDtypeStruct((t, d), bf16), compiler_params=sem, name="glu_mix",
    )(ys, w_glu_b, w_glu_b, proj, yag)
    h1 = pl.pallas_call(
        _out_proj_kernel, grid=grid,
        in_specs=[row_full(d), w_col(d), out_tile],
        out_specs=out_tile, out_shape=jax.ShapeDtypeStruct((t, d), f32), compiler_params=sem, name="out_proj",
    )(mix, w_out[0].astype(bf16), x2)

    kk = jnp.stack([peer_k1[0], peer_k2[0]], axis=1).astype(bf16)
    assert kk.shape[2] == LANES and kk.shape[3] == LANES
    hn2, st = _peer_query(h1, norm2_g, peer_w_q[0].astype(bf16), kk, _tile(t, 512))
    r2, e2, n1, e1 = _peer_topk(st, _tile(t, 512))
    u_b = peer_u[0].astype(bf16)
    vt_b = peer_v[0].T.astype(bf16)
    out = _peer_dense(hn2, u_b, vt_b, r2, e2, n1, e1, h1, _tile(t, 512), _tile(u_b.shape[0], 512))
    return out.reshape(batch, seq, d)
```

```python
import functools
import math

import jax
import jax.numpy as jnp
from jax import lax
from jax.experimental import pallas as pl
from jax.experimental.pallas import tpu as pltpu

EPS = 1e-6
ROPE_THETA = 10000.0
LAM_INIT = 0.8 - 0.6 * math.exp(-0.3 * 0)
TOPK = 16
LANES = 128
SUBLANES = 8
NEG_BIG = -1e30
VMEM_LIMIT = 56 * 1024 * 1024

bf16 = jnp.bfloat16
f32 = jnp.float32


def _cparams(sem):
    return pltpu.CompilerParams(dimension_semantics=sem, vmem_limit_bytes=VMEM_LIMIT)


def _dot(a, b):
    return jnp.dot(a, b, preferred_element_type=f32)


def _dot_nt(a, b):
    return lax.dot_general(a, b, (((1,), (1,)), ((), ())), preferred_element_type=f32)


def _sigmoid(x):
    return 1.0 / (1.0 + jnp.exp(-x))


def _gelu(x):
    return 0.5 * x * (1.0 + lax.erf(x * (1.0 / math.sqrt(2.0))))


def _stack_rows(rows):
    k = len(rows)
    idx = lax.broadcasted_iota(jnp.int32, (k, rows[0].shape[1]), 0)
    out = jnp.broadcast_to(rows[0], idx.shape)
    for j in range(1, k):
        out = jnp.where(idx == j, rows[j], out)
    return out


def _rms_matmul_kernel(x_ref, g_ref, w_ref, o_ref, xn_ref):
    @pl.when(pl.program_id(1) == 0)
    def _():
        x = x_ref[...]
        ms = jnp.mean(x * x, axis=-1, keepdims=True)
        xn_ref[...] = (x * lax.rsqrt(ms + EPS) * g_ref[...]).astype(bf16)

    o_ref[...] = _dot(xn_ref[...], w_ref[...]).astype(o_ref.dtype)


def _rms_matmul(x, g, w, tm, tn, out_dtype, name):
    m, k = x.shape
    n = w.shape[1]
    return pl.pallas_call(
        _rms_matmul_kernel,
        grid=(m // tm, n // tn),
        in_specs=[pl.BlockSpec((tm, k), lambda i, j: (i, 0)),
                  pl.BlockSpec((1, k), lambda i, j: (0, 0)),
                  pl.BlockSpec((k, tn), lambda i, j: (0, j))],
        out_specs=pl.BlockSpec((tm, tn), lambda i, j: (i, j)),
        out_shape=jax.ShapeDtypeStruct((m, n), out_dtype),
        scratch_shapes=[pltpu.VMEM((tm, k), bf16)],
        compiler_params=_cparams(("parallel", "arbitrary")),
        name=name,
    )(x, g, w)


def _normrot_kernel(x_ref, g_ref, cos_ref, sin_ref, j_ref, o_ref, *, scale, sub):
    width = x_ref.shape[1]
    cos = cos_ref[...]
    sin = sin_ref[...]
    jmat = j_ref[...]
    lane = lax.broadcasted_iota(jnp.int32, cos.shape, 1)
    first_half = (lane % sub) < (sub // 2)
    for t in range(width // LANES):
        sl = slice(t * LANES, (t + 1) * LANES)
        x = x_ref[:, sl].astype(f32)
        x2 = x * x
        hi = x2.astype(bf16)
        lo = (x2 - hi.astype(f32)).astype(bf16)
        ssq = _dot(hi, jmat) + _dot(lo, jmat)
        xn = x * lax.rsqrt(ssq * (1.0 / sub) + EPS) * g_ref[:, sl]
        partner = jnp.where(first_half, pltpu.roll(xn, LANES - sub // 2, 1), pltpu.roll(xn, sub // 2, 1))
        o_ref[:, sl] = ((xn * cos + partner * sin) * scale).astype(o_ref.dtype)


def _normrot(x, col_block, width, g_t, cos, sin, jmat, tm, scale, sub, name):
    m = x.shape[0]
    nt = cos.shape[0] // tm
    return pl.pallas_call(
        functools.partial(_normrot_kernel, scale=scale, sub=sub),
        grid=(m // tm,),
        in_specs=[pl.BlockSpec((tm, width), lambda i: (i, col_block)),
                  pl.BlockSpec((1, width), lambda i: (0, 0)),
                  pl.BlockSpec((tm, LANES), lambda i: (i % nt, 0)),
                  pl.BlockSpec((tm, LANES), lambda i: (i % nt, 0)),
                  pl.BlockSpec((LANES, LANES), lambda i: (0, 0))],
        out_specs=pl.BlockSpec((tm, width), lambda i: (i, 0)),
        out_shape=jax.ShapeDtypeStruct((m, width), bf16),
        compiler_params=_cparams(("parallel",)),
        name=name,
    )(x, g_t, cos, sin, jmat)


def _attn_kernel(q_ref, k_ref, v_ref, km_ref, vm_ref, lq1_ref, lk1_ref, lq2_ref, lk2_ref, sg_ref,
                 o_ref, qq_scr, m_scr, acc_scr, s_scr, *, n_meta, sub, hpb):
    i = pl.program_id(2)
    tq = q_ref.shape[0]
    tk = tq
    hw = 2 * sub
    mp = km_ref.shape[0]
    lane = lax.broadcasted_iota(jnp.int32, (tq, hw), 1)
    head = lambda hh: slice(hh * hw, (hh + 1) * hw)

    def softmax_step(hh, s, v_aug, first):
        smax = jnp.max(s, axis=1, keepdims=True)
        if first:
            m_new = jnp.broadcast_to(smax, (2 * tq, hw))
        else:
            m_prev = m_scr[hh]
            m_new = jnp.maximum(m_prev, smax)
        p = jnp.concatenate([jnp.exp2(s[:, c * hw:(c + 1) * hw] - m_new) for c in range(s.shape[1] // hw)], axis=1)
        pv = _dot(p.astype(bf16), v_aug)
        if first:
            acc_scr[hh] = pv
        else:
            alpha = jnp.exp2(m_prev - m_new)
            acc_scr[hh] = jnp.concatenate([alpha, alpha], axis=1) * acc_scr[hh] + pv
        m_scr[hh] = m_new

    for hh in range(hpb):
        q = q_ref[:, head(hh)]
        zero = jnp.zeros_like(q)
        qq_scr[hh] = jnp.concatenate([jnp.where(lane < sub, q, zero), jnp.where(lane >= sub, q, zero)], axis=0)
        s = _dot_nt(qq_scr[hh], km_ref[:, head(hh)])
        col = lax.broadcasted_iota(jnp.int32, s.shape, 1)
        s = jnp.where(col < n_meta, s, NEG_BIG)
        softmax_step(hh, s, jnp.concatenate([vm_ref[:, head(hh)], jnp.ones((mp, hw), bf16)], axis=1), True)

    def put_scores(j, slot):
        start = pl.multiple_of(j * tk, tk)
        for hh in range(hpb):
            s_scr[slot, hh] = _dot_nt(qq_scr[hh], k_ref[pl.ds(start, tk), head(hh)])

    def consume(j, slot, masked):
        start = pl.multiple_of(j * tk, tk)
        for hh in range(hpb):
            s = s_scr[slot, hh]
            if masked:
                row = lax.broadcasted_iota(jnp.int32, s.shape, 0) % tq
                col = lax.broadcasted_iota(jnp.int32, s.shape, 1)
                s = jnp.where(col <= row, s, NEG_BIG)
            vb = v_ref[pl.ds(start, tk), head(hh)]
            softmax_step(hh, s, jnp.concatenate([vb, jnp.ones((tk, hw), bf16)], axis=1), False)

    def stage(j, slot):
        put_scores(j + 1, 1 - slot)
        consume(j, slot, False)

    put_scores(0, 0)

    def body(p, c):
        stage(2 * p, 0)
        stage(2 * p + 1, 1)
        return c

    lax.fori_loop(0, i // 2, body, 0)

    @pl.when(i % 2 == 0)
    def _():
        consume(i, 0, True)

    @pl.when(i % 2 == 1)
    def _():
        stage(i - 1, 0)
        consume(i, 1, True)

    lam = (jnp.exp(jnp.sum(lq1_ref[...] * lk1_ref[...], axis=1, keepdims=True))
           - jnp.exp(jnp.sum(lq2_ref[...] * lk2_ref[...], axis=1, keepdims=True)) + LAM_INIT)
    for hh in range(hpb):
        acc = acc_scr[hh]
        o = acc[:, :hw] / acc[:, hw:]
        att = o[:tq] - lam * o[tq:]
        ms = jnp.mean(att * att, axis=-1, keepdims=True)
        att = att * lax.rsqrt(ms + EPS) * sg_ref[...] * (1.0 - LAM_INIT)
        o_ref[:, head(hh)] = att.astype(o_ref.dtype)


def _attention(q_rot, k_rot, proj, v_col0, km, vm, lq1, lk1, lq2, lk2, sg, batch, seq, heads, tq, n_meta, sub):
    t, aw = q_rot.shape
    nq = seq // tq
    hw = 2 * sub
    hpb = 2 if heads % 2 == 0 else 1
    bw = hpb * hw
    mp = km.shape[0]
    vec = lambda: pl.BlockSpec((1, sub), lambda b, h, i: (0, 0))
    return pl.pallas_call(
        functools.partial(_attn_kernel, n_meta=n_meta, sub=sub, hpb=hpb),
        grid=(batch, heads // hpb, nq),
        in_specs=[pl.BlockSpec((tq, bw), lambda b, h, i: (b * nq + i, h)),
                  pl.BlockSpec((seq, bw), lambda b, h, i: (b, h)),
                  pl.BlockSpec((seq, bw), lambda b, h, i: (b, v_col0 // hpb + h)),
                  pl.BlockSpec((mp, bw), lambda b, h, i: (0, h)),
                  pl.BlockSpec((mp, bw), lambda b, h, i: (0, h)),
                  vec(), vec(), vec(), vec(),
                  pl.BlockSpec((1, hw), lambda b, h, i: (0, 0))],
        out_specs=pl.BlockSpec((tq, bw), lambda b, h, i: (b * nq + i, h)),
        out_shape=jax.ShapeDtypeStruct((t, aw), bf16),
        scratch_shapes=[pltpu.VMEM((hpb, 2 * tq, hw), bf16), pltpu.VMEM((hpb, 2 * tq, hw), f32),
                        pltpu.VMEM((hpb, 2 * tq, 2 * hw), f32), pltpu.VMEM((2, hpb, 2 * tq, tq), f32)],
        compiler_params=_cparams(("parallel", "parallel", "arbitrary")),
        name="diff_attention",
    )(q_rot, k_rot, proj, km, vm, lq1, lk1, lq2, lk2, sg)


def _cmul(ar, ai, br, bi):
    return ar * br - ai * bi, ar * bi + ai * br


def _ssm_prep_kernel(are_ref, aim_ref, ldt_ref, bre_ref, bim_ref, cre_ref, cim_ref,
                     pwr_ref, pwi_ref, bdr_ref, bdi_ref, bor_ref, boi_ref, *, n_state, n_chan):
    a_re = are_ref[...]
    a_im = aim_ref[...]
    dt = jnp.exp(ldt_ref[...])
    er = jnp.exp(a_re * dt)
    ab_re = er * jnp.cos(a_im * dt)
    ab_im = er * jnp.sin(a_im * dt)
    nr = ab_re - 1.0
    ni = ab_im
    den = a_re * a_re + a_im * a_im
    f_re = (nr * a_re + ni * a_im) / den
    f_im = (ni * a_re - nr * a_im) / den
    p1 = (ab_re, ab_im)
    p2 = _cmul(*p1, *p1)
    p3 = _cmul(*p2, *p1)
    p4 = _cmul(*p2, *p2)
    p5 = _cmul(*p4, *p1)
    p6 = _cmul(*p4, *p2)
    p7 = _cmul(*p4, *p3)
    p8 = _cmul(*p4, *p4)
    pw = (p1, p2, p3, p4, p5, p6, p7, p8)
    pwr_ref[...] = _stack_rows([p[0] for p in pw])
    pwi_ref[...] = _stack_rows([p[1] for p in pw])
    b_re = bre_ref[0]
    b_im = bim_ref[0]
    r = lax.broadcasted_iota(jnp.int32, b_re.shape, 0) // n_chan
    c = lax.broadcasted_iota(jnp.int32, b_re.shape, 1) // n_state
    keep = r == c
    bdr_ref[0] = jnp.where(keep, f_re * b_re - f_im * b_im, 0.0).astype(bf16)
    bdi_ref[0] = jnp.where(keep, f_re * b_im + f_im * b_re, 0.0).astype(bf16)
    c_re = cre_ref[0]
    c_im = cim_ref[0]
    r = lax.broadcasted_iota(jnp.int32, c_re.shape, 0) // n_state
    c = lax.broadcasted_iota(jnp.int32, c_re.shape, 1) // n_chan
    keep = r == c
    bor_ref[0] = jnp.where(keep, c_re, 0.0).astype(bf16)
    boi_ref[0] = jnp.where(keep, c_im, 0.0).astype(bf16)


def _ssm_prep(a_re, a_im, ldt, b_re_t, b_im_t, c_re_t, c_im_t, n_state, n_chan):
    ngt, _, sl = b_re_t.shape
    gp = a_re.shape[1]
    lane_vec = lambda: pl.BlockSpec((1, sl), lambda g: (0, g))
    bin_spec = lambda: pl.BlockSpec((1, LANES, sl), lambda g: (g, 0, 0))
    bout_spec = lambda: pl.BlockSpec((1, sl, LANES), lambda g: (g, 0, 0))
    return pl.pallas_call(
        functools.partial(_ssm_prep_kernel, n_state=n_state, n_chan=n_chan),
        grid=(ngt,),
        in_specs=[lane_vec(), lane_vec(), lane_vec(), bin_spec(), bin_spec(), bout_spec(), bout_spec()],
        out_specs=[pl.BlockSpec((SUBLANES, sl), lambda g: (0, g)), pl.BlockSpec((SUBLANES, sl), lambda g: (0, g)),
                   bin_spec(), bin_spec(), bout_spec(), bout_spec()],
        out_shape=[jax.ShapeDtypeStruct((SUBLANES, gp), f32), jax.ShapeDtypeStruct((SUBLANES, gp), f32),
                   jax.ShapeDtypeStruct(b_re_t.shape, bf16), jax.ShapeDtypeStruct(b_re_t.shape, bf16),
                   jax.ShapeDtypeStruct(c_re_t.shape, bf16), jax.ShapeDtypeStruct(c_re_t.shape, bf16)],
        compiler_params=_cparams(("parallel",)),
        name="ssm_prep",
    )(a_re, a_im, ldt, b_re_t, b_im_t, c_re_t, c_im_t)


def _ssm_scan_kernel(u_ref, bdr_ref, bdi_ref, bor_ref, boi_ref, pwr_ref, pwi_ref, d_ref, x0r_ref, x0i_ref,
                     y_ref, xfr_ref, xfi_ref, xr_scr, xi_scr, cr_scr, ci_scr):
    i = pl.program_id(2)
    tm = u_ref.shape[0]

    @pl.when(i == 0)
    def _():
        cr_scr[...] = x0r_ref[...]
        ci_scr[...] = x0i_ref[...]

    u = u_ref[...]
    xr = _dot(u, bdr_ref[0])
    xi = _dot(u, bdi_ref[0])
    pwr = pwr_ref[...]
    pwi = pwi_ref[...]
    sl = xr.shape[1]
    xr = xr.reshape(tm // SUBLANES, SUBLANES, sl)
    xi = xi.reshape(tm // SUBLANES, SUBLANES, sl)
    row = lax.broadcasted_iota(jnp.int32, (SUBLANES, sl), 0)
    for k in (1, 2, 4):
        mr = jnp.where(row >= k, pwr[k - 1:k], 0.0)[None]
        mi = jnp.where(row >= k, pwi[k - 1:k], 0.0)[None]
        tr, ti = _cmul(mr, mi, pltpu.roll(xr, k, 1), pltpu.roll(xi, k, 1))
        xr = xr + tr
        xi = xi + ti
    xr_scr[...] = xr.reshape(tm, sl)
    xi_scr[...] = xi.reshape(tm, sl)

    def body(r, carry):
        cr, ci = carry
        rows = pl.ds(pl.multiple_of(r * SUBLANES, SUBLANES), SUBLANES)
        tr, ti = _cmul(pwr, pwi, cr, ci)
        nr = xr_scr[rows, :] + tr
        ni = xi_scr[rows, :] + ti
        xr_scr[rows, :] = nr
        xi_scr[rows, :] = ni
        return nr[SUBLANES - 1:SUBLANES], ni[SUBLANES - 1:SUBLANES]

    cr, ci = lax.fori_loop(0, tm // SUBLANES, body, (cr_scr[...], ci_scr[...]))
    cr_scr[...] = cr
    ci_scr[...] = ci
    xfr_ref[0] = cr
    xfi_ref[0] = ci

    y = _dot(xr_scr[...].astype(bf16), bor_ref[0]) - _dot(xi_scr[...].astype(bf16), boi_ref[0])
    y = y + d_ref[...] * u.astype(f32)
    y_ref[...] = _gelu(y).astype(y_ref.dtype)


def _ssm_scan(u_arr, u_col0, prep, d, x0r, x0i, batch, seq, tm):
    pwr, pwi, bdr, bdi, bor, boi = prep
    ngt, _, sl = bdr.shape
    nt = seq // tm
    bin_spec = lambda: pl.BlockSpec((1, LANES, sl), lambda g, b, i: (g, 0, 0))
    bout_spec = lambda: pl.BlockSpec((1, sl, LANES), lambda g, b, i: (g, 0, 0))
    pw_spec = lambda: pl.BlockSpec((SUBLANES, sl), lambda g, b, i: (0, g))
    x0_spec = lambda: pl.BlockSpec((1, sl), lambda g, b, i: (0, g))
    xf_spec = lambda: pl.BlockSpec((1, 1, sl), lambda g, b, i: (b, 0, g))
    return pl.pallas_call(
        _ssm_scan_kernel,
        grid=(ngt, batch, nt),
        in_specs=[pl.BlockSpec((tm, LANES), lambda g, b, i: (b * nt + i, u_col0 + g)),
                  bin_spec(), bin_spec(), bout_spec(), bout_spec(), pw_spec(), pw_spec(),
                  pl.BlockSpec((1, LANES), lambda g, b, i: (0, g)), x0_spec(), x0_spec()],
        out_specs=[pl.BlockSpec((tm, LANES), lambda g, b, i: (b * nt + i, g)), xf_spec(), xf_spec()],
        out_shape=[jax.ShapeDtypeStruct((batch * seq, ngt * LANES), bf16),
                   jax.ShapeDtypeStruct((batch, 1, ngt * sl), f32),
                   jax.ShapeDtypeStruct((batch, 1, ngt * sl), f32)],
        scratch_shapes=[pltpu.VMEM((tm, sl), f32), pltpu.VMEM((tm, sl), f32),
                        pltpu.VMEM((1, sl), f32), pltpu.VMEM((1, sl), f32)],
        compiler_params=_cparams(("parallel", "arbitrary", "arbitrary")),
        name="ssm_scan",
    )(u_arr, bdr, bdi, bor, boi, pwr, pwi, d, x0r, x0i)


def _attn_out_kernel(a_ref, w_ref, g_ref, o_ref):
    y = _dot(a_ref[...], w_ref[...])
    o_ref[...] = (_sigmoid(g_ref[...].astype(f32)) * y).astype(o_ref.dtype)


def _glu_mix_kernel(s_ref, wa_ref, wb_ref, g_ref, ya_ref, o_ref):
    s = s_ref[...]
    ga = _dot(s, wa_ref[...])
    gb = _dot(s, wb_ref[...])
    yb = ga * _sigmoid(gb)
    o_ref[...] = (ya_ref[...].astype(f32) + _sigmoid(g_ref[...].astype(f32)) * yb).astype(o_ref.dtype)


def _out_proj_kernel(m_ref, w_ref, x_ref, o_ref):
    o_ref[...] = x_ref[...] + _dot(m_ref[...], w_ref[...])


def _peer_query_kernel(h_ref, g_ref, wq_ref, kk_ref, hn_ref, st_ref):
    x = h_ref[...]
    ms = jnp.mean(x * x, axis=-1, keepdims=True)
    hn_f = x * lax.rsqrt(ms + EPS) * g_ref[...]
    hn_ref[...] = hn_f.T.astype(bf16)
    q = _dot(hn_f.astype(bf16), wq_ref[...]).astype(bf16)
    n_heads = kk_ref.shape[0]
    half = kk_ref.shape[3]
    for h in range(n_heads):
        for side in range(2):
            c0 = (2 * h + side) * half
            st_ref[h, side] = _dot_nt(kk_ref[h, side], q[:, c0:c0 + half])


def _peer_query(h1, g, wq, kk, tm):
    t, d = h1.shape
    nh, _, nk, half = kk.shape
    return pl.pallas_call(
        _peer_query_kernel,
        grid=(t // tm,),
        in_specs=[pl.BlockSpec((tm, d), lambda i: (i, 0)),
                  pl.BlockSpec((1, d), lambda i: (0, 0)),
                  pl.BlockSpec(wq.shape, lambda i: (0, 0)),
                  pl.BlockSpec(kk.shape, lambda i: (0, 0, 0, 0))],
        out_specs=[pl.BlockSpec((d, tm), lambda i: (0, i)),
                   pl.BlockSpec((nh, 2, nk, tm), lambda i: (0, 0, 0, i))],
        out_shape=[jax.ShapeDtypeStruct((d, t), bf16), jax.ShapeDtypeStruct((nh, 2, nk, t), f32)],
        compiler_params=_cparams(("parallel",)),
        name="peer_query",
    )(h1, g, wq, kk)


def _top_desc(s, n, want_rank):
    vals = []
    cur = s
    rank = jnp.full(s.shape, float(n), f32) if want_rank else None
    for a in range(n):
        v = jnp.max(cur, axis=0, keepdims=True)
        vals.append(v)
        hit = cur == v
        if want_rank:
            rank = jnp.where(hit, float(a), rank)
        cur = jnp.where(hit, -jnp.inf, cur)
    return _stack_rows(vals), rank


def _peer_topk_kernel(st_ref, r2_ref, e2_ref, n1_ref, e1_ref):
    chunk = LANES
    for c in range(st_ref.shape[3] // chunk):
        cols = slice(c * chunk, (c + 1) * chunk)
        s1 = st_ref[0, 0, :, cols]
        s2 = st_ref[0, 1, :, cols]
        v1, _ = _top_desc(s1, TOPK, False)
        v2, rank2 = _top_desc(s2, TOPK, True)
        n_b = lambda a: -(-(TOPK // (a + 1)) // SUBLANES) * SUBLANES
        cand = jnp.concatenate([v1[a:a + 1] + v2[:n_b(a)] for a in range(TOPK)], axis=0)
        best, _ = _top_desc(cand, TOPK, False)
        tau = best[TOPK - 1:TOPK]
        z = jnp.sum(jnp.exp(best - best[0:1]), axis=0, keepdims=True)
        n1 = jnp.zeros(s1.shape, f32)
        for a in range(TOPK):
            cnt = jnp.sum(jnp.where((v1[a:a + 1] + v2) >= tau, 1.0, 0.0), axis=0, keepdims=True)
            n1 = jnp.where(s1 == v1[a:a + 1], cnt, n1)
        r2_ref[0, :, cols] = rank2.astype(bf16)
        e2_ref[0, :, cols] = jnp.exp(s2 - v2[0:1]).astype(bf16)
        n1_ref[0, :, cols] = n1
        e1_ref[0, :, cols] = jnp.exp(s1 - v1[0:1]) / z


def _peer_topk(st, tl):
    nh, _, nk, t = st.shape
    spec = lambda: pl.BlockSpec((1, nk, tl), lambda h, i: (h, 0, i))
    shp = lambda dt: jax.ShapeDtypeStruct((nh, nk, t), dt)
    return pl.pallas_call(
        _peer_topk_kernel,
        grid=(nh, t // tl),
        in_specs=[pl.BlockSpec((1, 2, nk, tl), lambda h, i: (h, 0, 0, i))],
        out_specs=[spec(), spec(), spec(), spec()],
        out_shape=[shp(bf16), shp(bf16), shp(f32), shp(f32)],
        compiler_params=_cparams(("parallel", "parallel")),
        name="peer_topk",
    )(st)


def _peer_dense_kernel(hn_ref, u_ref, vt_ref, r2_ref, e2_ref, n1_ref, e1_ref, h1_ref, o_ref,
                       acc_scr, a0_scr, a1_scr, *, nb, tn):
    e = pl.program_id(1)
    n_heads, nk, tm = r2_ref.shape
    eb = u_ref.shape[0]
    pk = 2 * SUBLANES

    @pl.when(e == 0)
    def _():
        acc_scr[...] = jnp.zeros_like(acc_scr)
        a1_scr[...] = jnp.zeros_like(a1_scr)

    def step(a_cur, a_prev):
        a_cur[...] = _dot(u_ref[...], hn_ref[...])
        blk = jnp.maximum(e - 1, 0)
        for c in range(tm // tn):
            cols = slice(c * tn, (c + 1) * tn)
            w_rows = []
            for ii in range(eb // nk):
                i1 = blk * (eb // nk) + ii
                gate = None
                for h in range(n_heads):
                    n1 = jnp.broadcast_to(n1_ref[h, pl.ds(i1, 1), cols], (pk, tn)).astype(bf16)
                    e1 = jnp.broadcast_to(e1_ref[h, pl.ds(i1, 1), cols], (pk, tn)).astype(bf16)
                    r2 = r2_ref[h, :, cols].reshape(nk // pk, pk, tn)
                    e2 = e2_ref[h, :, cols].reshape(nk // pk, pk, tn)
                    term = jnp.where(r2 < n1[None], e2, jnp.zeros_like(e2)) * e1[None]
                    gate = term if gate is None else gate + term
                g = _gelu(a_prev[ii * nk:(ii + 1) * nk, cols]).astype(bf16)
                w_rows.append(gate.reshape(nk, tn) * g)
            acc_scr[:, cols] += _dot(vt_ref[...], jnp.concatenate(w_rows, axis=0))

    @pl.when(e % 2 == 0)
    def _():
        step(a0_scr, a1_scr)

    @pl.when(e % 2 == 1)
    def _():
        step(a1_scr, a0_scr)

    @pl.when(e == nb)
    def _():
        o_ref[...] = h1_ref[...] + acc_scr[...].T


def _peer_dense(hn_t, u_b, vt_b, r2, e2, n1, e1, h1, tm, eb):
    d, t = hn_t.shape
    nb = u_b.shape[0] // eb
    nh, nk, _ = r2.shape
    tok = lambda: pl.BlockSpec((nh, nk, tm), lambda i, e: (0, 0, i), pipeline_mode=pl.Buffered(1))
    return pl.pallas_call(
        functools.partial(_peer_dense_kernel, nb=nb, tn=_tile(tm, 2 * LANES)),
        grid=(t // tm, nb + 1),
        in_specs=[pl.BlockSpec((d, tm), lambda i, e: (0, i)),
                  pl.BlockSpec((eb, d), lambda i, e: (jnp.minimum(e, nb - 1), 0)),
                  pl.BlockSpec((d, eb), lambda i, e: (0, jnp.maximum(e - 1, 0))),
                  tok(), tok(), tok(), tok(),
                  pl.BlockSpec((tm, d), lambda i, e: (i, 0), pipeline_mode=pl.Buffered(1))],
        out_specs=pl.BlockSpec((tm, d), lambda i, e: (i, 0)),
        out_shape=jax.ShapeDtypeStruct((t, d), f32),
        scratch_shapes=[pltpu.VMEM((d, tm), f32), pltpu.VMEM((eb, tm), f32), pltpu.VMEM((eb, tm), f32)],
        compiler_params=_cparams(("parallel", "arbitrary")),
        name="peer_dense",
    )(hn_t, u_b, vt_b, r2, e2, n1, e1, h1)


def _tile(n, pref):
    if n <= pref:
        return n
    t = pref // LANES * LANES
    while n % t:
        t -= LANES
    assert t > 0, (n, pref)
    return t


def kernel(x, meta_tokens, norm1_g, w_in, q_norm_g, k_norm_g, lambda_q1, lambda_k1, lambda_q2, lambda_k2, subln_g, w_attn_branch, ssm_a_re, ssm_a_im, ssm_log_dt, ssm_b_re, ssm_b_im, ssm_c_re, ssm_c_im, ssm_d, w_glu, w_out, norm2_g, peer_w_q, peer_k1, peer_k2, peer_u, peer_v):
    assert norm1_g.shape[0] == 1, "single-layer block only"
    batch, seq, d = x.shape
    n_meta = meta_tokens.shape[0]
    sub = q_norm_g.shape[-1]
    aw = w_attn_branch.shape[1]
    heads = aw // (2 * sub)
    n_groups, n_state, n_chan = ssm_b_re.shape[1:]
    sw = n_groups * n_chan
    assert 2 * sub == LANES and LANES % n_chan == 0 and sw % LANES == 0
    assert n_meta % SUBLANES == 0 and n_meta <= LANES
    t = batch * seq
    n_in = w_in.shape[2]
    assert n_in == 3 * aw + sw + 2 * d

    x2 = x.reshape(t, d)
    w_in_b = w_in[0].astype(bf16)

    proj = _rms_matmul(x2, norm1_g, w_in_b, _tile(t, 1024), _tile(n_in, 1024), bf16, "in_proj")
    proj_m = _rms_matmul(meta_tokens, norm1_g, w_in_b, n_meta, _tile(n_in, 1024), bf16, "in_proj_meta")

    half = sub // 2
    inv_freq = jnp.power(ROPE_THETA, -jnp.arange(half, dtype=f32) / half)
    ang = jnp.arange(n_meta + seq, dtype=f32)[:, None] * inv_freq[None, :]
    cos_t = jnp.tile(jnp.cos(ang), (1, LANES // half))
    sin_t = jnp.tile(jnp.concatenate([-jnp.sin(ang), jnp.sin(ang)], axis=1), (1, LANES // sub))
    lane = jnp.arange(LANES)
    jmat = (lane[:, None] // sub == lane[None, :] // sub).astype(bf16)
    gq = jnp.tile(q_norm_g, (1, aw // sub))
    gk = jnp.tile(k_norm_g, (1, aw // sub))
    tr = _tile(seq, 512)
    q_rot = _normrot(proj, 0, aw, gq, cos_t[n_meta:], sin_t[n_meta:], jmat, tr, sub ** -0.5 * math.log2(math.e), sub,
                      "q_normrot")
    k_rot = _normrot(proj, 1, aw, gk, cos_t[n_meta:], sin_t[n_meta:], jmat, tr, 1.0, sub, "k_normrot")
    km_rot = _normrot(proj_m, 1, aw, gk, cos_t[:n_meta], sin_t[:n_meta], jmat, n_meta, 1.0, sub, "k_normrot_meta")
    km = jnp.pad(km_rot, ((0, LANES - n_meta), (0, 0)))
    vm = jnp.pad(proj_m[:, 2 * aw:3 * aw], ((0, LANES - n_meta), (0, 0)))

    att = _attention(q_rot, k_rot, proj, 2 * aw // LANES, km, vm, lambda_q1, lambda_k1, lambda_q2, lambda_k2,
                     subln_g, batch, seq, heads, _tile(seq, 512), n_meta, sub)

    gpt = LANES // n_chan
    ngt = sw // LANES
    sl = gpt * n_state
    gp = n_groups * n_state
    a_re = ssm_a_re.reshape(1, gp)
    a_im = ssm_a_im.reshape(1, gp)
    ldt = jnp.repeat(ssm_log_dt[0], n_state).reshape(1, gp)
    lay_b = lambda b: jnp.tile(b[0].transpose(0, 2, 1).reshape(ngt, LANES, n_state), (1, 1, gpt))
    lay_c = lambda c: jnp.tile(c[0].transpose(0, 2, 1).reshape(ngt, sl, n_chan), (1, 1, gpt))
    prep = _ssm_prep(a_re, a_im, ldt, lay_b(ssm_b_re), lay_b(ssm_b_im), lay_c(ssm_c_re), lay_c(ssm_c_im),
                     n_state, n_chan)
    zeros = jnp.zeros((1, gp), f32)
    u_col0 = 3 * aw // LANES
    _, xmr, xmi = _ssm_scan(proj_m, u_col0, prep, ssm_d, zeros, zeros, 1, n_meta, n_meta)
    ys, _, _ = _ssm_scan(proj, u_col0, prep, ssm_d, xmr[0], xmi[0], batch, seq, _tile(seq, 512))

    tm = _tile(t, 1024)
    tn = _tile(d, 1024)
    nj = d // tn
    ga_col0 = (3 * aw + sw) // tn
    gb_col0 = (3 * aw + sw + d) // tn
    grid = (t // tm, nj)
    sem = _cparams(("parallel", "arbitrary"))
    row_full = lambda k: pl.BlockSpec((tm, k), lambda i, j: (i, 0))
    w_col = lambda k, off=0: pl.BlockSpec((k, tn), lambda i, j: (0, j + off))
    out_tile = pl.BlockSpec((tm, tn), lambda i, j: (i, j))
    yag = pl.pallas_call(
        _attn_out_kernel, grid=grid,
        in_specs=[row_full(aw), w_col(aw), pl.BlockSpec((tm, tn), lambda i, j: (i, ga_col0 + j))],
        out_specs=out_tile, out_shape=jax.ShapeDtypeStruct((t, d), bf16), compiler_params=sem, name="attn_out",
    )(att, w_attn_branch[0].astype(bf16), proj)
    w_glu_b = w_glu[0].astype(bf16)
    mix = pl.pallas_call(
        _glu_mix_kernel, grid=grid,
        in_specs=[row_full(sw), w_col(sw), w_col(sw, nj), pl.BlockSpec((tm, tn), lambda i, j: (i, gb_col0 + j)),
                  out_tile],
        out_specs=out_tile, out_shape=jax.ShapeDtypeStruct((t, d), bf16), compiler_params=sem, name="glu_mix",
    )(ys, w_glu_b, w_glu_b, proj, yag)
    h1 = pl.pallas_call(
        _out_proj_kernel, grid=grid,
        in_specs=[row_full(d), w_col(d), out_tile],
        out_specs=out_tile, out_shape=jax.ShapeDtypeStruct((t, d), f32), compiler_params=sem, name="out_proj",
    )(mix, w_out[0].astype(bf16), x2)

    kk = jnp.stack([peer_k1[0], peer_k2[0]], axis=1).astype(bf16)
    assert kk.shape[2] == LANES and kk.shape[3] == LANES
    hn2, st = _peer_query(h1, norm2_g, peer_w_q[0].astype(bf16), kk, _tile(t, 512))
    r2, e2, n1, e1 = _peer_topk(st, _tile(t, 512))
    u_b = peer_u[0].astype(bf16)
    vt_b = peer_v[0].T.astype(bf16)
    out = _peer_dense(hn2, u_b, vt_b, r2, e2, n1, e1, h1, _tile(t, 512), _tile(u_b.shape[0], 1024))
    return out.reshape(batch, seq, d)
```

```python
import functools
import math

import jax
import jax.numpy as jnp
from jax import lax
from jax.experimental import pallas as pl
from jax.experimental.pallas import tpu as pltpu

EPS = 1e-6
ROPE_THETA = 10000.0
LAM_INIT = 0.8 - 0.6 * math.exp(-0.3 * 0)
TOPK = 16
LANES = 128
SUBLANES = 8
NEG_BIG = -1e30
VMEM_LIMIT = 56 * 1024 * 1024

bf16 = jnp.bfloat16
f32 = jnp.float32


def _cparams(sem):
    return pltpu.CompilerParams(dimension_semantics=sem, vmem_limit_bytes=VMEM_LIMIT)


def _dot(a, b):
    return jnp.dot(a, b, preferred_element_type=f32)


def _dot_nt(a, b):
    return lax.dot_general(a, b, (((1,), (1,)), ((), ())), preferred_element_type=f32)


def _sigmoid(x):
    return 1.0 / (1.0 + jnp.exp(-x))


def _gelu(x):
    return 0.5 * x * (1.0 + lax.erf(x * (1.0 / math.sqrt(2.0))))


def _stack_rows(rows):
    k = len(rows)
    idx = lax.broadcasted_iota(jnp.int32, (k, rows[0].shape[1]), 0)
    out = jnp.broadcast_to(rows[0], idx.shape)
    for j in range(1, k):
        out = jnp.where(idx == j, rows[j], out)
    return out


def _rms_matmul_kernel(x_ref, g_ref, w_ref, o_ref, xn_ref):
    @pl.when(pl.program_id(1) == 0)
    def _():
        x = x_ref[...]
        ms = jnp.mean(x * x, axis=-1, keepdims=True)
        xn_ref[...] = (x * lax.rsqrt(ms + EPS) * g_ref[...]).astype(bf16)

    o_ref[...] = _dot(xn_ref[...], w_ref[...]).astype(o_ref.dtype)


def _rms_matmul(x, g, w, tm, tn, out_dtype, name):
    m, k = x.shape
    n = w.shape[1]
    return pl.pallas_call(
        _rms_matmul_kernel,
        grid=(m // tm, n // tn),
        in_specs=[pl.BlockSpec((tm, k), lambda i, j: (i, 0)),
                  pl.BlockSpec((1, k), lambda i, j: (0, 0)),
                  pl.BlockSpec((k, tn), lambda i, j: (0, j))],
        out_specs=pl.BlockSpec((tm, tn), lambda i, j: (i, j)),
        out_shape=jax.ShapeDtypeStruct((m, n), out_dtype),
        scratch_shapes=[pltpu.VMEM((tm, k), bf16)],
        compiler_params=_cparams(("parallel", "arbitrary")),
        name=name,
    )(x, g, w)


def _normrot_kernel(x_ref, g_ref, cos_ref, sin_ref, j_ref, o_ref, *, scale, sub):
    width = x_ref.shape[1]
    cos = cos_ref[...]
    sin = sin_ref[...]
    jmat = j_ref[...]
    lane = lax.broadcasted_iota(jnp.int32, cos.shape, 1)
    first_half = (lane % sub) < (sub // 2)
    for t in range(width // LANES):
        sl = slice(t * LANES, (t + 1) * LANES)
        x = x_ref[:, sl].astype(f32)
        x2 = x * x
        hi = x2.astype(bf16)
        lo = (x2 - hi.astype(f32)).astype(bf16)
        ssq = _dot(hi, jmat) + _dot(lo, jmat)
        xn = x * lax.rsqrt(ssq * (1.0 / sub) + EPS) * g_ref[:, sl]
        partner = jnp.where(first_half, pltpu.roll(xn, LANES - sub // 2, 1), pltpu.roll(xn, sub // 2, 1))
        o_ref[:, sl] = ((xn * cos + partner * sin) * scale).astype(o_ref.dtype)


def _normrot(x, col_block, width, g_t, cos, sin, jmat, tm, scale, sub, name):
    m = x.shape[0]
    nt = cos.shape[0] // tm
    return pl.pallas_call(
        functools.partial(_normrot_kernel, scale=scale, sub=sub),
        grid=(m // tm,),
        in_specs=[pl.BlockSpec((tm, width), lambda i: (i, col_block)),
                  pl.BlockSpec((1, width), lambda i: (0, 0)),
                  pl.BlockSpec((tm, LANES), lambda i: (i % nt, 0)),
                  pl.BlockSpec((tm, LANES), lambda i: (i % nt, 0)),
                  pl.BlockSpec((LANES, LANES), lambda i: (0, 0))],
        out_specs=pl.BlockSpec((tm, width), lambda i: (i, 0)),
        out_shape=jax.ShapeDtypeStruct((m, width), bf16),
        compiler_params=_cparams(("parallel",)),
        name=name,
    )(x, g_t, cos, sin, jmat)


def _attn_kernel(q_ref, k_ref, v_ref, km_ref, vm_ref, lq1_ref, lk1_ref, lq2_ref, lk2_ref, sg_ref,
                 o_ref, qq_scr, m_scr, acc_scr, s_scr, *, n_meta, sub, hpb):
    i = pl.program_id(2)
    tq = q_ref.shape[0]
    tk = tq
    hw = 2 * sub
    mp = km_ref.shape[0]
    lane = lax.broadcasted_iota(jnp.int32, (tq, hw), 1)
    head = lambda hh: slice(hh * hw, (hh + 1) * hw)

    def softmax_step(hh, s, v_aug, first):
        smax = jnp.max(s, axis=1, keepdims=True)
        if first:
            m_new = jnp.broadcast_to(smax, (2 * tq, hw))
        else:
            m_prev = m_scr[hh]
            m_new = jnp.maximum(m_prev, smax)
        p = jnp.concatenate([jnp.exp2(s[:, c * hw:(c + 1) * hw] - m_new) for c in range(s.shape[1] // hw)], axis=1)
        pv = _dot(p.astype(bf16), v_aug)
        if first:
            acc_scr[hh] = pv
        else:
            alpha = jnp.exp2(m_prev - m_new)
            acc_scr[hh] = jnp.concatenate([alpha, alpha], axis=1) * acc_scr[hh] + pv
        m_scr[hh] = m_new

    for hh in range(hpb):
        q = q_ref[:, head(hh)]
        zero = jnp.zeros_like(q)
        qq_scr[hh] = jnp.concatenate([jnp.where(lane < sub, q, zero), jnp.where(lane >= sub, q, zero)], axis=0)

    for hh in range(hpb):
        s = _dot_nt(qq_scr[hh], km_ref[:, head(hh)])
        col = lax.broadcasted_iota(jnp.int32, s.shape, 1)
        s = jnp.where(col < n_meta, s, NEG_BIG)
        softmax_step(hh, s, jnp.concatenate([vm_ref[:, head(hh)], jnp.ones((mp, hw), bf16)], axis=1), True)

    def put_scores(j, slot):
        start = pl.multiple_of(j * tk, tk)
        for hh in range(hpb):
            s_scr[slot, hh] = _dot_nt(qq_scr[hh], k_ref[pl.ds(start, tk), head(hh)])

    def consume(j, slot, masked):
        start = pl.multiple_of(j * tk, tk)
        for hh in range(hpb):
            s = s_scr[slot, hh]
            if masked:
                row = lax.broadcasted_iota(jnp.int32, s.shape, 0) % tq
                col = lax.broadcasted_iota(jnp.int32, s.shape, 1)
                s = jnp.where(col <= row, s, NEG_BIG)
            vb = v_ref[pl.ds(start, tk), head(hh)]
            softmax_step(hh, s, jnp.concatenate([vb, jnp.ones((tk, hw), bf16)], axis=1), False)

    def stage(j, slot):
        put_scores(j + 1, 1 - slot)
        consume(j, slot, False)

    put_scores(0, 0)

    def body(p, c):
        stage(2 * p, 0)
        stage(2 * p + 1, 1)
        return c

    lax.fori_loop(0, i // 2, body, 0)

    @pl.when(i % 2 == 0)
    def _():
        consume(i, 0, True)

    @pl.when(i % 2 == 1)
    def _():
        stage(i - 1, 0)
        consume(i, 1, True)

    lam = (jnp.exp(jnp.sum(lq1_ref[...] * lk1_ref[...], axis=1, keepdims=True))
           - jnp.exp(jnp.sum(lq2_ref[...] * lk2_ref[...], axis=1, keepdims=True)) + LAM_INIT)
    for hh in range(hpb):
        acc = acc_scr[hh]
        o = acc[:, :hw] / acc[:, hw:]
        att = o[:tq] - lam * o[tq:]
        ms = jnp.mean(att * att, axis=-1, keepdims=True)
        att = att * lax.rsqrt(ms + EPS) * sg_ref[...] * (1.0 - LAM_INIT)
        o_ref[:, head(hh)] = att.astype(o_ref.dtype)


def _attention(q_rot, k_rot, proj, v_col0, km, vm, lq1, lk1, lq2, lk2, sg, batch, seq, heads, tq, n_meta, sub):
    t, aw = q_rot.shape
    nq = seq // tq
    hw = 2 * sub
    hpb = 2 if heads % 2 == 0 else 1
    bw = hpb * hw
    mp = km.shape[0]
    vec = lambda: pl.BlockSpec((1, sub), lambda b, h, i: (0, 0))
    return pl.pallas_call(
        functools.partial(_attn_kernel, n_meta=n_meta, sub=sub, hpb=hpb),
        grid=(batch, heads // hpb, nq),
        in_specs=[pl.BlockSpec((tq, bw), lambda b, h, i: (b * nq + i, h)),
                  pl.BlockSpec((seq, bw), lambda b, h, i: (b, h)),
                  pl.BlockSpec((seq, bw), lambda b, h, i: (b, v_col0 // hpb + h)),
                  pl.BlockSpec((mp, bw), lambda b, h, i: (0, h)),
                  pl.BlockSpec((mp, bw), lambda b, h, i: (0, h)),
                  vec(), vec(), vec(), vec(),
                  pl.BlockSpec((1, hw), lambda b, h, i: (0, 0))],
        out_specs=pl.BlockSpec((tq, bw), lambda b, h, i: (b * nq + i, h)),
        out_shape=jax.ShapeDtypeStruct((t, aw), bf16),
        scratch_shapes=[pltpu.VMEM((hpb, 2 * tq, hw), bf16), pltpu.VMEM((hpb, 2 * tq, hw), f32),
                        pltpu.VMEM((hpb, 2 * tq, 2 * hw), f32), pltpu.VMEM((2, hpb, 2 * tq, tq), f32)],
        compiler_params=_cparams(("parallel", "parallel", "arbitrary")),
        name="diff_attention",
    )(q_rot, k_rot, proj, km, vm, lq1, lk1, lq2, lk2, sg)


def _cmul(ar, ai, br, bi):
    return ar * br - ai * bi, ar * bi + ai * br


def _ssm_prep_kernel(are_ref, aim_ref, ldt_ref, bre_ref, bim_ref, cre_ref, cim_ref,
                     pwr_ref, pwi_ref, bdr_ref, bdi_ref, bor_ref, boi_ref, *, n_state, n_chan):
    a_re = are_ref[...]
    a_im = aim_ref[...]
    dt = jnp.exp(ldt_ref[...])
    er = jnp.exp(a_re * dt)
    ab_re = er * jnp.cos(a_im * dt)
    ab_im = er * jnp.sin(a_im * dt)
    nr = ab_re - 1.0
    ni = ab_im
    den = a_re * a_re + a_im * a_im
    f_re = (nr * a_re + ni * a_im) / den
    f_im = (ni * a_re - nr * a_im) / den
    p1 = (ab_re, ab_im)
    p2 = _cmul(*p1, *p1)
    p3 = _cmul(*p2, *p1)
    p4 = _cmul(*p2, *p2)
    p5 = _cmul(*p4, *p1)
    p6 = _cmul(*p4, *p2)
    p7 = _cmul(*p4, *p3)
    p8 = _cmul(*p4, *p4)
    pw = (p1, p2, p3, p4, p5, p6, p7, p8)
    pwr_ref[...] = _stack_rows([p[0] for p in pw])
    pwi_ref[...] = _stack_rows([p[1] for p in pw])
    b_re = bre_ref[0]
    b_im = bim_ref[0]
    r = lax.broadcasted_iota(jnp.int32, b_re.shape, 0) // n_chan
    c = lax.broadcasted_iota(jnp.int32, b_re.shape, 1) // n_state
    keep = r == c
    bdr_ref[0] = jnp.where(keep, f_re * b_re - f_im * b_im, 0.0).astype(bf16)
    bdi_ref[0] = jnp.where(keep, f_re * b_im + f_im * b_re, 0.0).astype(bf16)
    c_re = cre_ref[0]
    c_im = cim_ref[0]
    r = lax.broadcasted_iota(jnp.int32, c_re.shape, 0) // n_state
    c = lax.broadcasted_iota(jnp.int32, c_re.shape, 1) // n_chan
    keep = r == c
    bor_ref[0] = jnp.where(keep, c_re, 0.0).astype(bf16)
    boi_ref[0] = jnp.where(keep, c_im, 0.0).astype(bf16)


def _ssm_prep(a_re, a_im, ldt, b_re_t, b_im_t, c_re_t, c_im_t, n_state, n_chan):
    ngt, _, sl = b_re_t.shape
    gp = a_re.shape[1]
    lane_vec = lambda: pl.BlockSpec((1, sl), lambda g: (0, g))
    bin_spec = lambda: pl.BlockSpec((1, LANES, sl), lambda g: (g, 0, 0))
    bout_spec = lambda: pl.BlockSpec((1, sl, LANES), lambda g: (g, 0, 0))
    return pl.pallas_call(
        functools.partial(_ssm_prep_kernel, n_state=n_state, n_chan=n_chan),
        grid=(ngt,),
        in_specs=[lane_vec(), lane_vec(), lane_vec(), bin_spec(), bin_spec(), bout_spec(), bout_spec()],
        out_specs=[pl.BlockSpec((SUBLANES, sl), lambda g: (0, g)), pl.BlockSpec((SUBLANES, sl), lambda g: (0, g)),
                   bin_spec(), bin_spec(), bout_spec(), bout_spec()],
        out_shape=[jax.ShapeDtypeStruct((SUBLANES, gp), f32), jax.ShapeDtypeStruct((SUBLANES, gp), f32),
                   jax.ShapeDtypeStruct(b_re_t.shape, bf16), jax.ShapeDtypeStruct(b_re_t.shape, bf16),
                   jax.ShapeDtypeStruct(c_re_t.shape, bf16), jax.ShapeDtypeStruct(c_re_t.shape, bf16)],
        compiler_params=_cparams(("parallel",)),
        name="ssm_prep",
    )(a_re, a_im, ldt, b_re_t, b_im_t, c_re_t, c_im_t)


def _ssm_scan_kernel(u_ref, bdr_ref, bdi_ref, bor_ref, boi_ref, pwr_ref, pwi_ref, d_ref, x0r_ref, x0i_ref,
                     y_ref, xfr_ref, xfi_ref, xr_scr, xi_scr, cr_scr, ci_scr):
    i = pl.program_id(2)
    tm = u_ref.shape[0]

    @pl.when(i == 0)
    def _():
        cr_scr[...] = x0r_ref[...]
        ci_scr[...] = x0i_ref[...]

    u = u_ref[...]
    xr = _dot(u, bdr_ref[0])
    xi = _dot(u, bdi_ref[0])
    pwr = pwr_ref[...]
    pwi = pwi_ref[...]
    sl = xr.shape[1]
    xr = xr.reshape(tm // SUBLANES, SUBLANES, sl)
    xi = xi.reshape(tm // SUBLANES, SUBLANES, sl)
    row = lax.broadcasted_iota(jnp.int32, (SUBLANES, sl), 0)
    for k in (1, 2, 4):
        mr = jnp.where(row >= k, pwr[k - 1:k], 0.0)[None]
        mi = jnp.where(row >= k, pwi[k - 1:k], 0.0)[None]
        tr, ti = _cmul(mr, mi, pltpu.roll(xr, k, 1), pltpu.roll(xi, k, 1))
        xr = xr + tr
        xi = xi + ti
    xr_scr[...] = xr.reshape(tm, sl)
    xi_scr[...] = xi.reshape(tm, sl)

    def body(r, carry):
        cr, ci = carry
        rows = pl.ds(pl.multiple_of(r * SUBLANES, SUBLANES), SUBLANES)
        tr, ti = _cmul(pwr, pwi, cr, ci)
        nr = xr_scr[rows, :] + tr
        ni = xi_scr[rows, :] + ti
        xr_scr[rows, :] = nr
        xi_scr[rows, :] = ni
        return nr[SUBLANES - 1:SUBLANES], ni[SUBLANES - 1:SUBLANES]

    cr, ci = lax.fori_loop(0, tm // SUBLANES, body, (cr_scr[...], ci_scr[...]))
    cr_scr[...] = cr
    ci_scr[...] = ci
    xfr_ref[0] = cr
    xfi_ref[0] = ci

    y = _dot(xr_scr[...].astype(bf16), bor_ref[0]) - _dot(xi_scr[...].astype(bf16), boi_ref[0])
    y = y + d_ref[...] * u.astype(f32)
    y_ref[...] = _gelu(y).astype(y_ref.dtype)


def _ssm_scan(u_arr, u_col0, prep, d, x0r, x0i, batch, seq, tm):
    pwr, pwi, bdr, bdi, bor, boi = prep
    ngt, _, sl = bdr.shape
    nt = seq // tm
    bin_spec = lambda: pl.BlockSpec((1, LANES, sl), lambda g, b, i: (g, 0, 0))
    bout_spec = lambda: pl.BlockSpec((1, sl, LANES), lambda g, b, i: (g, 0, 0))
    pw_spec = lambda: pl.BlockSpec((SUBLANES, sl), lambda g, b, i: (0, g))
    x0_spec = lambda: pl.BlockSpec((1, sl), lambda g, b, i: (0, g))
    xf_spec = lambda: pl.BlockSpec((1, 1, sl), lambda g, b, i: (b, 0, g))
    return pl.pallas_call(
        _ssm_scan_kernel,
        grid=(ngt, batch, nt),
        in_specs=[pl.BlockSpec((tm, LANES), lambda g, b, i: (b * nt + i, u_col0 + g)),
                  bin_spec(), bin_spec(), bout_spec(), bout_spec(), pw_spec(), pw_spec(),
                  pl.BlockSpec((1, LANES), lambda g, b, i: (0, g)), x0_spec(), x0_spec()],
        out_specs=[pl.BlockSpec((tm, LANES), lambda g, b, i: (b * nt + i, g)), xf_spec(), xf_spec()],
        out_shape=[jax.ShapeDtypeStruct((batch * seq, ngt * LANES), bf16),
                   jax.ShapeDtypeStruct((batch, 1, ngt * sl), f32),
                   jax.ShapeDtypeStruct((batch, 1, ngt * sl), f32)],
        scratch_shapes=[pltpu.VMEM((tm, sl), f32), pltpu.VMEM((tm, sl), f32),
                        pltpu.VMEM((1, sl), f32), pltpu.VMEM((1, sl), f32)],
        compiler_params=_cparams(("parallel", "arbitrary", "arbitrary")),
        name="ssm_scan",
    )(u_arr, bdr, bdi, bor, boi, pwr, pwi, d, x0r, x0i)


def _attn_out_kernel(a_ref, w_ref, g_ref, o_ref):
    y = _dot(a_ref[...], w_ref[...])
    o_ref[...] = (_sigmoid(g_ref[...].astype(f32)) * y).astype(o_ref.dtype)


def _glu_mix_kernel(s_ref, wa_ref, wb_ref, g_ref, ya_ref, o_ref):
    s = s_ref[...]
    ga = _dot(s, wa_ref[...])
    gb = _dot(s, wb_ref[...])
    yb = ga * _sigmoid(gb)
    o_ref[...] = (ya_ref[...].astype(f32) + _sigmoid(g_ref[...].astype(f32)) * yb).astype(o_ref.dtype)


def _out_proj_kernel(m_ref, w_ref, x_ref, o_ref):
    o_ref[...] = x_ref[...] + _dot(m_ref[...], w_ref[...])


def _peer_query_kernel(h_ref, g_ref, wq_ref, kk_ref, hn_ref, st_ref):
    x = h_ref[...]
    ms = jnp.mean(x * x, axis=-1, keepdims=True)
    hn_f = x * lax.rsqrt(ms + EPS) * g_ref[...]
    hn_ref[...] = hn_f.T.astype(bf16)
    q = _dot(hn_f.astype(bf16), wq_ref[...]).astype(bf16)
    n_heads = kk_ref.shape[0]
    half = kk_ref.shape[3]
    for h in range(n_heads):
        for side in range(2):
            c0 = (2 * h + side) * half
            st_ref[h, side] = _dot_nt(kk_ref[h, side], q[:, c0:c0 + half])


def _peer_query(h1, g, wq, kk, tm):
    t, d = h1.shape
    nh, _, nk, half = kk.shape
    return pl.pallas_call(
        _peer_query_kernel,
        grid=(t // tm,),
        in_specs=[pl.BlockSpec((tm, d), lambda i: (i, 0)),
                  pl.BlockSpec((1, d), lambda i: (0, 0)),
                  pl.BlockSpec(wq.shape, lambda i: (0, 0)),
                  pl.BlockSpec(kk.shape, lambda i: (0, 0, 0, 0))],
        out_specs=[pl.BlockSpec((d, tm), lambda i: (0, i)),
                   pl.BlockSpec((nh, 2, nk, tm), lambda i: (0, 0, 0, i))],
        out_shape=[jax.ShapeDtypeStruct((d, t), bf16), jax.ShapeDtypeStruct((nh, 2, nk, t), f32)],
        compiler_params=_cparams(("parallel",)),
        name="peer_query",
    )(h1, g, wq, kk)


def _sort_pairs(n):
    pairs = []
    p = 1
    while p < n:
        k = p
        while k >= 1:
            for j in range(k % p, n - k, 2 * k):
                for i in range(min(k, n - j - k)):
                    if (i + j) // (2 * p) == (i + j + k) // (2 * p):
                        pairs.append((i + j, i + j + k))
            k //= 2
        p *= 2
    return pairs


def _top_desc(tiles, n):
    m = len(tiles)
    vs = list(tiles)
    for i, j in _sort_pairs(1 << (m - 1).bit_length()):
        if j < m:
            vs[i], vs[j] = jnp.maximum(vs[i], vs[j]), jnp.minimum(vs[i], vs[j])
    neg = jnp.full(vs[0].shape, -jnp.inf, f32)
    vs = vs[:n] + [neg]
    vals = []
    for a in range(n):
        v = jnp.max(vs[0], axis=0, keepdims=True)
        vals.append(v)
        hit = vs[0] == v
        keep = min(len(vs) - 1, n - a - 1)
        vs = [jnp.where(hit, vs[k + 1], vs[k]) for k in range(keep)] + [neg]
    return _stack_rows(vals)


def _peer_topk_kernel(st_ref, r2_ref, e2_ref, n1_ref, e1_ref):
    chunk = LANES
    sub8 = lambda x: [x[k * SUBLANES:(k + 1) * SUBLANES] for k in range(x.shape[0] // SUBLANES)]
    for c in range(st_ref.shape[3] // chunk):
        cols = slice(c * chunk, (c + 1) * chunk)
        s1 = st_ref[0, 0, :, cols]
        s2 = st_ref[0, 1, :, cols]
        v1 = _top_desc(sub8(s1), TOPK)
        v2 = _top_desc(sub8(s2), TOPK)
        half = TOPK // 2
        cand = ([v1[0:1] + v2[:half], v1[0:1] + v2[half:]] + [v1[a:a + 1] + v2[:half] for a in range(1, half)]
                + [v1[half:] + v2[0:1]])
        best = _top_desc(cand, TOPK)
        tau = best[TOPK - 1:TOPK]
        z = jnp.sum(jnp.exp(best - best[0:1]), axis=0, keepdims=True)
        n1 = jnp.zeros(s1.shape, f32)
        rank2 = jnp.zeros(s2.shape, f32)
        for a in range(TOPK):
            cnt = jnp.sum(jnp.where((v1[a:a + 1] + v2) >= tau, 1.0, 0.0), axis=0, keepdims=True)
            n1 = jnp.where(s1 == v1[a:a + 1], cnt, n1)
            rank2 = jnp.where(s2 < v2[a:a + 1], a + 1.0, rank2)
        r2_ref[0, :, cols] = rank2.astype(bf16)
        e2_ref[0, :, cols] = jnp.exp(s2 - v2[0:1]).astype(bf16)
        n1_ref[0, :, cols] = n1
        e1_ref[0, :, cols] = jnp.exp(s1 - v1[0:1]) / z


def _peer_topk(st, tl):
    nh, _, nk, t = st.shape
    spec = lambda: pl.BlockSpec((1, nk, tl), lambda h, i: (h, 0, i))
    shp = lambda dt: jax.ShapeDtypeStruct((nh, nk, t), dt)
    return pl.pallas_call(
        _peer_topk_kernel,
        grid=(nh, t // tl),
        in_specs=[pl.BlockSpec((1, 2, nk, tl), lambda h, i: (h, 0, 0, i))],
        out_specs=[spec(), spec(), spec(), spec()],
        out_shape=[shp(bf16), shp(bf16), shp(f32), shp(f32)],
        compiler_params=_cparams(("parallel", "parallel")),
        name="peer_topk",
    )(st)


def _peer_dense_kernel(hn_ref, u_ref, vt_ref, r2_ref, e2_ref, n1_ref, e1_ref, h1_ref, o_ref,
                       acc_scr, a0_scr, a1_scr, *, nb, tn):
    e = pl.program_id(1)
    n_heads, nk, tm = r2_ref.shape
    eb = u_ref.shape[0]
    pk = 2 * SUBLANES

    @pl.when(e == 0)
    def _():
        acc_scr[...] = jnp.zeros_like(acc_scr)
        a1_scr[...] = jnp.zeros_like(a1_scr)

    def step(a_cur, a_prev):
        a_cur[...] = _dot(u_ref[...], hn_ref[...])
        blk = jnp.maximum(e - 1, 0)
        for c in range(tm // tn):
            cols = slice(c * tn, (c + 1) * tn)
            w_rows = []
            for ii in range(eb // nk):
                i1 = blk * (eb // nk) + ii
                gate = None
                for h in range(n_heads):
                    n1 = jnp.broadcast_to(n1_ref[h, pl.ds(i1, 1), cols], (pk, tn)).astype(bf16)
                    e1 = jnp.broadcast_to(e1_ref[h, pl.ds(i1, 1), cols], (pk, tn)).astype(bf16)
                    r2 = r2_ref[h, :, cols].reshape(nk // pk, pk, tn)
                    e2 = e2_ref[h, :, cols].reshape(nk // pk, pk, tn)
                    term = jnp.where(r2 < n1[None], e2, jnp.zeros_like(e2)) * e1[None]
                    gate = term if gate is None else gate + term
                g = _gelu(a_prev[ii * nk:(ii + 1) * nk, cols]).astype(bf16)
                w_rows.append(gate.reshape(nk, tn) * g)
            acc_scr[:, cols] += _dot(vt_ref[...], jnp.concatenate(w_rows, axis=0))

    @pl.when(e % 2 == 0)
    def _():
        step(a0_scr, a1_scr)

    @pl.when(e % 2 == 1)
    def _():
        step(a1_scr, a0_scr)

    @pl.when(e == nb)
    def _():
        o_ref[...] = h1_ref[...] + acc_scr[...].T


def _peer_dense(hn_t, u_b, vt_b, r2, e2, n1, e1, h1, tm, eb):
    d, t = hn_t.shape
    nb = u_b.shape[0] // eb
    nh, nk, _ = r2.shape
    tok = lambda: pl.BlockSpec((nh, nk, tm), lambda i, e: (0, 0, i), pipeline_mode=pl.Buffered(1))
    return pl.pallas_call(
        functools.partial(_peer_dense_kernel, nb=nb, tn=_tile(tm, 2 * LANES)),
        grid=(t // tm, nb + 1),
        in_specs=[pl.BlockSpec((d, tm), lambda i, e: (0, i)),
                  pl.BlockSpec((eb, d), lambda i, e: (jnp.minimum(e, nb - 1), 0)),
                  pl.BlockSpec((d, eb), lambda i, e: (0, jnp.maximum(e - 1, 0))),
                  tok(), tok(), tok(), tok(),
                  pl.BlockSpec((tm, d), lambda i, e: (i, 0), pipeline_mode=pl.Buffered(1))],
        out_specs=pl.BlockSpec((tm, d), lambda i, e: (i, 0)),
        out_shape=jax.ShapeDtypeStruct((t, d), f32),
        scratch_shapes=[pltpu.VMEM((d, tm), f32), pltpu.VMEM((eb, tm), f32), pltpu.VMEM((eb, tm), f32)],
        compiler_params=_cparams(("parallel", "arbitrary")),
        name="peer_dense",
    )(hn_t, u_b, vt_b, r2, e2, n1, e1, h1)


def _tile(n, pref):
    if n <= pref:
        return n
    t = pref // LANES * LANES
    while n % t:
        t -= LANES
    assert t > 0, (n, pref)
    return t


def kernel(x, meta_tokens, norm1_g, w_in, q_norm_g, k_norm_g, lambda_q1, lambda_k1, lambda_q2, lambda_k2, subln_g, w_attn_branch, ssm_a_re, ssm_a_im, ssm_log_dt, ssm_b_re, ssm_b_im, ssm_c_re, ssm_c_im, ssm_d, w_glu, w_out, norm2_g, peer_w_q, peer_k1, peer_k2, peer_u, peer_v):
    assert norm1_g.shape[0] == 1, "single-layer block only"
    batch, seq, d = x.shape
    n_meta = meta_tokens.shape[0]
    sub = q_norm_g.shape[-1]
    aw = w_attn_branch.shape[1]
    heads = aw // (2 * sub)
    n_groups, n_state, n_chan = ssm_b_re.shape[1:]
    sw = n_groups * n_chan
    assert 2 * sub == LANES and LANES % n_chan == 0 and sw % LANES == 0
    assert n_meta % SUBLANES == 0 and n_meta <= LANES
    t = batch * seq
    n_in = w_in.shape[2]
    assert n_in == 3 * aw + sw + 2 * d

    x2 = x.reshape(t, d)
    w_in_b = w_in[0].astype(bf16)

    proj = _rms_matmul(x2, norm1_g, w_in_b, _tile(t, 1024), _tile(n_in, 1024), bf16, "in_proj")
    proj_m = _rms_matmul(meta_tokens, norm1_g, w_in_b, n_meta, _tile(n_in, 1024), bf16, "in_proj_meta")

    half = sub // 2
    inv_freq = jnp.power(ROPE_THETA, -jnp.arange(half, dtype=f32) / half)
    ang = jnp.arange(n_meta + seq, dtype=f32)[:, None] * inv_freq[None, :]
    cos_t = jnp.tile(jnp.cos(ang), (1, LANES // half))
    sin_t = jnp.tile(jnp.concatenate([-jnp.sin(ang), jnp.sin(ang)], axis=1), (1, LANES // sub))
    lane = jnp.arange(LANES)
    jmat = (lane[:, None] // sub == lane[None, :] // sub).astype(bf16)
    gq = jnp.tile(q_norm_g, (1, aw // sub))
    gk = jnp.tile(k_norm_g, (1, aw // sub))
    tr = _tile(seq, 512)
    q_rot = _normrot(proj, 0, aw, gq, cos_t[n_meta:], sin_t[n_meta:], jmat, tr, sub ** -0.5 * math.log2(math.e), sub,
                      "q_normrot")
    k_rot = _normrot(proj, 1, aw, gk, cos_t[n_meta:], sin_t[n_meta:], jmat, tr, 1.0, sub, "k_normrot")
    km_rot = _normrot(proj_m, 1, aw, gk, cos_t[:n_meta], sin_t[:n_meta], jmat, n_meta, 1.0, sub, "k_normrot_meta")
    km = jnp.pad(km_rot, ((0, LANES - n_meta), (0, 0)))
    vm = jnp.pad(proj_m[:, 2 * aw:3 * aw], ((0, LANES - n_meta), (0, 0)))

    att = _attention(q_rot, k_rot, proj, 2 * aw // LANES, km, vm, lambda_q1, lambda_k1, lambda_q2, lambda_k2,
                     subln_g, batch, seq, heads, _tile(seq, 512), n_meta, sub)

    gpt = LANES // n_chan
    ngt = sw // LANES
    sl = gpt * n_state
    gp = n_groups * n_state
    a_re = ssm_a_re.reshape(1, gp)
    a_im = ssm_a_im.reshape(1, gp)
    ldt = jnp.repeat(ssm_log_dt[0], n_state).reshape(1, gp)
    lay_b = lambda b: jnp.tile(b[0].transpose(0, 2, 1).reshape(ngt, LANES, n_state), (1, 1, gpt))
    lay_c = lambda c: jnp.tile(c[0].transpose(0, 2, 1).reshape(ngt, sl, n_chan), (1, 1, gpt))
    prep = _ssm_prep(a_re, a_im, ldt, lay_b(ssm_b_re), lay_b(ssm_b_im), lay_c(ssm_c_re), lay_c(ssm_c_im),
                     n_state, n_chan)
    zeros = jnp.zeros((1, gp), f32)
    u_col0 = 3 * aw // LANES
    _, xmr, xmi = _ssm_scan(proj_m, u_col0, prep, ssm_d, zeros, zeros, 1, n_meta, n_meta)
    ys, _, _ = _ssm_scan(proj, u_col0, prep, ssm_d, xmr[0], xmi[0], batch, seq, _tile(seq, 512))

    tm = _tile(t, 1024)
    tn = _tile(d, 1024)
    nj = d // tn
    ga_col0 = (3 * aw + sw) // tn
    gb_col0 = (3 * aw + sw + d) // tn
    grid = (t // tm, nj)
    sem = _cparams(("parallel", "arbitrary"))
    row_full = lambda k: pl.BlockSpec((tm, k), lambda i, j: (i, 0))
    w_col = lambda k, off=0: pl.BlockSpec((k, tn), lambda i, j: (0, j + off))
    out_tile = pl.BlockSpec((tm, tn), lambda i, j: (i, j))
    yag = pl.pallas_call(
        _attn_out_kernel, grid=grid,
        in_specs=[row_full(aw), w_col(aw), pl.BlockSpec((tm, tn), lambda i, j: (i, ga_col0 + j))],
        out_specs=out_tile, out_shape=jax.ShapeDtypeStruct((t, d), bf16), compiler_params=sem, name="attn_out",
    )(att, w_attn_branch[0].astype(bf16), proj)
    w_glu_b = w_glu[0].astype(bf16)
    mix = pl.pallas_call(
        _glu_mix_kernel, grid=grid,
        in_specs=[row_full(sw), w_col(sw), w_col(sw, nj), pl.BlockSpec((tm, tn), lambda i, j: (i, gb_col0 + j)),
                  out_tile],
        out_specs=out_tile, out_shape=jax.ShapeDtypeStruct((t, d), bf16), compiler_params=sem, name="glu_mix",
    )(ys, w_glu_b, w_glu_b, proj, yag)
    h1 = pl.pallas_call(
        _out_proj_kernel, grid=grid,
        in_specs=[row_full(d), w_col(d), out_tile],
        out_specs=out_tile, out_shape=jax.ShapeDtypeStruct((t, d), f32), compiler_params=sem, name="out_proj",
    )(mix, w_out[0].astype(bf16), x2)

    kk = jnp.stack([peer_k1[0], peer_k2[0]], axis=1).astype(bf16)
    assert kk.shape[2] == LANES and kk.shape[3] == LANES
    hn2, st = _peer_query(h1, norm2_g, peer_w_q[0].astype(bf16), kk, _tile(t, 512))
    r2, e2, n1, e1 = _peer_topk(st, _tile(t, 512))
    u_b = peer_u[0].astype(bf16)
    vt_b = peer_v[0].T.astype(bf16)
    out = _peer_dense(hn2, u_b, vt_b, r2, e2, n1, e1, h1, _tile(t, 512), _tile(u_b.shape[0], 1024))
    return out.reshape(batch, seq, d)
```

```python
import functools
import math

import jax
import jax.numpy as jnp
from jax import lax
from jax.experimental import pallas as pl
from jax.experimental.pallas import tpu as pltpu

EPS = 1e-6
ROPE_THETA = 10000.0
LAM_INIT = 0.8 - 0.6 * math.exp(-0.3 * 0)
TOPK = 16
LANES = 128
SUBLANES = 8
NEG_BIG = -1e30
VMEM_LIMIT = 56 * 1024 * 1024

bf16 = jnp.bfloat16
f32 = jnp.float32


def _cparams(sem):
    return pltpu.CompilerParams(dimension_semantics=sem, vmem_limit_bytes=VMEM_LIMIT)


def _dot(a, b):
    return jnp.dot(a, b, preferred_element_type=f32)


def _dot_nt(a, b):
    return lax.dot_general(a, b, (((1,), (1,)), ((), ())), preferred_element_type=f32)


def _sigmoid(x):
    return 1.0 / (1.0 + jnp.exp(-x))


def _gelu(x):
    return 0.5 * x * (1.0 + lax.erf(x * (1.0 / math.sqrt(2.0))))


def _stack_rows(rows):
    k = len(rows)
    idx = lax.broadcasted_iota(jnp.int32, (k, rows[0].shape[1]), 0)
    out = jnp.broadcast_to(rows[0], idx.shape)
    for j in range(1, k):
        out = jnp.where(idx == j, rows[j], out)
    return out


def _rms_matmul_kernel(x_ref, g_ref, w_ref, o_ref, xn_ref):
    @pl.when(pl.program_id(1) == 0)
    def _():
        x = x_ref[...]
        ms = jnp.mean(x * x, axis=-1, keepdims=True)
        xn_ref[...] = (x * lax.rsqrt(ms + EPS) * g_ref[...]).astype(bf16)

    o_ref[...] = _dot(xn_ref[...], w_ref[...]).astype(o_ref.dtype)


def _rms_matmul(x, g, w, tm, tn, out_dtype, name):
    m, k = x.shape
    n = w.shape[1]
    return pl.pallas_call(
        _rms_matmul_kernel,
        grid=(m // tm, n // tn),
        in_specs=[pl.BlockSpec((tm, k), lambda i, j: (i, 0)),
                  pl.BlockSpec((1, k), lambda i, j: (0, 0)),
                  pl.BlockSpec((k, tn), lambda i, j: (0, j))],
        out_specs=pl.BlockSpec((tm, tn), lambda i, j: (i, j)),
        out_shape=jax.ShapeDtypeStruct((m, n), out_dtype),
        scratch_shapes=[pltpu.VMEM((tm, k), bf16)],
        compiler_params=_cparams(("parallel", "arbitrary")),
        name=name,
    )(x, g, w)


def _normrot_kernel(x_ref, g_ref, cos_ref, sin_ref, j_ref, o_ref, *, scale, sub):
    width = x_ref.shape[1]
    cos = cos_ref[...]
    sin = sin_ref[...]
    jmat = j_ref[...]
    lane = lax.broadcasted_iota(jnp.int32, cos.shape, 1)
    first_half = (lane % sub) < (sub // 2)
    for t in range(width // LANES):
        sl = slice(t * LANES, (t + 1) * LANES)
        x = x_ref[:, sl].astype(f32)
        x2 = x * x
        hi = x2.astype(bf16)
        lo = (x2 - hi.astype(f32)).astype(bf16)
        ssq = _dot(hi, jmat) + _dot(lo, jmat)
        xn = x * lax.rsqrt(ssq * (1.0 / sub) + EPS) * g_ref[:, sl]
        partner = jnp.where(first_half, pltpu.roll(xn, LANES - sub // 2, 1), pltpu.roll(xn, sub // 2, 1))
        o_ref[:, sl] = ((xn * cos + partner * sin) * scale).astype(o_ref.dtype)


def _normrot(x, col_block, width, g_t, cos, sin, jmat, tm, scale, sub, name):
    m = x.shape[0]
    nt = cos.shape[0] // tm
    return pl.pallas_call(
        functools.partial(_normrot_kernel, scale=scale, sub=sub),
        grid=(m // tm,),
        in_specs=[pl.BlockSpec((tm, width), lambda i: (i, col_block)),
                  pl.BlockSpec((1, width), lambda i: (0, 0)),
                  pl.BlockSpec((tm, LANES), lambda i: (i % nt, 0)),
                  pl.BlockSpec((tm, LANES), lambda i: (i % nt, 0)),
                  pl.BlockSpec((LANES, LANES), lambda i: (0, 0))],
        out_specs=pl.BlockSpec((tm, width), lambda i: (i, 0)),
        out_shape=jax.ShapeDtypeStruct((m, width), bf16),
        compiler_params=_cparams(("parallel",)),
        name=name,
    )(x, g_t, cos, sin, jmat)


def _attn_kernel(q_ref, k_ref, v_ref, km_ref, vm_ref, lq1_ref, lk1_ref, lq2_ref, lk2_ref, sg_ref,
                 o_ref, qq_scr, m_scr, acc_scr, s_scr, *, n_meta, sub, hpb):
    i = pl.program_id(2)
    tq = q_ref.shape[0]
    tk = tq
    hw = 2 * sub
    mp = km_ref.shape[0]
    lane = lax.broadcasted_iota(jnp.int32, (tq, hw), 1)
    head = lambda hh: slice(hh * hw, (hh + 1) * hw)

    def softmax_step(hh, s, v_aug, first):
        smax = jnp.max(s, axis=1, keepdims=True)
        if first:
            m_new = jnp.broadcast_to(smax, (2 * tq, hw))
        else:
            m_prev = m_scr[hh]
            m_new = jnp.maximum(m_prev, smax)
        p = jnp.concatenate([jnp.exp2(s[:, c * hw:(c + 1) * hw] - m_new) for c in range(s.shape[1] // hw)], axis=1)
        pv = _dot(p.astype(bf16), v_aug)
        if first:
            acc_scr[hh] = pv
        else:
            alpha = jnp.exp2(m_prev - m_new)
            acc_scr[hh] = jnp.concatenate([alpha, alpha], axis=1) * acc_scr[hh] + pv
        m_scr[hh] = m_new

    for hh in range(hpb):
        q = q_ref[:, head(hh)]
        zero = jnp.zeros_like(q)
        qq_scr[hh] = jnp.concatenate([jnp.where(lane < sub, q, zero), jnp.where(lane >= sub, q, zero)], axis=0)

    for hh in range(hpb):
        s = _dot_nt(qq_scr[hh], km_ref[:, head(hh)])
        col = lax.broadcasted_iota(jnp.int32, s.shape, 1)
        s = jnp.where(col < n_meta, s, NEG_BIG)
        softmax_step(hh, s, jnp.concatenate([vm_ref[:, head(hh)], jnp.ones((mp, hw), bf16)], axis=1), True)

    def put_scores(j, slot):
        start = pl.multiple_of(j * tk, tk)
        for hh in range(hpb):
            s_scr[slot, hh] = _dot_nt(qq_scr[hh], k_ref[pl.ds(start, tk), head(hh)])

    def consume(j, slot, masked):
        start = pl.multiple_of(j * tk, tk)
        for hh in range(hpb):
            s = s_scr[slot, hh]
            if masked:
                row = lax.broadcasted_iota(jnp.int32, s.shape, 0) % tq
                col = lax.broadcasted_iota(jnp.int32, s.shape, 1)
                s = jnp.where(col <= row, s, NEG_BIG)
            vb = v_ref[pl.ds(start, tk), head(hh)]
            softmax_step(hh, s, jnp.concatenate([vb, jnp.ones((tk, hw), bf16)], axis=1), False)

    def stage(j, slot):
        put_scores(j + 1, 1 - slot)
        consume(j, slot, False)

    put_scores(0, 0)

    def body(p, c):
        stage(2 * p, 0)
        stage(2 * p + 1, 1)
        return c

    lax.fori_loop(0, i // 2, body, 0)

    @pl.when(i % 2 == 0)
    def _():
        consume(i, 0, True)

    @pl.when(i % 2 == 1)
    def _():
        stage(i - 1, 0)
        consume(i, 1, True)

    lam = (jnp.exp(jnp.sum(lq1_ref[...] * lk1_ref[...], axis=1, keepdims=True))
           - jnp.exp(jnp.sum(lq2_ref[...] * lk2_ref[...], axis=1, keepdims=True)) + LAM_INIT)
    for hh in range(hpb):
        acc = acc_scr[hh]
        o = acc[:, :hw] / acc[:, hw:]
        att = o[:tq] - lam * o[tq:]
        ms = jnp.mean(att * att, axis=-1, keepdims=True)
        att = att * lax.rsqrt(ms + EPS) * sg_ref[...] * (1.0 - LAM_INIT)
        o_ref[:, head(hh)] = att.astype(o_ref.dtype)


def _attention(q_rot, k_rot, proj, v_col0, km, vm, lq1, lk1, lq2, lk2, sg, batch, seq, heads, tq, n_meta, sub):
    t, aw = q_rot.shape
    nq = seq // tq
    hw = 2 * sub
    hpb = 2 if heads % 2 == 0 else 1
    bw = hpb * hw
    mp = km.shape[0]
    vec = lambda: pl.BlockSpec((1, sub), lambda b, h, i: (0, 0))
    return pl.pallas_call(
        functools.partial(_attn_kernel, n_meta=n_meta, sub=sub, hpb=hpb),
        grid=(batch, heads // hpb, nq),
        in_specs=[pl.BlockSpec((tq, bw), lambda b, h, i: (b * nq + i, h)),
                  pl.BlockSpec((seq, bw), lambda b, h, i: (b, h)),
                  pl.BlockSpec((seq, bw), lambda b, h, i: (b, v_col0 // hpb + h)),
                  pl.BlockSpec((mp, bw), lambda b, h, i: (0, h)),
                  pl.BlockSpec((mp, bw), lambda b, h, i: (0, h)),
                  vec(), vec(), vec(), vec(),
                  pl.BlockSpec((1, hw), lambda b, h, i: (0, 0))],
        out_specs=pl.BlockSpec((tq, bw), lambda b, h, i: (b * nq + i, h)),
        out_shape=jax.ShapeDtypeStruct((t, aw), bf16),
        scratch_shapes=[pltpu.VMEM((hpb, 2 * tq, hw), bf16), pltpu.VMEM((hpb, 2 * tq, hw), f32),
                        pltpu.VMEM((hpb, 2 * tq, 2 * hw), f32), pltpu.VMEM((2, hpb, 2 * tq, tq), f32)],
        compiler_params=_cparams(("parallel", "parallel", "arbitrary")),
        name="diff_attention",
    )(q_rot, k_rot, proj, km, vm, lq1, lk1, lq2, lk2, sg)


SSM_BLOCK = 8


def _cmul(ar, ai, br, bi):
    return ar * br - ai * bi, ar * bi + ai * br


def _dot3(a, b):
    ah = a.astype(bf16)
    al = (a - ah.astype(f32)).astype(bf16)
    bh = b.astype(bf16)
    bl = (b - bh.astype(f32)).astype(bf16)
    return _dot(ah, bh) + (_dot(ah, bl) + _dot(al, bh))


def _abar(a_re, a_im, log_dt):
    dt = jnp.exp(log_dt)
    er = jnp.exp(a_re * dt)
    return er * jnp.cos(a_im * dt), er * jnp.sin(a_im * dt)


def _powers(ar, ai, n):
    out = [(jnp.ones_like(ar), jnp.zeros_like(ai)), (ar, ai)]
    for _ in range(2, n + 1):
        out.append(_cmul(*out[-1], ar, ai))
    return out


def _ssm_prep_kernel(ar_ref, ai_ref, ld_ref, ac_ref, aic_ref, ldc_ref, as_ref, ais_ref, lds_ref,
                     br_ref, bi_ref, cr_ref, ci_ref, w_ref, q_ref, k_ref, pr_ref, ps_ref, *, n_state, n_chan):
    blk = SSM_BLOCK
    gpt = LANES // n_chan
    a_re, a_im = ar_ref[0], ai_ref[0]
    abr, abi = _abar(a_re, a_im, ld_ref[0])
    den = a_re * a_re + a_im * a_im
    nr, ni = abr - 1.0, abi
    f_re = (nr * a_re + ni * a_im) / den
    f_im = (ni * a_re - nr * a_im) / den
    bbr, bbi = _cmul(f_re, f_im, br_ref[0], bi_ref[0])
    pw = _powers(abr, abi, blk - 1)
    is_re = lax.broadcasted_iota(jnp.int32, a_re.shape, 1) < n_state
    rgrp = lax.broadcasted_iota(jnp.int32, a_re.shape, 0) // n_chan
    wf = []
    for i in range(blk):
        xr, xi = _cmul(bbr, bbi, *pw[blk - 1 - i])
        tile = jnp.where(is_re, xr, xi)
        wf.append(jnp.concatenate([jnp.where(rgrp == g, tile, 0.0) for g in range(gpt)], axis=1))
        w_ref[0, i] = wf[i].astype(bf16)
    c_re, c_im = cr_ref[0], ci_ref[0]
    acr, aci = _abar(ac_ref[0], aic_ref[0], ldc_ref[0])
    pwc = _powers(acr, aci, blk)
    cgrp = lax.broadcasted_iota(jnp.int32, c_re.shape, 1) // n_chan

    def by_group(tr, ti):
        rows = []
        for g in range(gpt):
            rows += [jnp.where(cgrp == g, tr, 0.0), jnp.where(cgrp == g, ti, 0.0)]
        return jnp.concatenate(rows, axis=0)

    for j in range(blk):
        qr, qi = _cmul(c_re, c_im, *pwc[j + 1])
        q_ref[0, j] = by_group(qr, -qi).astype(bf16)
    ccat = by_group(c_re, -c_im)
    for dd in range(blk):
        k_ref[0, dd] = _dot3(wf[blk - 1 - dd], ccat).astype(bf16)
    asr, asi = _abar(as_ref[0], ais_ref[0], lds_ref[0])
    a8 = _powers(asr, asi, blk)[blk]
    pl_ = _powers(a8[0], a8[1], SUBLANES)[1:]
    re_lane = lax.broadcasted_iota(jnp.int32, asr.shape, 1) % LANES < n_state
    pr_ref[0] = _stack_rows([p[0] for p in pl_])
    ps_ref[0] = _stack_rows([jnp.where(re_lane, -p[1], p[1]) for p in pl_])


def _ssm_prep(rows, cols, scan, b_t, c_t, n_state, n_chan):
    ngt = b_t[0].shape[0]
    blk = SSM_BLOCK
    sl = scan[0].shape[2]
    spec3 = lambda a: pl.BlockSpec((1,) + a.shape[1:], lambda g: (g, 0, 0))
    ins = list(rows) + list(cols) + list(scan) + list(b_t) + list(c_t)
    w_s = jax.ShapeDtypeStruct((ngt, blk, LANES, sl), bf16)
    q_s = jax.ShapeDtypeStruct((ngt, blk, sl, LANES), bf16)
    k_s = jax.ShapeDtypeStruct((ngt, blk, LANES, LANES), bf16)
    t_s = jax.ShapeDtypeStruct((ngt, SUBLANES, sl), f32)
    spec4 = lambda a: pl.BlockSpec((1,) + a.shape[1:], lambda g: (g, 0, 0, 0))
    return pl.pallas_call(
        functools.partial(_ssm_prep_kernel, n_state=n_state, n_chan=n_chan),
        grid=(ngt,),
        in_specs=[spec3(a) for a in ins],
        out_specs=[spec4(w_s), spec4(q_s), spec4(k_s), spec3(t_s), spec3(t_s)],
        out_shape=[w_s, q_s, k_s, t_s, t_s],
        compiler_params=_cparams(("parallel",)),
        name="ssm_prep",
    )(*ins)


def _ssm_kernel(u_ref, um_ref, w_ref, q_ref, k_ref, pr_ref, ps_ref, d_ref, y_ref,
                uf_scr, umf_scr, s_scr, sw_scr, xp_scr, yf_scr, *, rows_per_seq, meta_rows, n_state):
    blk = SSM_BLOCK
    r = u_ref.shape[0] // blk
    sl = s_scr.shape[1]
    tiles = sl // LANES
    pr, ps = pr_ref[0], ps_ref[0]

    def swap(x):
        return jnp.concatenate([pltpu.roll(x[:, c * LANES:(c + 1) * LANES], n_state, 1) for c in range(tiles)],
                               axis=1)

    uf_scr[...] = u_ref[...].astype(f32)
    umf_scr[...] = jnp.zeros_like(umf_scr)
    umf_scr[0:um_ref.shape[0], :] = um_ref[...].astype(f32)
    step_rows = lambda i: uf_scr[pl.ds(i, r, stride=blk), :]
    ub = [step_rows(i).astype(bf16) for i in range(blk)]
    s = _dot(ub[0], w_ref[0, 0])
    sm = _dot(umf_scr[pl.ds(0, SUBLANES, stride=blk), :].astype(bf16), w_ref[0, 0])
    for i in range(1, blk):
        s = s + _dot(ub[i], w_ref[0, i])
        sm = sm + _dot(umf_scr[pl.ds(i, SUBLANES, stride=blk), :].astype(bf16), w_ref[0, i])
    rowmod = lax.broadcasted_iota(jnp.int32, s.shape, 0) % SUBLANES
    for k in (1, 2, 4):
        t = pltpu.roll(s, k, 0)
        t = t * pr[k - 1:k] + swap(t) * ps[k - 1:k]
        s = s + jnp.where(rowmod >= k, t, 0.0)
    s_scr[...] = s
    sw_scr[...] = swap(s)
    x0 = sm[0:1]
    for mrow in range(1, meta_rows):
        x0 = x0 * pr[0:1] + swap(x0) * ps[0:1] + sm[mrow:mrow + 1]
    x0w = swap(x0)
    row8 = lax.broadcasted_iota(jnp.int32, (SUBLANES, sl), 0)

    def body(b, carry):
        start = b % (rows_per_seq // SUBLANES) == 0
        rows = pl.ds(pl.multiple_of(b * SUBLANES, SUBLANES), SUBLANES)
        c = jnp.broadcast_to(jnp.where(start, x0, carry[0]), (SUBLANES, sl))
        cw = jnp.broadcast_to(jnp.where(start, x0w, carry[1]), (SUBLANES, sl))
        xs = s_scr[rows, :] + (c * pr + cw * ps)
        xw = sw_scr[rows, :] + (cw * pr - c * ps)
        xp_scr[rows, :] = jnp.where(row8 == 0, c, pltpu.roll(xs, 1, 0))
        return xs[SUBLANES - 1:SUBLANES], xw[SUBLANES - 1:SUBLANES]

    lax.fori_loop(0, r // SUBLANES, body, (x0, x0w))

    xpb = xp_scr[...].astype(bf16)
    for j in range(blk):
        y = _dot(xpb, q_ref[0, j]) + d_ref[...] * step_rows(j)
        for i in range(j + 1):
            y = y + _dot(ub[i], k_ref[0, j - i])
        yf_scr[pl.ds(j, r, stride=blk), :] = _gelu(y)
    y_ref[...] = yf_scr[...].astype(y_ref.dtype)


def _ssm(u_arr, u_col0, um, prep, d, rows_per_seq, n_state):
    w, q, k, pr, ps = prep
    ngt, blk, _, sl = w.shape
    t = u_arr.shape[0]
    n_meta = um.shape[0]
    spec3 = lambda a: pl.BlockSpec((1,) + a.shape[1:], lambda g: (g, 0, 0))
    spec4 = lambda a: pl.BlockSpec((1,) + a.shape[1:], lambda g: (g, 0, 0, 0))
    return pl.pallas_call(
        functools.partial(_ssm_kernel, rows_per_seq=rows_per_seq, meta_rows=n_meta // blk, n_state=n_state),
        grid=(ngt,),
        in_specs=[pl.BlockSpec((t, LANES), lambda g: (0, u_col0 + g)),
                  pl.BlockSpec((n_meta, LANES), lambda g: (0, g)),
                  spec4(w), spec4(q), spec4(k), spec3(pr), spec3(ps),
                  pl.BlockSpec((1, LANES), lambda g: (0, g))],
        out_specs=pl.BlockSpec((t, LANES), lambda g: (0, g)),
        out_shape=jax.ShapeDtypeStruct((t, ngt * LANES), bf16),
        scratch_shapes=[pltpu.VMEM((t, LANES), f32), pltpu.VMEM((SUBLANES * blk, LANES), f32),
                        pltpu.VMEM((t // blk, sl), f32), pltpu.VMEM((t // blk, sl), f32),
                        pltpu.VMEM((t // blk, sl), f32), pltpu.VMEM((t, LANES), f32)],
        compiler_params=_cparams(("parallel",)),
        name="ssm_blocked",
    )(u_arr, um, w, q, k, pr, ps, d)


def _attn_out_kernel(a_ref, w_ref, g_ref, o_ref):
    y = _dot(a_ref[...], w_ref[...])
    o_ref[...] = (_sigmoid(g_ref[...].astype(f32)) * y).astype(o_ref.dtype)


def _glu_mix_kernel(s_ref, wa_ref, wb_ref, g_ref, ya_ref, o_ref):
    s = s_ref[...]
    ga = _dot(s, wa_ref[...])
    gb = _dot(s, wb_ref[...])
    yb = ga * _sigmoid(gb)
    o_ref[...] = (ya_ref[...].astype(f32) + _sigmoid(g_ref[...].astype(f32)) * yb).astype(o_ref.dtype)


def _out_proj_kernel(m_ref, w_ref, x_ref, o_ref):
    o_ref[...] = x_ref[...] + _dot(m_ref[...], w_ref[...])


def _peer_query_kernel(h_ref, g_ref, wq_ref, kk_ref, hn_ref, st_ref):
    x = h_ref[...]
    ms = jnp.mean(x * x, axis=-1, keepdims=True)
    hn_f = x * lax.rsqrt(ms + EPS) * g_ref[...]
    hn_ref[...] = hn_f.T.astype(bf16)
    q = _dot(hn_f.astype(bf16), wq_ref[...]).astype(bf16)
    n_heads = kk_ref.shape[0]
    half = kk_ref.shape[3]
    for h in range(n_heads):
        for side in range(2):
            c0 = (2 * h + side) * half
            st_ref[h, side] = _dot_nt(kk_ref[h, side], q[:, c0:c0 + half])


def _peer_query(h1, g, wq, kk, tm):
    t, d = h1.shape
    nh, _, nk, half = kk.shape
    return pl.pallas_call(
        _peer_query_kernel,
        grid=(t // tm,),
        in_specs=[pl.BlockSpec((tm, d), lambda i: (i, 0)),
                  pl.BlockSpec((1, d), lambda i: (0, 0)),
                  pl.BlockSpec(wq.shape, lambda i: (0, 0)),
                  pl.BlockSpec(kk.shape, lambda i: (0, 0, 0, 0))],
        out_specs=[pl.BlockSpec((d, tm), lambda i: (0, i)),
                   pl.BlockSpec((nh, 2, nk, tm), lambda i: (0, 0, 0, i))],
        out_shape=[jax.ShapeDtypeStruct((d, t), bf16), jax.ShapeDtypeStruct((nh, 2, nk, t), f32)],
        compiler_params=_cparams(("parallel",)),
        name="peer_query",
    )(h1, g, wq, kk)


def _sort_pairs(n):
    pairs = []
    p = 1
    while p < n:
        k = p
        while k >= 1:
            for j in range(k % p, n - k, 2 * k):
                for i in range(min(k, n - j - k)):
                    if (i + j) // (2 * p) == (i + j + k) // (2 * p):
                        pairs.append((i + j, i + j + k))
            k //= 2
        p *= 2
    return pairs


def _top_desc(tiles, n):
    m = len(tiles)
    vs = list(tiles)
    for i, j in _sort_pairs(1 << (m - 1).bit_length()):
        if j < m:
            vs[i], vs[j] = jnp.maximum(vs[i], vs[j]), jnp.minimum(vs[i], vs[j])
    neg = jnp.full(vs[0].shape, -jnp.inf, f32)
    vs = vs[:n] + [neg]
    vals = []
    for a in range(n):
        v = jnp.max(vs[0], axis=0, keepdims=True)
        vals.append(v)
        hit = vs[0] == v
        keep = min(len(vs) - 1, n - a - 1)
        vs = [jnp.where(hit, vs[k + 1], vs[k]) for k in range(keep)] + [neg]
    return _stack_rows(vals)


def _peer_topk_kernel(st_ref, r2_ref, e2_ref, n1_ref, e1_ref):
    chunk = LANES
    sub8 = lambda x: [x[k * SUBLANES:(k + 1) * SUBLANES] for k in range(x.shape[0] // SUBLANES)]
    for c in range(st_ref.shape[3] // chunk):
        cols = slice(c * chunk, (c + 1) * chunk)
        s1 = st_ref[0, 0, :, cols]
        s2 = st_ref[0, 1, :, cols]
        v1 = _top_desc(sub8(s1), TOPK)
        v2 = _top_desc(sub8(s2), TOPK)
        half = TOPK // 2
        cand = ([v1[0:1] + v2[:half], v1[0:1] + v2[half:]] + [v1[a:a + 1] + v2[:half] for a in range(1, half)]
                + [v1[half:] + v2[0:1]])
        best = _top_desc(cand, TOPK)
        tau = best[TOPK - 1:TOPK]
        z = jnp.sum(jnp.exp(best - best[0:1]), axis=0, keepdims=True)
        n1 = jnp.zeros(s1.shape, f32)
        rank2 = jnp.zeros(s2.shape, f32)
        for a in range(TOPK):
            cnt = jnp.sum(jnp.where((v1[a:a + 1] + v2) >= tau, 1.0, 0.0), axis=0, keepdims=True)
            n1 = jnp.where(s1 == v1[a:a + 1], cnt, n1)
            rank2 = jnp.where(s2 < v2[a:a + 1], a + 1.0, rank2)
        r2_ref[0, :, cols] = rank2.astype(bf16)
        e2_ref[0, :, cols] = jnp.exp(s2 - v2[0:1]).astype(bf16)
        n1_ref[0, :, cols] = n1
        e1_ref[0, :, cols] = jnp.exp(s1 - v1[0:1]) / z


def _peer_topk(st, tl):
    nh, _, nk, t = st.shape
    spec = lambda: pl.BlockSpec((1, nk, tl), lambda h, i: (h, 0, i))
    shp = lambda dt: jax.ShapeDtypeStruct((nh, nk, t), dt)
    return pl.pallas_call(
        _peer_topk_kernel,
        grid=(nh, t // tl),
        in_specs=[pl.BlockSpec((1, 2, nk, tl), lambda h, i: (h, 0, 0, i))],
        out_specs=[spec(), spec(), spec(), spec()],
        out_shape=[shp(bf16), shp(bf16), shp(f32), shp(f32)],
        compiler_params=_cparams(("parallel", "parallel")),
        name="peer_topk",
    )(st)


def _peer_dense_kernel(hn_ref, u_ref, vt_ref, r2_ref, e2_ref, n1_ref, e1_ref, h1_ref, o_ref,
                       acc_scr, a0_scr, a1_scr, *, nb, tn):
    e = pl.program_id(1)
    n_heads, nk, tm = r2_ref.shape
    eb = u_ref.shape[0]
    pk = 2 * SUBLANES

    @pl.when(e == 0)
    def _():
        acc_scr[...] = jnp.zeros_like(acc_scr)
        a1_scr[...] = jnp.zeros_like(a1_scr)

    def step(a_cur, a_prev):
        a_cur[...] = _dot(u_ref[...], hn_ref[...])
        blk = jnp.maximum(e - 1, 0)
        for c in range(tm // tn):
            cols = slice(c * tn, (c + 1) * tn)
            w_rows = []
            for ii in range(eb // nk):
                i1 = blk * (eb // nk) + ii
                gate = None
                for h in range(n_heads):
                    n1 = jnp.broadcast_to(n1_ref[h, pl.ds(i1, 1), cols], (pk, tn)).astype(bf16)
                    e1 = jnp.broadcast_to(e1_ref[h, pl.ds(i1, 1), cols], (pk, tn)).astype(bf16)
                    r2 = r2_ref[h, :, cols].reshape(nk // pk, pk, tn)
                    e2 = e2_ref[h, :, cols].reshape(nk // pk, pk, tn)
                    term = jnp.where(r2 < n1[None], e2, jnp.zeros_like(e2)) * e1[None]
                    gate = term if gate is None else gate + term
                g = _gelu(a_prev[ii * nk:(ii + 1) * nk, cols]).astype(bf16)
                w_rows.append(gate.reshape(nk, tn) * g)
            acc_scr[:, cols] += _dot(vt_ref[...], jnp.concatenate(w_rows, axis=0))

    @pl.when(e % 2 == 0)
    def _():
        step(a0_scr, a1_scr)

    @pl.when(e % 2 == 1)
    def _():
        step(a1_scr, a0_scr)

    @pl.when(e == nb)
    def _():
        o_ref[...] = h1_ref[...] + acc_scr[...].T


def _peer_dense(hn_t, u_b, vt_b, r2, e2, n1, e1, h1, tm, eb):
    d, t = hn_t.shape
    nb = u_b.shape[0] // eb
    nh, nk, _ = r2.shape
    tok = lambda: pl.BlockSpec((nh, nk, tm), lambda i, e: (0, 0, i), pipeline_mode=pl.Buffered(1))
    return pl.pallas_call(
        functools.partial(_peer_dense_kernel, nb=nb, tn=_tile(tm, 2 * LANES)),
        grid=(t // tm, nb + 1),
        in_specs=[pl.BlockSpec((d, tm), lambda i, e: (0, i)),
                  pl.BlockSpec((eb, d), lambda i, e: (jnp.minimum(e, nb - 1), 0)),
                  pl.BlockSpec((d, eb), lambda i, e: (0, jnp.maximum(e - 1, 0))),
                  tok(), tok(), tok(), tok(),
                  pl.BlockSpec((tm, d), lambda i, e: (i, 0), pipeline_mode=pl.Buffered(1))],
        out_specs=pl.BlockSpec((tm, d), lambda i, e: (i, 0)),
        out_shape=jax.ShapeDtypeStruct((t, d), f32),
        scratch_shapes=[pltpu.VMEM((d, tm), f32), pltpu.VMEM((eb, tm), f32), pltpu.VMEM((eb, tm), f32)],
        compiler_params=_cparams(("parallel", "arbitrary")),
        name="peer_dense",
    )(hn_t, u_b, vt_b, r2, e2, n1, e1, h1)


def _tile(n, pref):
    if n <= pref:
        return n
    t = pref // LANES * LANES
    while n % t:
        t -= LANES
    assert t > 0, (n, pref)
    return t


def kernel(x, meta_tokens, norm1_g, w_in, q_norm_g, k_norm_g, lambda_q1, lambda_k1, lambda_q2, lambda_k2, subln_g, w_attn_branch, ssm_a_re, ssm_a_im, ssm_log_dt, ssm_b_re, ssm_b_im, ssm_c_re, ssm_c_im, ssm_d, w_glu, w_out, norm2_g, peer_w_q, peer_k1, peer_k2, peer_u, peer_v):
    assert norm1_g.shape[0] == 1, "single-layer block only"
    batch, seq, d = x.shape
    n_meta = meta_tokens.shape[0]
    sub = q_norm_g.shape[-1]
    aw = w_attn_branch.shape[1]
    heads = aw // (2 * sub)
    n_groups, n_state, n_chan = ssm_b_re.shape[1:]
    sw = n_groups * n_chan
    assert 2 * sub == LANES and LANES % n_chan == 0 and sw % LANES == 0
    assert n_meta % SUBLANES == 0 and n_meta <= LANES
    t = batch * seq
    n_in = w_in.shape[2]
    assert n_in == 3 * aw + sw + 2 * d

    x2 = x.reshape(t, d)
    w_in_b = w_in[0].astype(bf16)

    proj = _rms_matmul(x2, norm1_g, w_in_b, _tile(t, 1024), _tile(n_in, 1024), bf16, "in_proj")
    proj_m = _rms_matmul(meta_tokens, norm1_g, w_in_b, n_meta, _tile(n_in, 1024), bf16, "in_proj_meta")

    half = sub // 2
    inv_freq = jnp.power(ROPE_THETA, -jnp.arange(half, dtype=f32) / half)
    ang = jnp.arange(n_meta + seq, dtype=f32)[:, None] * inv_freq[None, :]
    cos_t = jnp.tile(jnp.cos(ang), (1, LANES // half))
    sin_t = jnp.tile(jnp.concatenate([-jnp.sin(ang), jnp.sin(ang)], axis=1), (1, LANES // sub))
    lane = jnp.arange(LANES)
    jmat = (lane[:, None] // sub == lane[None, :] // sub).astype(bf16)
    gq = jnp.tile(q_norm_g, (1, aw // sub))
    gk = jnp.tile(k_norm_g, (1, aw // sub))
    tr = _tile(seq, 512)
    q_rot = _normrot(proj, 0, aw, gq, cos_t[n_meta:], sin_t[n_meta:], jmat, tr, sub ** -0.5 * math.log2(math.e), sub,
                      "q_normrot")
    k_rot = _normrot(proj, 1, aw, gk, cos_t[n_meta:], sin_t[n_meta:], jmat, tr, 1.0, sub, "k_normrot")
    km_rot = _normrot(proj_m, 1, aw, gk, cos_t[:n_meta], sin_t[:n_meta], jmat, n_meta, 1.0, sub, "k_normrot_meta")
    km = jnp.pad(km_rot, ((0, LANES - n_meta), (0, 0)))
    vm = jnp.pad(proj_m[:, 2 * aw:3 * aw], ((0, LANES - n_meta), (0, 0)))

    att = _attention(q_rot, k_rot, proj, 2 * aw // LANES, km, vm, lambda_q1, lambda_k1, lambda_q2, lambda_k2,
                     subln_g, batch, seq, heads, _tile(seq, 512), n_meta, sub)

    blk = SSM_BLOCK
    gpt = LANES // n_chan
    ngt = sw // LANES
    assert 2 * n_state == LANES and seq % (blk * SUBLANES) == 0 and n_meta % blk == 0
    grp = lambda a: a.reshape(ngt, gpt, n_state)
    row_l = lambda a: jnp.tile(jnp.repeat(grp(a), n_chan, axis=1), (1, 1, 2))
    col_l = lambda a: jnp.repeat(grp(a).transpose(0, 2, 1), n_chan, axis=2)
    scan_l = lambda a: jnp.tile(grp(a), (1, 1, 2)).reshape(ngt, 1, gpt * LANES)
    ldt2 = jnp.broadcast_to(ssm_log_dt[0][:, None], (n_groups, n_state))
    lays = lambda f: (f(ssm_a_re[0]), f(ssm_a_im[0]), f(ldt2))
    b_l = lambda b: jnp.tile(b[0].transpose(0, 2, 1).reshape(ngt, LANES, n_state), (1, 1, 2))
    c_l = lambda c: c[0].reshape(ngt, gpt, n_chan, n_state).transpose(0, 3, 1, 2).reshape(ngt, n_state, LANES)
    prep = _ssm_prep(lays(row_l), lays(col_l), lays(scan_l), (b_l(ssm_b_re), b_l(ssm_b_im)),
                     (c_l(ssm_c_re), c_l(ssm_c_im)), n_state, n_chan)
    u0 = 3 * aw
    ys = _ssm(proj, u0 // LANES, proj_m[:, u0:u0 + sw], prep, ssm_d, seq // blk, n_state)

    tm = _tile(t, 1024)
    tn = _tile(d, 1024)
    nj = d // tn
    ga_col0 = (3 * aw + sw) // tn
    gb_col0 = (3 * aw + sw + d) // tn
    grid = (t // tm, nj)
    sem = _cparams(("parallel", "arbitrary"))
    row_full = lambda k: pl.BlockSpec((tm, k), lambda i, j: (i, 0))
    w_col = lambda k, off=0: pl.BlockSpec((k, tn), lambda i, j: (0, j + off))
    out_tile = pl.BlockSpec((tm, tn), lambda i, j: (i, j))
    yag = pl.pallas_call(
        _attn_out_kernel, grid=grid,
        in_specs=[row_full(aw), w_col(aw), pl.BlockSpec((tm, tn), lambda i, j: (i, ga_col0 + j))],
        out_specs=out_tile, out_shape=jax.ShapeDtypeStruct((t, d), bf16), compiler_params=sem, name="attn_out",
    )(att, w_attn_branch[0].astype(bf16), proj)
    w_glu_b = w_glu[0].astype(bf16)
    mix = pl.pallas_call(
        _glu_mix_kernel, grid=grid,
        in_specs=[row_full(sw), w_col(sw), w_col(sw, nj), pl.BlockSpec((tm, tn), lambda i, j: (i, gb_col0 + j)),
                  out_tile],
        out_specs=out_tile, out_shape=jax.ShapeDtypeStruct((t, d), bf16), compiler_params=sem, name="glu_mix",
    )(ys, w_glu_b, w_glu_b, proj, yag)
    h1 = pl.pallas_call(
        _out_proj_kernel, grid=grid,
        in_specs=[row_full(d), w_col(d), out_tile],
        out_specs=out_tile, out_shape=jax.ShapeDtypeStruct((t, d), f32), compiler_params=sem, name="out_proj",
    )(mix, w_out[0].astype(bf16), x2)

    kk = jnp.stack([peer_k1[0], peer_k2[0]], axis=1).astype(bf16)
    assert kk.shape[2] == LANES and kk.shape[3] == LANES
    hn2, st = _peer_query(h1, norm2_g, peer_w_q[0].astype(bf16), kk, _tile(t, 512))
    r2, e2, n1, e1 = _peer_topk(st, _tile(t, 512))
    u_b = peer_u[0].astype(bf16)
    vt_b = peer_v[0].T.astype(bf16)
    out = _peer_dense(hn2, u_b, vt_b, r2, e2, n1, e1, h1, _tile(t, 512), _tile(u_b.shape[0], 1024))
    return out.reshape(batch, seq, d)
```

```python
import functools
import math

import jax
import jax.numpy as jnp
from jax import lax
from jax.experimental import pallas as pl
from jax.experimental.pallas import tpu as pltpu

EPS = 1e-6
ROPE_THETA = 10000.0
LAM_INIT = 0.8 - 0.6 * math.exp(-0.3 * 0)
TOPK = 16
LANES = 128
SUBLANES = 8
NEG_BIG = -1e30
VMEM_LIMIT = 56 * 1024 * 1024

bf16 = jnp.bfloat16
f32 = jnp.float32


def _cparams(sem):
    return pltpu.CompilerParams(dimension_semantics=sem, vmem_limit_bytes=VMEM_LIMIT)


def _dot(a, b):
    return jnp.dot(a, b, preferred_element_type=f32)


def _dot_nt(a, b):
    return lax.dot_general(a, b, (((1,), (1,)), ((), ())), preferred_element_type=f32)


def _sigmoid(x):
    return 1.0 / (1.0 + jnp.exp(-x))


def _gelu(x):
    return 0.5 * x * (1.0 + lax.erf(x * (1.0 / math.sqrt(2.0))))


def _stack_rows(rows):
    k = len(rows)
    idx = lax.broadcasted_iota(jnp.int32, (k, rows[0].shape[1]), 0)
    out = jnp.broadcast_to(rows[0], idx.shape)
    for j in range(1, k):
        out = jnp.where(idx == j, rows[j], out)
    return out


def _rms_matmul_kernel(x_ref, g_ref, w_ref, o_ref, xn_ref):
    @pl.when(pl.program_id(1) == 0)
    def _():
        x = x_ref[...]
        ms = jnp.mean(x * x, axis=-1, keepdims=True)
        xn_ref[...] = (x * lax.rsqrt(ms + EPS) * g_ref[...]).astype(bf16)

    o_ref[...] = _dot(xn_ref[...], w_ref[...]).astype(o_ref.dtype)


def _rms_matmul(x, g, w, tm, tn, out_dtype, name):
    m, k = x.shape
    n = w.shape[1]
    return pl.pallas_call(
        _rms_matmul_kernel,
        grid=(m // tm, n // tn),
        in_specs=[pl.BlockSpec((tm, k), lambda i, j: (i, 0)),
                  pl.BlockSpec((1, k), lambda i, j: (0, 0)),
                  pl.BlockSpec((k, tn), lambda i, j: (0, j))],
        out_specs=pl.BlockSpec((tm, tn), lambda i, j: (i, j)),
        out_shape=jax.ShapeDtypeStruct((m, n), out_dtype),
        scratch_shapes=[pltpu.VMEM((tm, k), bf16)],
        compiler_params=_cparams(("parallel", "arbitrary")),
        name=name,
    )(x, g, w)


def _normrot_kernel(x_ref, g_ref, cos_ref, sin_ref, j_ref, o_ref, *, scale, sub):
    width = x_ref.shape[1]
    cos = cos_ref[...]
    sin = sin_ref[...]
    jmat = j_ref[...]
    lane = lax.broadcasted_iota(jnp.int32, cos.shape, 1)
    first_half = (lane % sub) < (sub // 2)
    for t in range(width // LANES):
        sl = slice(t * LANES, (t + 1) * LANES)
        x = x_ref[:, sl].astype(f32)
        x2 = x * x
        hi = x2.astype(bf16)
        lo = (x2 - hi.astype(f32)).astype(bf16)
        ssq = _dot(hi, jmat) + _dot(lo, jmat)
        xn = x * lax.rsqrt(ssq * (1.0 / sub) + EPS) * g_ref[:, sl]
        partner = jnp.where(first_half, pltpu.roll(xn, LANES - sub // 2, 1), pltpu.roll(xn, sub // 2, 1))
        o_ref[:, sl] = ((xn * cos + partner * sin) * scale).astype(o_ref.dtype)


def _normrot(x, col_block, width, g_t, cos, sin, jmat, tm, scale, sub, name):
    m = x.shape[0]
    nt = cos.shape[0] // tm
    return pl.pallas_call(
        functools.partial(_normrot_kernel, scale=scale, sub=sub),
        grid=(m // tm,),
        in_specs=[pl.BlockSpec((tm, width), lambda i: (i, col_block)),
                  pl.BlockSpec((1, width), lambda i: (0, 0)),
                  pl.BlockSpec((tm, LANES), lambda i: (i % nt, 0)),
                  pl.BlockSpec((tm, LANES), lambda i: (i % nt, 0)),
                  pl.BlockSpec((LANES, LANES), lambda i: (0, 0))],
        out_specs=pl.BlockSpec((tm, width), lambda i: (i, 0)),
        out_shape=jax.ShapeDtypeStruct((m, width), bf16),
        compiler_params=_cparams(("parallel",)),
        name=name,
    )(x, g_t, cos, sin, jmat)


def _attn_kernel(q_ref, k_ref, v_ref, km_ref, vm_ref, lq1_ref, lk1_ref, lq2_ref, lk2_ref, sg_ref,
                 o_ref, qq_scr, m_scr, acc_scr, s_scr, *, n_meta, sub, hpb):
    i = pl.program_id(2)
    tq = q_ref.shape[0]
    tk = tq
    hw = 2 * sub
    mp = km_ref.shape[0]
    lane = lax.broadcasted_iota(jnp.int32, (tq, hw), 1)
    head = lambda hh: slice(hh * hw, (hh + 1) * hw)

    def softmax_step(hh, s, v_aug, first):
        smax = jnp.max(s, axis=1, keepdims=True)
        if first:
            m_new = jnp.broadcast_to(smax, (2 * tq, hw))
        else:
            m_prev = m_scr[hh]
            m_new = jnp.maximum(m_prev, smax)
        p = jnp.concatenate([jnp.exp2(s[:, c * hw:(c + 1) * hw] - m_new) for c in range(s.shape[1] // hw)], axis=1)
        pv = _dot(p.astype(bf16), v_aug)
        if first:
            acc_scr[hh] = pv
        else:
            alpha = jnp.exp2(m_prev - m_new)
            acc_scr[hh] = jnp.concatenate([alpha, alpha], axis=1) * acc_scr[hh] + pv
        m_scr[hh] = m_new

    for hh in range(hpb):
        q = q_ref[:, head(hh)]
        zero = jnp.zeros_like(q)
        qq_scr[hh] = jnp.concatenate([jnp.where(lane < sub, q, zero), jnp.where(lane >= sub, q, zero)], axis=0)

    def put_scores(j, slot):
        start = pl.multiple_of(j * tk, tk)
        for hh in range(hpb):
            s_scr[slot, hh] = _dot_nt(qq_scr[hh], k_ref[pl.ds(start, tk), head(hh)])

    def consume(j, slot, masked):
        start = pl.multiple_of(j * tk, tk)
        for hh in range(hpb):
            s = s_scr[slot, hh]
            if masked:
                row = lax.broadcasted_iota(jnp.int32, s.shape, 0) % tq
                col = lax.broadcasted_iota(jnp.int32, s.shape, 1)
                s = jnp.where(col <= row, s, NEG_BIG)
            vb = v_ref[pl.ds(start, tk), head(hh)]
            softmax_step(hh, s, jnp.concatenate([vb, jnp.ones((tk, hw), bf16)], axis=1), False)

    def stage(j, slot):
        put_scores(j + 1, 1 - slot)
        consume(j, slot, False)

    put_scores(0, 0)

    for hh in range(hpb):
        s = _dot_nt(qq_scr[hh], km_ref[:, head(hh)])
        col = lax.broadcasted_iota(jnp.int32, s.shape, 1)
        s = jnp.where(col < n_meta, s, NEG_BIG)
        softmax_step(hh, s, jnp.concatenate([vm_ref[:, head(hh)], jnp.ones((mp, hw), bf16)], axis=1), True)

    def body(p, c):
        stage(2 * p, 0)
        stage(2 * p + 1, 1)
        return c

    lax.fori_loop(0, i // 2, body, 0)

    @pl.when(i % 2 == 0)
    def _():
        consume(i, 0, True)

    @pl.when(i % 2 == 1)
    def _():
        stage(i - 1, 0)
        consume(i, 1, True)

    lam = (jnp.exp(jnp.sum(lq1_ref[...] * lk1_ref[...], axis=1, keepdims=True))
           - jnp.exp(jnp.sum(lq2_ref[...] * lk2_ref[...], axis=1, keepdims=True)) + LAM_INIT)
    for hh in range(hpb):
        acc = acc_scr[hh]
        o = acc[:, :hw] / acc[:, hw:]
        att = o[:tq] - lam * o[tq:]
        ms = jnp.mean(att * att, axis=-1, keepdims=True)
        att = att * lax.rsqrt(ms + EPS) * sg_ref[...] * (1.0 - LAM_INIT)
        o_ref[:, head(hh)] = att.astype(o_ref.dtype)


def _attention(q_rot, k_rot, proj, v_col0, km, vm, lq1, lk1, lq2, lk2, sg, batch, seq, heads, tq, n_meta, sub):
    t, aw = q_rot.shape
    nq = seq // tq
    hw = 2 * sub
    hpb = 2 if heads % 2 == 0 else 1
    bw = hpb * hw
    mp = km.shape[0]
    vec = lambda: pl.BlockSpec((1, sub), lambda b, h, i: (0, 0))
    return pl.pallas_call(
        functools.partial(_attn_kernel, n_meta=n_meta, sub=sub, hpb=hpb),
        grid=(batch, heads // hpb, nq),
        in_specs=[pl.BlockSpec((tq, bw), lambda b, h, i: (b * nq + i, h)),
                  pl.BlockSpec((seq, bw), lambda b, h, i: (b, h)),
                  pl.BlockSpec((seq, bw), lambda b, h, i: (b, v_col0 // hpb + h)),
                  pl.BlockSpec((mp, bw), lambda b, h, i: (0, h)),
                  pl.BlockSpec((mp, bw), lambda b, h, i: (0, h)),
                  vec(), vec(), vec(), vec(),
                  pl.BlockSpec((1, hw), lambda b, h, i: (0, 0))],
        out_specs=pl.BlockSpec((tq, bw), lambda b, h, i: (b * nq + i, h)),
        out_shape=jax.ShapeDtypeStruct((t, aw), bf16),
        scratch_shapes=[pltpu.VMEM((hpb, 2 * tq, hw), bf16), pltpu.VMEM((hpb, 2 * tq, hw), f32),
                        pltpu.VMEM((hpb, 2 * tq, 2 * hw), f32), pltpu.VMEM((2, hpb, 2 * tq, tq), f32)],
        compiler_params=_cparams(("parallel", "parallel", "arbitrary")),
        name="diff_attention",
    )(q_rot, k_rot, proj, km, vm, lq1, lk1, lq2, lk2, sg)


SSM_BLOCK = 8


def _cmul(ar, ai, br, bi):
    return ar * br - ai * bi, ar * bi + ai * br


def _dot3(a, b):
    ah = a.astype(bf16)
    al = (a - ah.astype(f32)).astype(bf16)
    bh = b.astype(bf16)
    bl = (b - bh.astype(f32)).astype(bf16)
    return _dot(ah, bh) + (_dot(ah, bl) + _dot(al, bh))


def _abar(a_re, a_im, log_dt):
    dt = jnp.exp(log_dt)
    er = jnp.exp(a_re * dt)
    return er * jnp.cos(a_im * dt), er * jnp.sin(a_im * dt)


def _powers(ar, ai, n):
    out = [(jnp.ones_like(ar), jnp.zeros_like(ai)), (ar, ai)]
    for _ in range(2, n + 1):
        out.append(_cmul(*out[-1], ar, ai))
    return out


def _ssm_prep_kernel(ar_ref, ai_ref, ld_ref, ac_ref, aic_ref, ldc_ref, as_ref, ais_ref, lds_ref,
                     br_ref, bi_ref, cr_ref, ci_ref, w_ref, q_ref, k_ref, pr_ref, ps_ref, *, n_state, n_chan):
    blk = SSM_BLOCK
    gpt = LANES // n_chan
    a_re, a_im = ar_ref[0], ai_ref[0]
    abr, abi = _abar(a_re, a_im, ld_ref[0])
    den = a_re * a_re + a_im * a_im
    nr, ni = abr - 1.0, abi
    f_re = (nr * a_re + ni * a_im) / den
    f_im = (ni * a_re - nr * a_im) / den
    bbr, bbi = _cmul(f_re, f_im, br_ref[0], bi_ref[0])
    pw = _powers(abr, abi, blk - 1)
    is_re = lax.broadcasted_iota(jnp.int32, a_re.shape, 1) < n_state
    rgrp = lax.broadcasted_iota(jnp.int32, a_re.shape, 0) // n_chan
    wf = []
    for i in range(blk):
        xr, xi = _cmul(bbr, bbi, *pw[blk - 1 - i])
        tile = jnp.where(is_re, xr, xi)
        wf.append(jnp.concatenate([jnp.where(rgrp == g, tile, 0.0) for g in range(gpt)], axis=1))
        w_ref[0, i] = wf[i].astype(bf16)
    c_re, c_im = cr_ref[0], ci_ref[0]
    acr, aci = _abar(ac_ref[0], aic_ref[0], ldc_ref[0])
    pwc = _powers(acr, aci, blk)
    cgrp = lax.broadcasted_iota(jnp.int32, c_re.shape, 1) // n_chan

    def by_group(tr, ti):
        rows = []
        for g in range(gpt):
            rows += [jnp.where(cgrp == g, tr, 0.0), jnp.where(cgrp == g, ti, 0.0)]
        return jnp.concatenate(rows, axis=0)

    for j in range(blk):
        qr, qi = _cmul(c_re, c_im, *pwc[j + 1])
        q_ref[0, :, j * LANES:(j + 1) * LANES] = by_group(qr, -qi).astype(bf16)
    ccat = by_group(c_re, -c_im)
    kd = [_dot3(wf[blk - 1 - dd], ccat).astype(bf16) for dd in range(blk)]
    for i in range(blk):
        for j in range(blk):
            tile = kd[j - i] if j >= i else jnp.zeros((LANES, LANES), bf16)
            k_ref[0, i * LANES:(i + 1) * LANES, j * LANES:(j + 1) * LANES] = tile
    asr, asi = _abar(as_ref[0], ais_ref[0], lds_ref[0])
    a8 = _powers(asr, asi, blk)[blk]
    pl_ = _powers(a8[0], a8[1], SUBLANES)[1:]
    re_lane = lax.broadcasted_iota(jnp.int32, asr.shape, 1) % LANES < n_state
    pr_ref[0] = _stack_rows([p[0] for p in pl_])
    ps_ref[0] = _stack_rows([jnp.where(re_lane, -p[1], p[1]) for p in pl_])


def _ssm_prep(rows, cols, scan, b_t, c_t, n_state, n_chan):
    ngt = b_t[0].shape[0]
    blk = SSM_BLOCK
    sl = scan[0].shape[2]
    spec3 = lambda a: pl.BlockSpec((1,) + a.shape[1:], lambda g: (g, 0, 0))
    ins = list(rows) + list(cols) + list(scan) + list(b_t) + list(c_t)
    w_s = jax.ShapeDtypeStruct((ngt, blk, LANES, sl), bf16)
    q_s = jax.ShapeDtypeStruct((ngt, sl, blk * LANES), bf16)
    k_s = jax.ShapeDtypeStruct((ngt, blk * LANES, blk * LANES), bf16)
    t_s = jax.ShapeDtypeStruct((ngt, SUBLANES, sl), f32)
    spec4 = lambda a: pl.BlockSpec((1,) + a.shape[1:], lambda g: (g, 0, 0, 0))
    return pl.pallas_call(
        functools.partial(_ssm_prep_kernel, n_state=n_state, n_chan=n_chan),
        grid=(ngt,),
        in_specs=[spec3(a) for a in ins],
        out_specs=[spec4(w_s), spec3(q_s), spec3(k_s), spec3(t_s), spec3(t_s)],
        out_shape=[w_s, q_s, k_s, t_s, t_s],
        compiler_params=_cparams(("parallel",)),
        name="ssm_prep",
    )(*ins)


def _ssm_kernel(u_ref, um_ref, w_ref, q_ref, k_ref, pr_ref, ps_ref, d_ref, y_ref,
                uf_scr, umf_scr, s_scr, sw_scr, xp_scr, yf_scr, *, rows_per_seq, meta_rows, n_state):
    blk = SSM_BLOCK
    r = u_ref.shape[0] // blk
    sl = s_scr.shape[1]
    tiles = sl // LANES
    pr, ps = pr_ref[0], ps_ref[0]

    def swap(x):
        return jnp.concatenate([pltpu.roll(x[:, c * LANES:(c + 1) * LANES], n_state, 1) for c in range(tiles)],
                               axis=1)

    uf_scr[...] = u_ref[...].astype(f32)
    umf_scr[...] = jnp.zeros_like(umf_scr)
    umf_scr[0:um_ref.shape[0], :] = um_ref[...].astype(f32)
    step_rows = lambda i: uf_scr[pl.ds(i, r, stride=blk), :]
    ucat = jnp.concatenate([step_rows(i).astype(bf16) for i in range(blk)], axis=1)
    umcat = jnp.concatenate([umf_scr[pl.ds(i, SUBLANES, stride=blk), :].astype(bf16) for i in range(blk)], axis=1)
    s = _dot(ucat, w_ref[0])
    sm = _dot(umcat, w_ref[0])
    rowmod = lax.broadcasted_iota(jnp.int32, s.shape, 0) % SUBLANES
    for k in (1, 2, 4):
        t = pltpu.roll(s, k, 0)
        t = t * pr[k - 1:k] + swap(t) * ps[k - 1:k]
        s = s + jnp.where(rowmod >= k, t, 0.0)
    s_scr[...] = s
    sw_scr[...] = swap(s)
    x0 = sm[0:1]
    for mrow in range(1, meta_rows):
        x0 = x0 * pr[0:1] + swap(x0) * ps[0:1] + sm[mrow:mrow + 1]
    x0w = swap(x0)
    row8 = lax.broadcasted_iota(jnp.int32, (SUBLANES, sl), 0)

    def body(b, carry):
        start = b % (rows_per_seq // SUBLANES) == 0
        rows = pl.ds(pl.multiple_of(b * SUBLANES, SUBLANES), SUBLANES)
        c = jnp.broadcast_to(jnp.where(start, x0, carry[0]), (SUBLANES, sl))
        cw = jnp.broadcast_to(jnp.where(start, x0w, carry[1]), (SUBLANES, sl))
        xs = s_scr[rows, :] + (c * pr + cw * ps)
        xw = sw_scr[rows, :] + (cw * pr - c * ps)
        xp_scr[rows, :] = jnp.where(row8 == 0, c, pltpu.roll(xs, 1, 0))
        return xs[SUBLANES - 1:SUBLANES], xw[SUBLANES - 1:SUBLANES]

    lax.fori_loop(0, r // SUBLANES, body, (x0, x0w))

    y_all = _dot(xp_scr[...].astype(bf16), q_ref[0]) + _dot(ucat, k_ref[0])
    for j in range(blk):
        y = y_all[:, j * LANES:(j + 1) * LANES] + d_ref[...] * step_rows(j)
        yf_scr[pl.ds(j, r, stride=blk), :] = _gelu(y)
    y_ref[...] = yf_scr[...].astype(y_ref.dtype)


def _ssm(u_arr, u_col0, um, prep, d, rows_per_seq, n_state):
    w, q, k, pr, ps = prep
    ngt, blk, _, sl = w.shape
    w = w.reshape(ngt, blk * LANES, sl)
    t = u_arr.shape[0]
    n_meta = um.shape[0]
    spec3 = lambda a: pl.BlockSpec((1,) + a.shape[1:], lambda g: (g, 0, 0))
    return pl.pallas_call(
        functools.partial(_ssm_kernel, rows_per_seq=rows_per_seq, meta_rows=n_meta // blk, n_state=n_state),
        grid=(ngt,),
        in_specs=[pl.BlockSpec((t, LANES), lambda g: (0, u_col0 + g)),
                  pl.BlockSpec((n_meta, LANES), lambda g: (0, g)),
                  spec3(w), spec3(q), spec3(k), spec3(pr), spec3(ps),
                  pl.BlockSpec((1, LANES), lambda g: (0, g))],
        out_specs=pl.BlockSpec((t, LANES), lambda g: (0, g)),
        out_shape=jax.ShapeDtypeStruct((t, ngt * LANES), bf16),
        scratch_shapes=[pltpu.VMEM((t, LANES), f32), pltpu.VMEM((SUBLANES * blk, LANES), f32),
                        pltpu.VMEM((t // blk, sl), f32), pltpu.VMEM((t // blk, sl), f32),
                        pltpu.VMEM((t // blk, sl), f32), pltpu.VMEM((t, LANES), f32)],
        compiler_params=_cparams(("parallel",)),
        name="ssm_blocked",
    )(u_arr, um, w, q, k, pr, ps, d)


def _attn_out_kernel(a_ref, w_ref, g_ref, o_ref):
    y = _dot(a_ref[...], w_ref[...])
    o_ref[...] = (_sigmoid(g_ref[...].astype(f32)) * y).astype(o_ref.dtype)


def _glu_mix_kernel(s_ref, wa_ref, wb_ref, g_ref, ya_ref, o_ref):
    s = s_ref[...]
    ga = _dot(s, wa_ref[...])
    gb = _dot(s, wb_ref[...])
    yb = ga * _sigmoid(gb)
    o_ref[...] = (ya_ref[...].astype(f32) + _sigmoid(g_ref[...].astype(f32)) * yb).astype(o_ref.dtype)


def _out_proj_kernel(m_ref, w_ref, x_ref, o_ref):
    o_ref[...] = x_ref[...] + _dot(m_ref[...], w_ref[...])


def _peer_query_kernel(h_ref, g_ref, wq_ref, kk_ref, hn_ref, st_ref):
    x = h_ref[...]
    ms = jnp.mean(x * x, axis=-1, keepdims=True)
    hn_f = x * lax.rsqrt(ms + EPS) * g_ref[...]
    hn_ref[...] = hn_f.T.astype(bf16)
    q = _dot(hn_f.astype(bf16), wq_ref[...]).astype(bf16)
    n_heads = kk_ref.shape[0]
    half = kk_ref.shape[3]
    for h in range(n_heads):
        for side in range(2):
            c0 = (2 * h + side) * half
            st_ref[h, side] = _dot_nt(kk_ref[h, side], q[:, c0:c0 + half])


def _peer_query(h1, g, wq, kk, tm):
    t, d = h1.shape
    nh, _, nk, half = kk.shape
    return pl.pallas_call(
        _peer_query_kernel,
        grid=(t // tm,),
        in_specs=[pl.BlockSpec((tm, d), lambda i: (i, 0)),
                  pl.BlockSpec((1, d), lambda i: (0, 0)),
                  pl.BlockSpec(wq.shape, lambda i: (0, 0)),
                  pl.BlockSpec(kk.shape, lambda i: (0, 0, 0, 0))],
        out_specs=[pl.BlockSpec((d, tm), lambda i: (0, i)),
                   pl.BlockSpec((nh, 2, nk, tm), lambda i: (0, 0, 0, i))],
        out_shape=[jax.ShapeDtypeStruct((d, t), bf16), jax.ShapeDtypeStruct((nh, 2, nk, t), f32)],
        compiler_params=_cparams(("parallel",)),
        name="peer_query",
    )(h1, g, wq, kk)


def _sort_pairs(n):
    pairs = []
    p = 1
    while p < n:
        k = p
        while k >= 1:
            for j in range(k % p, n - k, 2 * k):
                for i in range(min(k, n - j - k)):
                    if (i + j) // (2 * p) == (i + j + k) // (2 * p):
                        pairs.append((i + j, i + j + k))
            k //= 2
        p *= 2
    return pairs


def _top_desc(tiles, n):
    m = len(tiles)
    vs = list(tiles)
    for i, j in _sort_pairs(1 << (m - 1).bit_length()):
        if j < m:
            vs[i], vs[j] = jnp.maximum(vs[i], vs[j]), jnp.minimum(vs[i], vs[j])
    neg = jnp.full(vs[0].shape, -jnp.inf, f32)
    vs = vs[:n] + [neg]
    vals = []
    for a in range(n):
        v = jnp.max(vs[0], axis=0, keepdims=True)
        vals.append(v)
        hit = vs[0] == v
        keep = min(len(vs) - 1, n - a - 1)
        vs = [jnp.where(hit, vs[k + 1], vs[k]) for k in range(keep)] + [neg]
    return _stack_rows(vals)


def _peer_topk_kernel(st_ref, r2_ref, e2_ref, n1_ref, e1_ref):
    chunk = LANES
    sub8 = lambda x: [x[k * SUBLANES:(k + 1) * SUBLANES] for k in range(x.shape[0] // SUBLANES)]
    for c in range(st_ref.shape[3] // chunk):
        cols = slice(c * chunk, (c + 1) * chunk)
        s1 = st_ref[0, 0, :, cols]
        s2 = st_ref[0, 1, :, cols]
        v1 = _top_desc(sub8(s1), TOPK)
        v2 = _top_desc(sub8(s2), TOPK)
        half = TOPK // 2
        cand = ([v1[0:1] + v2[:half], v1[0:1] + v2[half:]] + [v1[a:a + 1] + v2[:half] for a in range(1, half)]
                + [v1[half:] + v2[0:1]])
        best = _top_desc(cand, TOPK)
        tau = best[TOPK - 1:TOPK]
        z = jnp.sum(jnp.exp(best - best[0:1]), axis=0, keepdims=True)
        n1 = jnp.zeros(s1.shape, f32)
        rank2 = jnp.zeros(s2.shape, f32)
        for a in range(TOPK):
            cnt = jnp.sum(jnp.where((v1[a:a + 1] + v2) >= tau, 1.0, 0.0), axis=0, keepdims=True)
            n1 = jnp.where(s1 == v1[a:a + 1], cnt, n1)
            rank2 = jnp.where(s2 < v2[a:a + 1], a + 1.0, rank2)
        r2_ref[0, :, cols] = rank2.astype(bf16)
        e2_ref[0, :, cols] = jnp.exp(s2 - v2[0:1]).astype(bf16)
        n1_ref[0, :, cols] = n1
        e1_ref[0, :, cols] = jnp.exp(s1 - v1[0:1]) / z


def _peer_topk(st, tl):
    nh, _, nk, t = st.shape
    spec = lambda: pl.BlockSpec((1, nk, tl), lambda h, i: (h, 0, i))
    shp = lambda dt: jax.ShapeDtypeStruct((nh, nk, t), dt)
    return pl.pallas_call(
        _peer_topk_kernel,
        grid=(nh, t // tl),
        in_specs=[pl.BlockSpec((1, 2, nk, tl), lambda h, i: (h, 0, 0, i))],
        out_specs=[spec(), spec(), spec(), spec()],
        out_shape=[shp(bf16), shp(bf16), shp(f32), shp(f32)],
        compiler_params=_cparams(("parallel", "parallel")),
        name="peer_topk",
    )(st)


def _peer_dense_kernel(hn_ref, u_ref, vt_ref, r2_ref, e2_ref, n1_ref, e1_ref, h1_ref, o_ref,
                       acc_scr, a0_scr, a1_scr, *, nb, tn):
    e = pl.program_id(1)
    n_heads, nk, tm = r2_ref.shape
    eb = u_ref.shape[0]
    pk = 2 * SUBLANES

    @pl.when(e == 0)
    def _():
        acc_scr[...] = jnp.zeros_like(acc_scr)
        a1_scr[...] = jnp.zeros_like(a1_scr)

    def step(a_cur, a_prev):
        a_cur[...] = _dot(u_ref[...], hn_ref[...])
        blk = jnp.maximum(e - 1, 0)
        for c in range(tm // tn):
            cols = slice(c * tn, (c + 1) * tn)
            w_rows = []
            for ii in range(eb // nk):
                i1 = blk * (eb // nk) + ii
                gate = None
                for h in range(n_heads):
                    n1 = jnp.broadcast_to(n1_ref[h, pl.ds(i1, 1), cols], (pk, tn)).astype(bf16)
                    e1 = jnp.broadcast_to(e1_ref[h, pl.ds(i1, 1), cols], (pk, tn)).astype(bf16)
                    r2 = r2_ref[h, :, cols].reshape(nk // pk, pk, tn)
                    e2 = e2_ref[h, :, cols].reshape(nk // pk, pk, tn)
                    term = jnp.where(r2 < n1[None], e2, jnp.zeros_like(e2)) * e1[None]
                    gate = term if gate is None else gate + term
                g = _gelu(a_prev[ii * nk:(ii + 1) * nk, cols]).astype(bf16)
                w_rows.append(gate.reshape(nk, tn) * g)
            acc_scr[:, cols] += _dot(vt_ref[...], jnp.concatenate(w_rows, axis=0))

    @pl.when(e % 2 == 0)
    def _():
        step(a0_scr, a1_scr)

    @pl.when(e % 2 == 1)
    def _():
        step(a1_scr, a0_scr)

    @pl.when(e == nb)
    def _():
        o_ref[...] = h1_ref[...] + acc_scr[...].T


def _peer_dense(hn_t, u_b, vt_b, r2, e2, n1, e1, h1, tm, eb):
    d, t = hn_t.shape
    nb = u_b.shape[0] // eb
    nh, nk, _ = r2.shape
    tok = lambda: pl.BlockSpec((nh, nk, tm), lambda i, e: (0, 0, i), pipeline_mode=pl.Buffered(1))
    return pl.pallas_call(
        functools.partial(_peer_dense_kernel, nb=nb, tn=_tile(tm, 2 * LANES)),
        grid=(t // tm, nb + 1),
        in_specs=[pl.BlockSpec((d, tm), lambda i, e: (0, i)),
                  pl.BlockSpec((eb, d), lambda i, e: (jnp.minimum(e, nb - 1), 0)),
                  pl.BlockSpec((d, eb), lambda i, e: (0, jnp.maximum(e - 1, 0))),
                  tok(), tok(), tok(), tok(),
                  pl.BlockSpec((tm, d), lambda i, e: (i, 0), pipeline_mode=pl.Buffered(1))],
        out_specs=pl.BlockSpec((tm, d), lambda i, e: (i, 0)),
        out_shape=jax.ShapeDtypeStruct((t, d), f32),
        scratch_shapes=[pltpu.VMEM((d, tm), f32), pltpu.VMEM((eb, tm), f32), pltpu.VMEM((eb, tm), f32)],
        compiler_params=_cparams(("parallel", "arbitrary")),
        name="peer_dense",
    )(hn_t, u_b, vt_b, r2, e2, n1, e1, h1)


def _tile(n, pref):
    if n <= pref:
        return n
    t = pref // LANES * LANES
    while n % t:
        t -= LANES
    assert t > 0, (n, pref)
    return t


def kernel(x, meta_tokens, norm1_g, w_in, q_norm_g, k_norm_g, lambda_q1, lambda_k1, lambda_q2, lambda_k2, subln_g, w_attn_branch, ssm_a_re, ssm_a_im, ssm_log_dt, ssm_b_re, ssm_b_im, ssm_c_re, ssm_c_im, ssm_d, w_glu, w_out, norm2_g, peer_w_q, peer_k1, peer_k2, peer_u, peer_v):
    assert norm1_g.shape[0] == 1, "single-layer block only"
    batch, seq, d = x.shape
    n_meta = meta_tokens.shape[0]
    sub = q_norm_g.shape[-1]
    aw = w_attn_branch.shape[1]
    heads = aw // (2 * sub)
    n_groups, n_state, n_chan = ssm_b_re.shape[1:]
    sw = n_groups * n_chan
    assert 2 * sub == LANES and LANES % n_chan == 0 and sw % LANES == 0
    assert n_meta % SUBLANES == 0 and n_meta <= LANES
    t = batch * seq
    n_in = w_in.shape[2]
    assert n_in == 3 * aw + sw + 2 * d

    x2 = x.reshape(t, d)
    w_in_b = w_in[0].astype(bf16)

    proj = _rms_matmul(x2, norm1_g, w_in_b, _tile(t, 1024), _tile(n_in, 1024), bf16, "in_proj")
    proj_m = _rms_matmul(meta_tokens, norm1_g, w_in_b, n_meta, _tile(n_in, 1024), bf16, "in_proj_meta")

    half = sub // 2
    inv_freq = jnp.power(ROPE_THETA, -jnp.arange(half, dtype=f32) / half)
    ang = jnp.arange(n_meta + seq, dtype=f32)[:, None] * inv_freq[None, :]
    cos_t = jnp.tile(jnp.cos(ang), (1, LANES // half))
    sin_t = jnp.tile(jnp.concatenate([-jnp.sin(ang), jnp.sin(ang)], axis=1), (1, LANES // sub))
    lane = jnp.arange(LANES)
    jmat = (lane[:, None] // sub == lane[None, :] // sub).astype(bf16)
    gq = jnp.tile(q_norm_g, (1, aw // sub))
    gk = jnp.tile(k_norm_g, (1, aw // sub))
    tr = _tile(seq, 512)
    q_rot = _normrot(proj, 0, aw, gq, cos_t[n_meta:], sin_t[n_meta:], jmat, tr, sub ** -0.5 * math.log2(math.e), sub,
                      "q_normrot")
    k_rot = _normrot(proj, 1, aw, gk, cos_t[n_meta:], sin_t[n_meta:], jmat, tr, 1.0, sub, "k_normrot")
    km_rot = _normrot(proj_m, 1, aw, gk, cos_t[:n_meta], sin_t[:n_meta], jmat, n_meta, 1.0, sub, "k_normrot_meta")
    km = jnp.pad(km_rot, ((0, LANES - n_meta), (0, 0)))
    vm = jnp.pad(proj_m[:, 2 * aw:3 * aw], ((0, LANES - n_meta), (0, 0)))

    att = _attention(q_rot, k_rot, proj, 2 * aw // LANES, km, vm, lambda_q1, lambda_k1, lambda_q2, lambda_k2,
                     subln_g, batch, seq, heads, _tile(seq, 512), n_meta, sub)

    blk = SSM_BLOCK
    gpt = LANES // n_chan
    ngt = sw // LANES
    assert 2 * n_state == LANES and seq % (blk * SUBLANES) == 0 and n_meta % blk == 0
    grp = lambda a: a.reshape(ngt, gpt, n_state)
    row_l = lambda a: jnp.tile(jnp.repeat(grp(a), n_chan, axis=1), (1, 1, 2))
    col_l = lambda a: jnp.repeat(grp(a).transpose(0, 2, 1), n_chan, axis=2)
    scan_l = lambda a: jnp.tile(grp(a), (1, 1, 2)).reshape(ngt, 1, gpt * LANES)
    ldt2 = jnp.broadcast_to(ssm_log_dt[0][:, None], (n_groups, n_state))
    lays = lambda f: (f(ssm_a_re[0]), f(ssm_a_im[0]), f(ldt2))
    b_l = lambda b: jnp.tile(b[0].transpose(0, 2, 1).reshape(ngt, LANES, n_state), (1, 1, 2))
    c_l = lambda c: c[0].reshape(ngt, gpt, n_chan, n_state).transpose(0, 3, 1, 2).reshape(ngt, n_state, LANES)
    prep = _ssm_prep(lays(row_l), lays(col_l), lays(scan_l), (b_l(ssm_b_re), b_l(ssm_b_im)),
                     (c_l(ssm_c_re), c_l(ssm_c_im)), n_state, n_chan)
    u0 = 3 * aw
    ys = _ssm(proj, u0 // LANES, proj_m[:, u0:u0 + sw], prep, ssm_d, seq // blk, n_state)

    tm = _tile(t, 1024)
    tn = _tile(d, 1024)
    nj = d // tn
    ga_col0 = (3 * aw + sw) // tn
    gb_col0 = (3 * aw + sw + d) // tn
    grid = (t // tm, nj)
    sem = _cparams(("parallel", "arbitrary"))
    row_full = lambda k: pl.BlockSpec((tm, k), lambda i, j: (i, 0))
    w_col = lambda k, off=0: pl.BlockSpec((k, tn), lambda i, j: (0, j + off))
    out_tile = pl.BlockSpec((tm, tn), lambda i, j: (i, j))
    yag = pl.pallas_call(
        _attn_out_kernel, grid=grid,
        in_specs=[row_full(aw), w_col(aw), pl.BlockSpec((tm, tn), lambda i, j: (i, ga_col0 + j))],
        out_specs=out_tile, out_shape=jax.ShapeDtypeStruct((t, d), bf16), compiler_params=sem, name="attn_out",
    )(att, w_attn_branch[0].astype(bf16), proj)
    w_glu_b = w_glu[0].astype(bf16)
    mix = pl.pallas_call(
        _glu_mix_kernel, grid=grid,
        in_specs=[row_full(sw), w_col(sw), w_col(sw, nj), pl.BlockSpec((tm, tn), lambda i, j: (i, gb_col0 + j)),
                  out_tile],
        out_specs=out_tile, out_shape=jax.ShapeDtypeStruct((t, d), bf16), compiler_params=sem, name="glu_mix",
    )(ys, w_glu_b, w_glu_b, proj, yag)
    h1 = pl.pallas_call(
        _out_proj_kernel, grid=grid,
        in_specs=[row_full(d), w_col(d), out_tile],
        out_specs=out_tile, out_shape=jax.ShapeDtypeStruct((t, d), f32), compiler_params=sem, name="out_proj",
    )(mix, w_out[0].astype(bf16), x2)

    kk = jnp.stack([peer_k1[0], peer_k2[0]], axis=1).astype(bf16)
    assert kk.shape[2] == LANES and kk.shape[3] == LANES
    hn2, st = _peer_query(h1, norm2_g, peer_w_q[0].astype(bf16), kk, _tile(t, 512))
    r2, e2, n1, e1 = _peer_topk(st, _tile(t, 512))
    u_b = peer_u[0].astype(bf16)
    vt_b = peer_v[0].T.astype(bf16)
    out = _peer_dense(hn2, u_b, vt_b, r2, e2, n1, e1, h1, _tile(t, 512), _tile(u_b.shape[0], 1024))
    return out.reshape(batch, seq, d)
```

```python
import functools
import math

import jax
import jax.numpy as jnp
from jax import lax
from jax.experimental import pallas as pl
from jax.experimental.pallas import tpu as pltpu

EPS = 1e-6
ROPE_THETA = 10000.0
LAM_INIT = 0.8 - 0.6 * math.exp(-0.3 * 0)
TOPK = 16
LANES = 128
SUBLANES = 8
NEG_BIG = -1e30
VMEM_LIMIT = 56 * 1024 * 1024

bf16 = jnp.bfloat16
f32 = jnp.float32


def _cparams(sem):
    return pltpu.CompilerParams(dimension_semantics=sem, vmem_limit_bytes=VMEM_LIMIT)


def _dot(a, b):
    return jnp.dot(a, b, preferred_element_type=f32)


def _dot_nt(a, b):
    return lax.dot_general(a, b, (((1,), (1,)), ((), ())), preferred_element_type=f32)


def _sigmoid(x):
    return 1.0 / (1.0 + jnp.exp(-x))


def _gelu(x):
    return 0.5 * x * (1.0 + lax.erf(x * (1.0 / math.sqrt(2.0))))


def _stack_rows(rows):
    k = len(rows)
    idx = lax.broadcasted_iota(jnp.int32, (k, rows[0].shape[1]), 0)
    out = jnp.broadcast_to(rows[0], idx.shape)
    for j in range(1, k):
        out = jnp.where(idx == j, rows[j], out)
    return out


def _rms_matmul_kernel(x_ref, g_ref, w_ref, o_ref, xn_ref):
    @pl.when(pl.program_id(1) == 0)
    def _():
        x = x_ref[...]
        ms = jnp.mean(x * x, axis=-1, keepdims=True)
        xn_ref[...] = (x * lax.rsqrt(ms + EPS) * g_ref[...]).astype(bf16)

    o_ref[...] = _dot(xn_ref[...], w_ref[...]).astype(o_ref.dtype)


def _rms_matmul(x, g, w, tm, tn, out_dtype, name):
    m, k = x.shape
    n = w.shape[1]
    return pl.pallas_call(
        _rms_matmul_kernel,
        grid=(m // tm, n // tn),
        in_specs=[pl.BlockSpec((tm, k), lambda i, j: (i, 0)),
                  pl.BlockSpec((1, k), lambda i, j: (0, 0)),
                  pl.BlockSpec((k, tn), lambda i, j: (0, j))],
        out_specs=pl.BlockSpec((tm, tn), lambda i, j: (i, j)),
        out_shape=jax.ShapeDtypeStruct((m, n), out_dtype),
        scratch_shapes=[pltpu.VMEM((tm, k), bf16)],
        compiler_params=_cparams(("parallel", "arbitrary")),
        name=name,
    )(x, g, w)


def _normrot_kernel(x_ref, g_ref, cos_ref, sin_ref, j_ref, o_ref, *, scale, sub):
    width = x_ref.shape[1]
    cos = cos_ref[...]
    sin = sin_ref[...]
    jmat = j_ref[...]
    lane = lax.broadcasted_iota(jnp.int32, cos.shape, 1)
    first_half = (lane % sub) < (sub // 2)
    for t in range(width // LANES):
        sl = slice(t * LANES, (t + 1) * LANES)
        x = x_ref[:, sl].astype(f32)
        x2 = x * x
        hi = x2.astype(bf16)
        lo = (x2 - hi.astype(f32)).astype(bf16)
        ssq = _dot(hi, jmat) + _dot(lo, jmat)
        xn = x * lax.rsqrt(ssq * (1.0 / sub) + EPS) * g_ref[:, sl]
        partner = jnp.where(first_half, pltpu.roll(xn, LANES - sub // 2, 1), pltpu.roll(xn, sub // 2, 1))
        o_ref[:, sl] = ((xn * cos + partner * sin) * scale).astype(o_ref.dtype)


def _normrot(x, col_block, width, g_t, cos, sin, jmat, tm, scale, sub, name):
    m = x.shape[0]
    nt = cos.shape[0] // tm
    return pl.pallas_call(
        functools.partial(_normrot_kernel, scale=scale, sub=sub),
        grid=(m // tm,),
        in_specs=[pl.BlockSpec((tm, width), lambda i: (i, col_block)),
                  pl.BlockSpec((1, width), lambda i: (0, 0)),
                  pl.BlockSpec((tm, LANES), lambda i: (i % nt, 0)),
                  pl.BlockSpec((tm, LANES), lambda i: (i % nt, 0)),
                  pl.BlockSpec((LANES, LANES), lambda i: (0, 0))],
        out_specs=pl.BlockSpec((tm, width), lambda i: (i, 0)),
        out_shape=jax.ShapeDtypeStruct((m, width), bf16),
        compiler_params=_cparams(("parallel",)),
        name=name,
    )(x, g_t, cos, sin, jmat)


def _attn_kernel(q_ref, k_ref, v_ref, km_ref, vm_ref, lq1_ref, lk1_ref, lq2_ref, lk2_ref, sg_ref,
                 o_ref, qq_scr, m_scr, acc_scr, s_scr, *, n_meta, sub, hpb):
    i = pl.program_id(2)
    tq = q_ref.shape[0]
    tk = tq
    hw = 2 * sub
    mp = km_ref.shape[0]
    lane = lax.broadcasted_iota(jnp.int32, (tq, hw), 1)
    head = lambda hh: slice(hh * hw, (hh + 1) * hw)

    def softmax_step(hh, s, v_aug, first):
        smax = jnp.max(s, axis=1, keepdims=True)
        if first:
            m_new = jnp.broadcast_to(smax, (2 * tq, hw))
        else:
            m_prev = m_scr[hh]
            m_new = jnp.maximum(m_prev, smax)
        p = jnp.concatenate([jnp.exp2(s[:, c * hw:(c + 1) * hw] - m_new) for c in range(s.shape[1] // hw)], axis=1)
        pv = _dot(p.astype(bf16), v_aug)
        if first:
            acc_scr[hh] = pv
        else:
            alpha = jnp.exp2(m_prev - m_new)
            acc_scr[hh] = jnp.concatenate([alpha, alpha], axis=1) * acc_scr[hh] + pv
        m_scr[hh] = m_new

    for hh in range(hpb):
        q = q_ref[:, head(hh)]
        zero = jnp.zeros_like(q)
        qq_scr[hh] = jnp.concatenate([jnp.where(lane < sub, q, zero), jnp.where(lane >= sub, q, zero)], axis=0)

    def put_scores(j, slot):
        start = pl.multiple_of(j * tk, tk)
        for hh in range(hpb):
            s_scr[slot, hh] = _dot_nt(qq_scr[hh], k_ref[pl.ds(start, tk), head(hh)])

    def consume(j, slot, masked):
        start = pl.multiple_of(j * tk, tk)
        for hh in range(hpb):
            s = s_scr[slot, hh]
            if masked:
                row = lax.broadcasted_iota(jnp.int32, s.shape, 0) % tq
                col = lax.broadcasted_iota(jnp.int32, s.shape, 1)
                s = jnp.where(col <= row, s, NEG_BIG)
            vb = v_ref[pl.ds(start, tk), head(hh)]
            softmax_step(hh, s, jnp.concatenate([vb, jnp.ones((tk, hw), bf16)], axis=1), False)

    def stage(j, slot):
        put_scores(j + 1, 1 - slot)
        consume(j, slot, False)

    put_scores(0, 0)

    for hh in range(hpb):
        s = _dot_nt(qq_scr[hh], km_ref[:, head(hh)])
        col = lax.broadcasted_iota(jnp.int32, s.shape, 1)
        s = jnp.where(col < n_meta, s, NEG_BIG)
        softmax_step(hh, s, jnp.concatenate([vm_ref[:, head(hh)], jnp.ones((mp, hw), bf16)], axis=1), True)

    def body(p, c):
        stage(2 * p, 0)
        stage(2 * p + 1, 1)
        return c

    lax.fori_loop(0, i // 2, body, 0)

    @pl.when(i % 2 == 0)
    def _():
        consume(i, 0, True)

    @pl.when(i % 2 == 1)
    def _():
        stage(i - 1, 0)
        consume(i, 1, True)

    lam = (jnp.exp(jnp.sum(lq1_ref[...] * lk1_ref[...], axis=1, keepdims=True))
           - jnp.exp(jnp.sum(lq2_ref[...] * lk2_ref[...], axis=1, keepdims=True)) + LAM_INIT)
    for hh in range(hpb):
        acc = acc_scr[hh]
        o = acc[:, :hw] / acc[:, hw:]
        att = o[:tq] - lam * o[tq:]
        ms = jnp.mean(att * att, axis=-1, keepdims=True)
        att = att * lax.rsqrt(ms + EPS) * sg_ref[...] * (1.0 - LAM_INIT)
        o_ref[:, head(hh)] = att.astype(o_ref.dtype)


def _attention(q_rot, k_rot, proj, v_col0, km, vm, lq1, lk1, lq2, lk2, sg, batch, seq, heads, tq, n_meta, sub):
    t, aw = q_rot.shape
    nq = seq // tq
    hw = 2 * sub
    hpb = 2 if heads % 2 == 0 else 1
    bw = hpb * hw
    mp = km.shape[0]
    vec = lambda: pl.BlockSpec((1, sub), lambda b, h, i: (0, 0))
    return pl.pallas_call(
        functools.partial(_attn_kernel, n_meta=n_meta, sub=sub, hpb=hpb),
        grid=(batch, heads // hpb, nq),
        in_specs=[pl.BlockSpec((tq, bw), lambda b, h, i: (b * nq + i, h)),
                  pl.BlockSpec((seq, bw), lambda b, h, i: (b, h)),
                  pl.BlockSpec((seq, bw), lambda b, h, i: (b, v_col0 // hpb + h)),
                  pl.BlockSpec((mp, bw), lambda b, h, i: (0, h)),
                  pl.BlockSpec((mp, bw), lambda b, h, i: (0, h)),
                  vec(), vec(), vec(), vec(),
                  pl.BlockSpec((1, hw), lambda b, h, i: (0, 0))],
        out_specs=pl.BlockSpec((tq, bw), lambda b, h, i: (b * nq + i, h)),
        out_shape=jax.ShapeDtypeStruct((t, aw), bf16),
        scratch_shapes=[pltpu.VMEM((hpb, 2 * tq, hw), bf16), pltpu.VMEM((hpb, 2 * tq, hw), f32),
                        pltpu.VMEM((hpb, 2 * tq, 2 * hw), f32), pltpu.VMEM((2, hpb, 2 * tq, tq), f32)],
        compiler_params=_cparams(("parallel", "parallel", "arbitrary")),
        name="diff_attention",
    )(q_rot, k_rot, proj, km, vm, lq1, lk1, lq2, lk2, sg)


SSM_BLOCK = 8


def _cmul(ar, ai, br, bi):
    return ar * br - ai * bi, ar * bi + ai * br


def _dot3(a, b):
    ah = a.astype(bf16)
    al = (a - ah.astype(f32)).astype(bf16)
    bh = b.astype(bf16)
    bl = (b - bh.astype(f32)).astype(bf16)
    return _dot(ah, bh) + (_dot(ah, bl) + _dot(al, bh))


def _abar(a_re, a_im, log_dt):
    dt = jnp.exp(log_dt)
    er = jnp.exp(a_re * dt)
    return er * jnp.cos(a_im * dt), er * jnp.sin(a_im * dt)


def _powers(ar, ai, n):
    out = [(jnp.ones_like(ar), jnp.zeros_like(ai)), (ar, ai)]
    for _ in range(2, n + 1):
        out.append(_cmul(*out[-1], ar, ai))
    return out


def _ssm_prep_kernel(ar_ref, ai_ref, ld_ref, ac_ref, aic_ref, ldc_ref, as_ref, ais_ref, lds_ref,
                     br_ref, bi_ref, cr_ref, ci_ref, w_ref, q_ref, k_ref, pr_ref, ps_ref, *, n_state, n_chan):
    blk = SSM_BLOCK
    gpt = LANES // n_chan
    a_re, a_im = ar_ref[0], ai_ref[0]
    abr, abi = _abar(a_re, a_im, ld_ref[0])
    den = a_re * a_re + a_im * a_im
    nr, ni = abr - 1.0, abi
    f_re = (nr * a_re + ni * a_im) / den
    f_im = (ni * a_re - nr * a_im) / den
    bbr, bbi = _cmul(f_re, f_im, br_ref[0], bi_ref[0])
    pw = _powers(abr, abi, blk - 1)
    is_re = lax.broadcasted_iota(jnp.int32, a_re.shape, 1) < n_state
    rgrp = lax.broadcasted_iota(jnp.int32, a_re.shape, 0) // n_chan
    wf = []
    for i in range(blk):
        xr, xi = _cmul(bbr, bbi, *pw[blk - 1 - i])
        tile = jnp.where(is_re, xr, xi)
        wf.append(jnp.concatenate([jnp.where(rgrp == g, tile, 0.0) for g in range(gpt)], axis=1))
        w_ref[0, i] = wf[i].astype(bf16)
    c_re, c_im = cr_ref[0], ci_ref[0]
    acr, aci = _abar(ac_ref[0], aic_ref[0], ldc_ref[0])
    pwc = _powers(acr, aci, blk)
    cgrp = lax.broadcasted_iota(jnp.int32, c_re.shape, 1) // n_chan

    def by_group(tr, ti):
        rows = []
        for g in range(gpt):
            rows += [jnp.where(cgrp == g, tr, 0.0), jnp.where(cgrp == g, ti, 0.0)]
        return jnp.concatenate(rows, axis=0)

    for j in range(blk):
        qr, qi = _cmul(c_re, c_im, *pwc[j + 1])
        q_ref[0, :, j * LANES:(j + 1) * LANES] = by_group(qr, -qi).astype(bf16)
    ccat = by_group(c_re, -c_im)
    kd = [_dot3(wf[blk - 1 - dd], ccat).astype(bf16) for dd in range(blk)]
    for i in range(blk):
        for j in range(blk):
            tile = kd[j - i] if j >= i else jnp.zeros((LANES, LANES), bf16)
            k_ref[0, i * LANES:(i + 1) * LANES, j * LANES:(j + 1) * LANES] = tile
    asr, asi = _abar(as_ref[0], ais_ref[0], lds_ref[0])
    a8 = _powers(asr, asi, blk)[blk]
    pl_ = _powers(a8[0], a8[1], SUBLANES)[1:]
    re_lane = lax.broadcasted_iota(jnp.int32, asr.shape, 1) % LANES < n_state
    pr_ref[0] = _stack_rows([p[0] for p in pl_])
    ps_ref[0] = _stack_rows([jnp.where(re_lane, -p[1], p[1]) for p in pl_])


def _ssm_prep(rows, cols, scan, b_t, c_t, n_state, n_chan):
    ngt = b_t[0].shape[0]
    blk = SSM_BLOCK
    sl = scan[0].shape[2]
    spec3 = lambda a: pl.BlockSpec((1,) + a.shape[1:], lambda g: (g, 0, 0))
    ins = list(rows) + list(cols) + list(scan) + list(b_t) + list(c_t)
    w_s = jax.ShapeDtypeStruct((ngt, blk, LANES, sl), bf16)
    q_s = jax.ShapeDtypeStruct((ngt, sl, blk * LANES), bf16)
    k_s = jax.ShapeDtypeStruct((ngt, blk * LANES, blk * LANES), bf16)
    t_s = jax.ShapeDtypeStruct((ngt, SUBLANES, sl), f32)
    spec4 = lambda a: pl.BlockSpec((1,) + a.shape[1:], lambda g: (g, 0, 0, 0))
    return pl.pallas_call(
        functools.partial(_ssm_prep_kernel, n_state=n_state, n_chan=n_chan),
        grid=(ngt,),
        in_specs=[spec3(a) for a in ins],
        out_specs=[spec4(w_s), spec3(q_s), spec3(k_s), spec3(t_s), spec3(t_s)],
        out_shape=[w_s, q_s, k_s, t_s, t_s],
        compiler_params=_cparams(("parallel",)),
        name="ssm_prep",
    )(*ins)


def _ssm_kernel(u_ref, um_ref, w_ref, q_ref, k_ref, pr_ref, ps_ref, d_ref, y_ref,
                uf_scr, umf_scr, s_scr, sw_scr, xp_scr, yf_scr, *, rows_per_seq, meta_rows, n_state):
    blk = SSM_BLOCK
    r = u_ref.shape[0] // blk
    sl = s_scr.shape[1]
    tiles = sl // LANES
    pr, ps = pr_ref[0], ps_ref[0]

    def swap(x):
        return jnp.concatenate([pltpu.roll(x[:, c * LANES:(c + 1) * LANES], n_state, 1) for c in range(tiles)],
                               axis=1)

    uf_scr[...] = u_ref[...].astype(f32)
    umf_scr[...] = jnp.zeros_like(umf_scr)
    umf_scr[0:um_ref.shape[0], :] = um_ref[...].astype(f32)
    step_rows = lambda i: uf_scr[pl.ds(i, r, stride=blk), :]
    ucat = jnp.concatenate([step_rows(i).astype(bf16) for i in range(blk)], axis=1)
    umcat = jnp.concatenate([umf_scr[pl.ds(i, SUBLANES, stride=blk), :].astype(bf16) for i in range(blk)], axis=1)
    s = _dot(ucat, w_ref[0])
    sm = _dot(umcat, w_ref[0])
    rowmod = lax.broadcasted_iota(jnp.int32, s.shape, 0) % SUBLANES
    for k in (1, 2, 4):
        t = pltpu.roll(s, k, 0)
        t = t * pr[k - 1:k] + swap(t) * ps[k - 1:k]
        s = s + jnp.where(rowmod >= k, t, 0.0)
    s_scr[...] = s
    sw_scr[...] = swap(s)
    x0 = sm[0:1]
    for mrow in range(1, meta_rows):
        x0 = x0 * pr[0:1] + swap(x0) * ps[0:1] + sm[mrow:mrow + 1]
    x0w = swap(x0)
    row8 = lax.broadcasted_iota(jnp.int32, (SUBLANES, sl), 0)

    def body(b, carry):
        start = b % (rows_per_seq // SUBLANES) == 0
        rows = pl.ds(pl.multiple_of(b * SUBLANES, SUBLANES), SUBLANES)
        c = jnp.broadcast_to(jnp.where(start, x0, carry[0]), (SUBLANES, sl))
        cw = jnp.broadcast_to(jnp.where(start, x0w, carry[1]), (SUBLANES, sl))
        xs = s_scr[rows, :] + (c * pr + cw * ps)
        xw = sw_scr[rows, :] + (cw * pr - c * ps)
        xp_scr[rows, :] = jnp.where(row8 == 0, c, pltpu.roll(xs, 1, 0))
        return xs[SUBLANES - 1:SUBLANES], xw[SUBLANES - 1:SUBLANES]

    lax.fori_loop(0, r // SUBLANES, body, (x0, x0w))

    y_all = _dot(xp_scr[...].astype(bf16), q_ref[0]) + _dot(ucat, k_ref[0])
    for j in range(blk):
        y = y_all[:, j * LANES:(j + 1) * LANES] + d_ref[...] * step_rows(j)
        yf_scr[pl.ds(j, r, stride=blk), :] = _gelu(y)
    y_ref[...] = yf_scr[...].astype(y_ref.dtype)


def _ssm(u_arr, u_col0, um, prep, d, rows_per_seq, n_state):
    w, q, k, pr, ps = prep
    ngt, blk, _, sl = w.shape
    w = w.reshape(ngt, blk * LANES, sl)
    t = u_arr.shape[0]
    n_meta = um.shape[0]
    spec3 = lambda a: pl.BlockSpec((1,) + a.shape[1:], lambda g: (g, 0, 0))
    return pl.pallas_call(
        functools.partial(_ssm_kernel, rows_per_seq=rows_per_seq, meta_rows=n_meta // blk, n_state=n_state),
        grid=(ngt,),
        in_specs=[pl.BlockSpec((t, LANES), lambda g: (0, u_col0 + g)),
                  pl.BlockSpec((n_meta, LANES), lambda g: (0, g)),
                  spec3(w), spec3(q), spec3(k), spec3(pr), spec3(ps),
                  pl.BlockSpec((1, LANES), lambda g: (0, g))],
        out_specs=pl.BlockSpec((t, LANES), lambda g: (0, g)),
        out_shape=jax.ShapeDtypeStruct((t, ngt * LANES), bf16),
        scratch_shapes=[pltpu.VMEM((t, LANES), f32), pltpu.VMEM((SUBLANES * blk, LANES), f32),
                        pltpu.VMEM((t // blk, sl), f32), pltpu.VMEM((t // blk, sl), f32),
                        pltpu.VMEM((t // blk, sl), f32), pltpu.VMEM((t, LANES), f32)],
        compiler_params=_cparams(("parallel",)),
        name="ssm_blocked",
    )(u_arr, um, w, q, k, pr, ps, d)


def _mix_kernel(a_ref, wattn_ref, s_ref, wa_ref, wb_ref, ga_ref, gb_ref, o_ref):
    ya = _dot(a_ref[...], wattn_ref[...])
    s = s_ref[...]
    yb = _dot(s, wa_ref[...]) * _sigmoid(_dot(s, wb_ref[...]))
    o_ref[...] = (_sigmoid(ga_ref[...].astype(f32)) * ya
                  + _sigmoid(gb_ref[...].astype(f32)) * yb).astype(o_ref.dtype)


def _transpose_cast_kernel(x_ref, o_ref):
    o_ref[...] = x_ref[...].T.astype(o_ref.dtype)


def _transpose_cast(x, tr, dtype, name):
    rows, cols = x.shape
    return pl.pallas_call(
        _transpose_cast_kernel,
        grid=(rows // tr,),
        in_specs=[pl.BlockSpec((tr, cols), lambda i: (i, 0))],
        out_specs=pl.BlockSpec((cols, tr), lambda i: (0, i)),
        out_shape=jax.ShapeDtypeStruct((cols, rows), dtype),
        compiler_params=_cparams(("parallel",)),
        name=name,
    )(x)


def _out_proj_kernel(m_ref, w_ref, x_ref, o_ref):
    o_ref[...] = x_ref[...] + _dot(m_ref[...], w_ref[...])


def _peer_query_kernel(h_ref, g_ref, wq_ref, kk_ref, hn_ref, st_ref):
    x = h_ref[...]
    ms = jnp.mean(x * x, axis=-1, keepdims=True)
    hn_f = x * lax.rsqrt(ms + EPS) * g_ref[...]
    hn_ref[...] = hn_f.T.astype(bf16)
    q = _dot(hn_f.astype(bf16), wq_ref[...]).astype(bf16)
    n_heads = kk_ref.shape[0]
    half = kk_ref.shape[3]
    for h in range(n_heads):
        for side in range(2):
            c0 = (2 * h + side) * half
            st_ref[h, side] = _dot_nt(kk_ref[h, side], q[:, c0:c0 + half])


def _peer_query(h1, g, wq, kk, tm):
    t, d = h1.shape
    nh, _, nk, half = kk.shape
    return pl.pallas_call(
        _peer_query_kernel,
        grid=(t // tm,),
        in_specs=[pl.BlockSpec((tm, d), lambda i: (i, 0)),
                  pl.BlockSpec((1, d), lambda i: (0, 0)),
                  pl.BlockSpec(wq.shape, lambda i: (0, 0)),
                  pl.BlockSpec(kk.shape, lambda i: (0, 0, 0, 0))],
        out_specs=[pl.BlockSpec((d, tm), lambda i: (0, i)),
                   pl.BlockSpec((nh, 2, nk, tm), lambda i: (0, 0, 0, i))],
        out_shape=[jax.ShapeDtypeStruct((d, t), bf16), jax.ShapeDtypeStruct((nh, 2, nk, t), f32)],
        compiler_params=_cparams(("parallel",)),
        name="peer_query",
    )(h1, g, wq, kk)


def _sort_pairs(n):
    pairs = []
    p = 1
    while p < n:
        k = p
        while k >= 1:
            for j in range(k % p, n - k, 2 * k):
                for i in range(min(k, n - j - k)):
                    if (i + j) // (2 * p) == (i + j + k) // (2 * p):
                        pairs.append((i + j, i + j + k))
            k //= 2
        p *= 2
    return pairs


def _top_desc(tiles, n):
    m = len(tiles)
    vs = list(tiles)
    for i, j in _sort_pairs(1 << (m - 1).bit_length()):
        if j < m:
            vs[i], vs[j] = jnp.maximum(vs[i], vs[j]), jnp.minimum(vs[i], vs[j])
    neg = jnp.full(vs[0].shape, -jnp.inf, f32)
    vs = vs[:n] + [neg]
    vals = []
    for a in range(n):
        v = jnp.max(vs[0], axis=0, keepdims=True)
        vals.append(v)
        hit = vs[0] == v
        keep = min(len(vs) - 1, n - a - 1)
        vs = [jnp.where(hit, vs[k + 1], vs[k]) for k in range(keep)] + [neg]
    return _stack_rows(vals)


def _peer_topk_kernel(st_ref, r2_ref, e2_ref, n1_ref, e1_ref):
    chunk = LANES
    sub8 = lambda x: [x[k * SUBLANES:(k + 1) * SUBLANES] for k in range(x.shape[0] // SUBLANES)]
    for c in range(st_ref.shape[3] // chunk):
        cols = slice(c * chunk, (c + 1) * chunk)
        s1 = st_ref[0, 0, :, cols]
        s2 = st_ref[0, 1, :, cols]
        v1 = _top_desc(sub8(s1), TOPK)
        v2 = _top_desc(sub8(s2), TOPK)
        half = TOPK // 2
        cand = ([v1[0:1] + v2[:half], v1[0:1] + v2[half:]] + [v1[a:a + 1] + v2[:half] for a in range(1, half)]
                + [v1[half:] + v2[0:1]])
        best = _top_desc(cand, TOPK)
        tau = best[TOPK - 1:TOPK]
        z = jnp.sum(jnp.exp(best - best[0:1]), axis=0, keepdims=True)
        n1 = jnp.zeros(s1.shape, f32)
        rank2 = jnp.zeros(s2.shape, f32)
        for a in range(TOPK):
            cnt = jnp.sum(jnp.where((v1[a:a + 1] + v2) >= tau, 1.0, 0.0), axis=0, keepdims=True)
            n1 = jnp.where(s1 == v1[a:a + 1], cnt, n1)
            rank2 = jnp.where(s2 < v2[a:a + 1], a + 1.0, rank2)
        r2_ref[0, :, cols] = rank2.astype(bf16)
        e2_ref[0, :, cols] = jnp.exp(s2 - v2[0:1]).astype(bf16)
        n1_ref[0, :, cols] = n1
        e1_ref[0, :, cols] = jnp.exp(s1 - v1[0:1]) / z


def _peer_topk(st, tl):
    nh, _, nk, t = st.shape
    spec = lambda: pl.BlockSpec((1, nk, tl), lambda h, i: (h, 0, i))
    shp = lambda dt: jax.ShapeDtypeStruct((nh, nk, t), dt)
    return pl.pallas_call(
        _peer_topk_kernel,
        grid=(nh, t // tl),
        in_specs=[pl.BlockSpec((1, 2, nk, tl), lambda h, i: (h, 0, 0, i))],
        out_specs=[spec(), spec(), spec(), spec()],
        out_shape=[shp(bf16), shp(bf16), shp(f32), shp(f32)],
        compiler_params=_cparams(("parallel", "parallel")),
        name="peer_topk",
    )(st)


def _peer_dense_kernel(hn_ref, u_ref, vt_ref, r2_ref, e2_ref, n1_ref, e1_ref, h1_ref, o_ref,
                       acc_scr, a0_scr, a1_scr, *, nb, tn):
    e = pl.program_id(1)
    n_heads, nk, tm = r2_ref.shape
    eb = u_ref.shape[0]
    pk = 2 * SUBLANES

    @pl.when(e == 0)
    def _():
        acc_scr[...] = jnp.zeros_like(acc_scr)
        a1_scr[...] = jnp.zeros_like(a1_scr)

    def step(a_cur, a_prev):
        a_cur[...] = _dot(u_ref[...], hn_ref[...])
        blk = jnp.maximum(e - 1, 0)
        for c in range(tm // tn):
            cols = slice(c * tn, (c + 1) * tn)
            w_rows = []
            for ii in range(eb // nk):
                i1 = blk * (eb // nk) + ii
                gate = None
                for h in range(n_heads):
                    n1 = jnp.broadcast_to(n1_ref[h, pl.ds(i1, 1), cols], (pk, tn)).astype(bf16)
                    e1 = jnp.broadcast_to(e1_ref[h, pl.ds(i1, 1), cols], (pk, tn)).astype(bf16)
                    r2 = r2_ref[h, :, cols].reshape(nk // pk, pk, tn)
                    e2 = e2_ref[h, :, cols].reshape(nk // pk, pk, tn)
                    term = jnp.where(r2 < n1[None], e2, jnp.zeros_like(e2)) * e1[None]
                    gate = term if gate is None else gate + term
                g = _gelu(a_prev[ii * nk:(ii + 1) * nk, cols]).astype(bf16)
                w_rows.append(gate.reshape(nk, tn) * g)
            acc_scr[:, cols] += _dot(vt_ref[...], jnp.concatenate(w_rows, axis=0))

    @pl.when(e % 2 == 0)
    def _():
        step(a0_scr, a1_scr)

    @pl.when(e % 2 == 1)
    def _():
        step(a1_scr, a0_scr)

    @pl.when(e == nb)
    def _():
        o_ref[...] = h1_ref[...] + acc_scr[...].T


def _peer_dense(hn_t, u_b, vt_b, r2, e2, n1, e1, h1, tm, eb):
    d, t = hn_t.shape
    nb = u_b.shape[0] // eb
    nh, nk, _ = r2.shape
    tok = lambda: pl.BlockSpec((nh, nk, tm), lambda i, e: (0, 0, i), pipeline_mode=pl.Buffered(1))
    return pl.pallas_call(
        functools.partial(_peer_dense_kernel, nb=nb, tn=_tile(tm, 2 * LANES)),
        grid=(t // tm, nb + 1),
        in_specs=[pl.BlockSpec((d, tm), lambda i, e: (0, i)),
                  pl.BlockSpec((eb, d), lambda i, e: (jnp.minimum(e, nb - 1), 0)),
                  pl.BlockSpec((d, eb), lambda i, e: (0, jnp.maximum(e - 1, 0))),
                  tok(), tok(), tok(), tok(),
                  pl.BlockSpec((tm, d), lambda i, e: (i, 0), pipeline_mode=pl.Buffered(1))],
        out_specs=pl.BlockSpec((tm, d), lambda i, e: (i, 0)),
        out_shape=jax.ShapeDtypeStruct((t, d), f32),
        scratch_shapes=[pltpu.VMEM((d, tm), f32), pltpu.VMEM((eb, tm), f32), pltpu.VMEM((eb, tm), f32)],
        compiler_params=_cparams(("parallel", "arbitrary")),
        name="peer_dense",
    )(hn_t, u_b, vt_b, r2, e2, n1, e1, h1)


def _tile(n, pref):
    if n <= pref:
        return n
    t = pref // LANES * LANES
    while n % t:
        t -= LANES
    assert t > 0, (n, pref)
    return t


def kernel(x, meta_tokens, norm1_g, w_in, q_norm_g, k_norm_g, lambda_q1, lambda_k1, lambda_q2, lambda_k2, subln_g, w_attn_branch, ssm_a_re, ssm_a_im, ssm_log_dt, ssm_b_re, ssm_b_im, ssm_c_re, ssm_c_im, ssm_d, w_glu, w_out, norm2_g, peer_w_q, peer_k1, peer_k2, peer_u, peer_v):
    assert norm1_g.shape[0] == 1, "single-layer block only"
    batch, seq, d = x.shape
    n_meta = meta_tokens.shape[0]
    sub = q_norm_g.shape[-1]
    aw = w_attn_branch.shape[1]
    heads = aw // (2 * sub)
    n_groups, n_state, n_chan = ssm_b_re.shape[1:]
    sw = n_groups * n_chan
    assert 2 * sub == LANES and LANES % n_chan == 0 and sw % LANES == 0
    assert n_meta % SUBLANES == 0 and n_meta <= LANES
    t = batch * seq
    n_in = w_in.shape[2]
    assert n_in == 3 * aw + sw + 2 * d

    x2 = x.reshape(t, d)
    w_in_b = w_in[0].astype(bf16)

    proj = _rms_matmul(x2, norm1_g, w_in_b, _tile(t, 1024), _tile(n_in, 1024), bf16, "in_proj")
    proj_m = _rms_matmul(meta_tokens, norm1_g, w_in_b, n_meta, _tile(n_in, 1024), bf16, "in_proj_meta")

    half = sub // 2
    inv_freq = jnp.power(ROPE_THETA, -jnp.arange(half, dtype=f32) / half)
    ang = jnp.arange(n_meta + seq, dtype=f32)[:, None] * inv_freq[None, :]
    cos_t = jnp.tile(jnp.cos(ang), (1, LANES // half))
    sin_t = jnp.tile(jnp.concatenate([-jnp.sin(ang), jnp.sin(ang)], axis=1), (1, LANES // sub))
    lane = jnp.arange(LANES)
    jmat = (lane[:, None] // sub == lane[None, :] // sub).astype(bf16)
    gq = jnp.tile(q_norm_g, (1, aw // sub))
    gk = jnp.tile(k_norm_g, (1, aw // sub))
    tr = _tile(seq, 512)
    q_rot = _normrot(proj, 0, aw, gq, cos_t[n_meta:], sin_t[n_meta:], jmat, tr, sub ** -0.5 * math.log2(math.e), sub,
                      "q_normrot")
    k_rot = _normrot(proj, 1, aw, gk, cos_t[n_meta:], sin_t[n_meta:], jmat, tr, 1.0, sub, "k_normrot")
    km_rot = _normrot(proj_m, 1, aw, gk, cos_t[:n_meta], sin_t[:n_meta], jmat, n_meta, 1.0, sub, "k_normrot_meta")
    km = jnp.pad(km_rot, ((0, LANES - n_meta), (0, 0)))
    vm = jnp.pad(proj_m[:, 2 * aw:3 * aw], ((0, LANES - n_meta), (0, 0)))

    att = _attention(q_rot, k_rot, proj, 2 * aw // LANES, km, vm, lambda_q1, lambda_k1, lambda_q2, lambda_k2,
                     subln_g, batch, seq, heads, _tile(seq, 512), n_meta, sub)

    blk = SSM_BLOCK
    gpt = LANES // n_chan
    ngt = sw // LANES
    assert 2 * n_state == LANES and seq % (blk * SUBLANES) == 0 and n_meta % blk == 0
    grp = lambda a: a.reshape(ngt, gpt, n_state)
    row_l = lambda a: jnp.tile(jnp.repeat(grp(a), n_chan, axis=1), (1, 1, 2))
    col_l = lambda a: jnp.repeat(grp(a).transpose(0, 2, 1), n_chan, axis=2)
    scan_l = lambda a: jnp.tile(grp(a), (1, 1, 2)).reshape(ngt, 1, gpt * LANES)
    ldt2 = jnp.broadcast_to(ssm_log_dt[0][:, None], (n_groups, n_state))
    lays = lambda f: (f(ssm_a_re[0]), f(ssm_a_im[0]), f(ldt2))
    b_l = lambda b: jnp.tile(b[0].transpose(0, 2, 1).reshape(ngt, LANES, n_state), (1, 1, 2))
    c_l = lambda c: c[0].reshape(ngt, gpt, n_chan, n_state).transpose(0, 3, 1, 2).reshape(ngt, n_state, LANES)
    prep = _ssm_prep(lays(row_l), lays(col_l), lays(scan_l), (b_l(ssm_b_re), b_l(ssm_b_im)),
                     (c_l(ssm_c_re), c_l(ssm_c_im)), n_state, n_chan)
    u0 = 3 * aw
    ys = _ssm(proj, u0 // LANES, proj_m[:, u0:u0 + sw], prep, ssm_d, seq // blk, n_state)

    tm = _tile(t, 1024)
    tn = _tile(d, 1024)
    nj = d // tn
    ga_col0 = (3 * aw + sw) // tn
    gb_col0 = (3 * aw + sw + d) // tn
    grid = (t // tm, nj)
    sem = _cparams(("parallel", "arbitrary"))
    row_full = lambda k: pl.BlockSpec((tm, k), lambda i, j: (i, 0))
    w_col = lambda k, off=0: pl.BlockSpec((k, tn), lambda i, j: (0, j + off))
    out_tile = pl.BlockSpec((tm, tn), lambda i, j: (i, j))
    w_glu_b = w_glu[0].astype(bf16)
    gate = lambda col0: pl.BlockSpec((tm, tn), lambda i, j: (i, col0 + j))
    mix = pl.pallas_call(
        _mix_kernel, grid=grid,
        in_specs=[row_full(aw), w_col(aw), row_full(sw), w_col(sw), w_col(sw, nj), gate(ga_col0), gate(gb_col0)],
        out_specs=out_tile, out_shape=jax.ShapeDtypeStruct((t, d), bf16), compiler_params=sem, name="gated_mix",
    )(att, w_attn_branch[0].astype(bf16), ys, w_glu_b, w_glu_b, proj, proj)
    h1 = pl.pallas_call(
        _out_proj_kernel, grid=grid,
        in_specs=[row_full(d), w_col(d), out_tile],
        out_specs=out_tile, out_shape=jax.ShapeDtypeStruct((t, d), f32), compiler_params=sem, name="out_proj",
    )(mix, w_out[0].astype(bf16), x2)

    kk = jnp.stack([peer_k1[0], peer_k2[0]], axis=1).astype(bf16)
    assert kk.shape[2] == LANES and kk.shape[3] == LANES
    hn2, st = _peer_query(h1, norm2_g, peer_w_q[0].astype(bf16), kk, _tile(t, 512))
    r2, e2, n1, e1 = _peer_topk(st, _tile(t, 512))
    u_b = peer_u[0].astype(bf16)
    vt_b = _transpose_cast(peer_v[0], _tile(peer_v.shape[1], 512), bf16, "peer_v_transpose")
    out = _peer_dense(hn2, u_b, vt_b, r2, e2, n1, e1, h1, _tile(t, 512), _tile(u_b.shape[0], 1024))
    return out.reshape(batch, seq, d)
```

```python
import functools
import math

import jax
import jax.numpy as jnp
from jax import lax
from jax.experimental import pallas as pl
from jax.experimental.pallas import tpu as pltpu

EPS = 1e-6
ROPE_THETA = 10000.0
LAM_INIT = 0.8 - 0.6 * math.exp(-0.3 * 0)
TOPK = 16
LANES = 128
SUBLANES = 8
NEG_BIG = -1e30
VMEM_LIMIT = 56 * 1024 * 1024

bf16 = jnp.bfloat16
f32 = jnp.float32


def _cparams(sem):
    return pltpu.CompilerParams(dimension_semantics=sem, vmem_limit_bytes=VMEM_LIMIT)


def _dot(a, b):
    return jnp.dot(a, b, preferred_element_type=f32)


def _dot_nt(a, b):
    return lax.dot_general(a, b, (((1,), (1,)), ((), ())), preferred_element_type=f32)


def _sigmoid(x):
    return 1.0 / (1.0 + jnp.exp(-x))


def _gelu(x):
    return 0.5 * x * (1.0 + lax.erf(x * (1.0 / math.sqrt(2.0))))


def _stack_rows(rows):
    k = len(rows)
    idx = lax.broadcasted_iota(jnp.int32, (k, rows[0].shape[1]), 0)
    out = jnp.broadcast_to(rows[0], idx.shape)
    for j in range(1, k):
        out = jnp.where(idx == j, rows[j], out)
    return out


def _rms_matmul_kernel(x_ref, g_ref, w_ref, o_ref, xn_ref):
    @pl.when(pl.program_id(1) == 0)
    def _():
        x = x_ref[...]
        ms = jnp.mean(x * x, axis=-1, keepdims=True)
        xn_ref[...] = (x * lax.rsqrt(ms + EPS) * g_ref[...]).astype(bf16)

    o_ref[...] = _dot(xn_ref[...], w_ref[...]).astype(o_ref.dtype)


def _rms_matmul(x, g, w, tm, tn, out_dtype, name):
    m, k = x.shape
    n = w.shape[1]
    return pl.pallas_call(
        _rms_matmul_kernel,
        grid=(m // tm, n // tn),
        in_specs=[pl.BlockSpec((tm, k), lambda i, j: (i, 0)),
                  pl.BlockSpec((1, k), lambda i, j: (0, 0)),
                  pl.BlockSpec((k, tn), lambda i, j: (0, j))],
        out_specs=pl.BlockSpec((tm, tn), lambda i, j: (i, j)),
        out_shape=jax.ShapeDtypeStruct((m, n), out_dtype),
        scratch_shapes=[pltpu.VMEM((tm, k), bf16)],
        compiler_params=_cparams(("parallel", "arbitrary")),
        name=name,
    )(x, g, w)


def _normrot_kernel(x_ref, g_ref, cos_ref, sin_ref, j_ref, o_ref, *, scale, sub):
    width = x_ref.shape[1]
    cos = cos_ref[...]
    sin = sin_ref[...]
    jmat = j_ref[...]
    lane = lax.broadcasted_iota(jnp.int32, cos.shape, 1)
    first_half = (lane % sub) < (sub // 2)
    for t in range(width // LANES):
        sl = slice(t * LANES, (t + 1) * LANES)
        x = x_ref[:, sl].astype(f32)
        x2 = x * x
        hi = x2.astype(bf16)
        lo = (x2 - hi.astype(f32)).astype(bf16)
        ssq = _dot(hi, jmat) + _dot(lo, jmat)
        xn = x * lax.rsqrt(ssq * (1.0 / sub) + EPS) * g_ref[:, sl]
        partner = jnp.where(first_half, pltpu.roll(xn, LANES - sub // 2, 1), pltpu.roll(xn, sub // 2, 1))
        o_ref[:, sl] = ((xn * cos + partner * sin) * scale).astype(o_ref.dtype)


def _normrot(x, col_block, width, g_t, cos, sin, jmat, tm, scale, sub, name):
    m = x.shape[0]
    nt = cos.shape[0] // tm
    return pl.pallas_call(
        functools.partial(_normrot_kernel, scale=scale, sub=sub),
        grid=(m // tm,),
        in_specs=[pl.BlockSpec((tm, width), lambda i: (i, col_block)),
                  pl.BlockSpec((1, width), lambda i: (0, 0)),
                  pl.BlockSpec((tm, LANES), lambda i: (i % nt, 0)),
                  pl.BlockSpec((tm, LANES), lambda i: (i % nt, 0)),
                  pl.BlockSpec((LANES, LANES), lambda i: (0, 0))],
        out_specs=pl.BlockSpec((tm, width), lambda i: (i, 0)),
        out_shape=jax.ShapeDtypeStruct((m, width), bf16),
        compiler_params=_cparams(("parallel",)),
        name=name,
    )(x, g_t, cos, sin, jmat)


def _attn_kernel(q_ref, k_ref, v_ref, km_ref, vm_ref, lq1_ref, lk1_ref, lq2_ref, lk2_ref, sg_ref,
                 o_ref, qq_scr, m_scr, acc_scr, s_scr, *, n_meta, sub, hpb):
    i = pl.program_id(2)
    tq = q_ref.shape[0]
    tk = tq
    hw = 2 * sub
    mp = km_ref.shape[0]
    lane = lax.broadcasted_iota(jnp.int32, (tq, hw), 1)
    head = lambda hh: slice(hh * hw, (hh + 1) * hw)

    def softmax_step(hh, s, v_aug, first):
        smax = jnp.max(s, axis=1, keepdims=True)
        if first:
            m_new = jnp.broadcast_to(smax, (2 * tq, hw))
        else:
            m_prev = m_scr[hh]
            m_new = jnp.maximum(m_prev, smax)
        p = jnp.concatenate([jnp.exp2(s[:, c * hw:(c + 1) * hw] - m_new) for c in range(s.shape[1] // hw)], axis=1)
        pv = _dot(p.astype(bf16), v_aug)
        if first:
            acc_scr[hh] = pv
        else:
            alpha = jnp.exp2(m_prev - m_new)
            acc_scr[hh] = jnp.concatenate([alpha, alpha], axis=1) * acc_scr[hh] + pv
        m_scr[hh] = m_new

    for hh in range(hpb):
        q = q_ref[:, head(hh)]
        zero = jnp.zeros_like(q)
        qq_scr[hh] = jnp.concatenate([jnp.where(lane < sub, q, zero), jnp.where(lane >= sub, q, zero)], axis=0)

    def put_scores(j, slot):
        start = pl.multiple_of(j * tk, tk)
        for hh in range(hpb):
            s_scr[slot, hh] = _dot_nt(qq_scr[hh], k_ref[pl.ds(start, tk), head(hh)])

    def consume(j, slot, masked):
        start = pl.multiple_of(j * tk, tk)
        for hh in range(hpb):
            s = s_scr[slot, hh]
            if masked:
                row = lax.broadcasted_iota(jnp.int32, s.shape, 0) % tq
                col = lax.broadcasted_iota(jnp.int32, s.shape, 1)
                s = jnp.where(col <= row, s, NEG_BIG)
            vb = v_ref[pl.ds(start, tk), head(hh)]
            softmax_step(hh, s, jnp.concatenate([vb, jnp.ones((tk, hw), bf16)], axis=1), False)

    def stage(j, slot):
        put_scores(j + 1, 1 - slot)
        consume(j, slot, False)

    put_scores(0, 0)

    for hh in range(hpb):
        s = _dot_nt(qq_scr[hh], km_ref[:, head(hh)])
        col = lax.broadcasted_iota(jnp.int32, s.shape, 1)
        s = jnp.where(col < n_meta, s, NEG_BIG)
        softmax_step(hh, s, jnp.concatenate([vm_ref[:, head(hh)], jnp.ones((mp, hw), bf16)], axis=1), True)

    def body(p, c):
        stage(2 * p, 0)
        stage(2 * p + 1, 1)
        return c

    lax.fori_loop(0, i // 2, body, 0)

    @pl.when(i % 2 == 0)
    def _():
        consume(i, 0, True)

    @pl.when(i % 2 == 1)
    def _():
        stage(i - 1, 0)
        consume(i, 1, True)

    lam = (jnp.exp(jnp.sum(lq1_ref[...] * lk1_ref[...], axis=1, keepdims=True))
           - jnp.exp(jnp.sum(lq2_ref[...] * lk2_ref[...], axis=1, keepdims=True)) + LAM_INIT)
    for hh in range(hpb):
        acc = acc_scr[hh]
        o = acc[:, :hw] / acc[:, hw:]
        att = o[:tq] - lam * o[tq:]
        ms = jnp.mean(att * att, axis=-1, keepdims=True)
        att = att * lax.rsqrt(ms + EPS) * sg_ref[...] * (1.0 - LAM_INIT)
        o_ref[:, head(hh)] = att.astype(o_ref.dtype)


def _attention(q_rot, k_rot, proj, v_col0, km, vm, lq1, lk1, lq2, lk2, sg, batch, seq, heads, tq, n_meta, sub):
    t, aw = q_rot.shape
    nq = seq // tq
    hw = 2 * sub
    hpb = 2 if heads % 2 == 0 else 1
    bw = hpb * hw
    mp = km.shape[0]
    vec = lambda: pl.BlockSpec((1, sub), lambda b, h, i: (0, 0))
    return pl.pallas_call(
        functools.partial(_attn_kernel, n_meta=n_meta, sub=sub, hpb=hpb),
        grid=(batch, heads // hpb, nq),
        in_specs=[pl.BlockSpec((tq, bw), lambda b, h, i: (b * nq + i, h)),
                  pl.BlockSpec((seq, bw), lambda b, h, i: (b, h)),
                  pl.BlockSpec((seq, bw), lambda b, h, i: (b, v_col0 // hpb + h)),
                  pl.BlockSpec((mp, bw), lambda b, h, i: (0, h)),
                  pl.BlockSpec((mp, bw), lambda b, h, i: (0, h)),
                  vec(), vec(), vec(), vec(),
                  pl.BlockSpec((1, hw), lambda b, h, i: (0, 0))],
        out_specs=pl.BlockSpec((tq, bw), lambda b, h, i: (b * nq + i, h)),
        out_shape=jax.ShapeDtypeStruct((t, aw), bf16),
        scratch_shapes=[pltpu.VMEM((hpb, 2 * tq, hw), bf16), pltpu.VMEM((hpb, 2 * tq, hw), f32),
                        pltpu.VMEM((hpb, 2 * tq, 2 * hw), f32), pltpu.VMEM((2, hpb, 2 * tq, tq), f32)],
        compiler_params=_cparams(("parallel", "parallel", "arbitrary")),
        name="diff_attention",
    )(q_rot, k_rot, proj, km, vm, lq1, lk1, lq2, lk2, sg)


SSM_BLOCK = 8


def _cmul(ar, ai, br, bi):
    return ar * br - ai * bi, ar * bi + ai * br


def _dot3(a, b):
    ah = a.astype(bf16)
    al = (a - ah.astype(f32)).astype(bf16)
    bh = b.astype(bf16)
    bl = (b - bh.astype(f32)).astype(bf16)
    return _dot(ah, bh) + (_dot(ah, bl) + _dot(al, bh))


def _abar(a_re, a_im, log_dt):
    dt = jnp.exp(log_dt)
    er = jnp.exp(a_re * dt)
    return er * jnp.cos(a_im * dt), er * jnp.sin(a_im * dt)


def _powers(ar, ai, n):
    out = [(jnp.ones_like(ar), jnp.zeros_like(ai)), (ar, ai)]
    for _ in range(2, n + 1):
        out.append(_cmul(*out[-1], ar, ai))
    return out


def _ssm_prep_kernel(ar_ref, ai_ref, ld_ref, ac_ref, aic_ref, ldc_ref, as_ref, ais_ref, lds_ref,
                     br_ref, bi_ref, cr_ref, ci_ref, w_ref, q_ref, k_ref, pr_ref, ps_ref, *, n_state, n_chan):
    blk = SSM_BLOCK
    gpt = LANES // n_chan
    a_re, a_im = ar_ref[0], ai_ref[0]
    abr, abi = _abar(a_re, a_im, ld_ref[0])
    den = a_re * a_re + a_im * a_im
    nr, ni = abr - 1.0, abi
    f_re = (nr * a_re + ni * a_im) / den
    f_im = (ni * a_re - nr * a_im) / den
    bbr, bbi = _cmul(f_re, f_im, br_ref[0], bi_ref[0])
    pw = _powers(abr, abi, blk - 1)
    is_re = lax.broadcasted_iota(jnp.int32, a_re.shape, 1) < n_state
    rgrp = lax.broadcasted_iota(jnp.int32, a_re.shape, 0) // n_chan
    wf = []
    for i in range(blk):
        xr, xi = _cmul(bbr, bbi, *pw[blk - 1 - i])
        tile = jnp.where(is_re, xr, xi)
        wf.append(jnp.concatenate([jnp.where(rgrp == g, tile, 0.0) for g in range(gpt)], axis=1))
        w_ref[0, i] = wf[i].astype(bf16)
    c_re, c_im = cr_ref[0], ci_ref[0]
    acr, aci = _abar(ac_ref[0], aic_ref[0], ldc_ref[0])
    pwc = _powers(acr, aci, blk)
    cgrp = lax.broadcasted_iota(jnp.int32, c_re.shape, 1) // n_chan

    def by_group(tr, ti):
        rows = []
        for g in range(gpt):
            rows += [jnp.where(cgrp == g, tr, 0.0), jnp.where(cgrp == g, ti, 0.0)]
        return jnp.concatenate(rows, axis=0)

    for j in range(blk):
        qr, qi = _cmul(c_re, c_im, *pwc[j + 1])
        q_ref[0, :, j * LANES:(j + 1) * LANES] = by_group(qr, -qi).astype(bf16)
    ccat = by_group(c_re, -c_im)
    kd = [_dot3(wf[blk - 1 - dd], ccat).astype(bf16) for dd in range(blk)]
    for i in range(blk):
        for j in range(blk):
            tile = kd[j - i] if j >= i else jnp.zeros((LANES, LANES), bf16)
            k_ref[0, i * LANES:(i + 1) * LANES, j * LANES:(j + 1) * LANES] = tile
    asr, asi = _abar(as_ref[0], ais_ref[0], lds_ref[0])
    a8 = _powers(asr, asi, blk)[blk]
    pl_ = _powers(a8[0], a8[1], SUBLANES)[1:]
    re_lane = lax.broadcasted_iota(jnp.int32, asr.shape, 1) % LANES < n_state
    pr_ref[0] = _stack_rows([p[0] for p in pl_])
    ps_ref[0] = _stack_rows([jnp.where(re_lane, -p[1], p[1]) for p in pl_])


def _ssm_prep(rows, cols, scan, b_t, c_t, n_state, n_chan):
    ngt = b_t[0].shape[0]
    blk = SSM_BLOCK
    sl = scan[0].shape[2]
    spec3 = lambda a: pl.BlockSpec((1,) + a.shape[1:], lambda g: (g, 0, 0))
    ins = list(rows) + list(cols) + list(scan) + list(b_t) + list(c_t)
    w_s = jax.ShapeDtypeStruct((ngt, blk, LANES, sl), bf16)
    q_s = jax.ShapeDtypeStruct((ngt, sl, blk * LANES), bf16)
    k_s = jax.ShapeDtypeStruct((ngt, blk * LANES, blk * LANES), bf16)
    t_s = jax.ShapeDtypeStruct((ngt, SUBLANES, sl), f32)
    spec4 = lambda a: pl.BlockSpec((1,) + a.shape[1:], lambda g: (g, 0, 0, 0))
    return pl.pallas_call(
        functools.partial(_ssm_prep_kernel, n_state=n_state, n_chan=n_chan),
        grid=(ngt,),
        in_specs=[spec3(a) for a in ins],
        out_specs=[spec4(w_s), spec3(q_s), spec3(k_s), spec3(t_s), spec3(t_s)],
        out_shape=[w_s, q_s, k_s, t_s, t_s],
        compiler_params=_cparams(("parallel",)),
        name="ssm_prep",
    )(*ins)


def _ssm_kernel(u_ref, um_ref, w_ref, q_ref, k_ref, pr_ref, ps_ref, d_ref, y_ref,
                uf_scr, umf_scr, s_scr, sw_scr, xp_scr, yf_scr, *, rows_per_seq, meta_rows, n_state):
    blk = SSM_BLOCK
    r = u_ref.shape[0] // blk
    sl = s_scr.shape[1]
    tiles = sl // LANES
    pr, ps = pr_ref[0], ps_ref[0]

    def swap(x):
        return jnp.concatenate([pltpu.roll(x[:, c * LANES:(c + 1) * LANES], n_state, 1) for c in range(tiles)],
                               axis=1)

    uf_scr[...] = u_ref[...].astype(f32)
    umf_scr[...] = jnp.zeros_like(umf_scr)
    umf_scr[0:um_ref.shape[0], :] = um_ref[...].astype(f32)
    step_rows = lambda i: uf_scr[pl.ds(i, r, stride=blk), :]
    ucat = jnp.concatenate([step_rows(i).astype(bf16) for i in range(blk)], axis=1)
    umcat = jnp.concatenate([umf_scr[pl.ds(i, SUBLANES, stride=blk), :].astype(bf16) for i in range(blk)], axis=1)
    s = _dot(ucat, w_ref[0])
    sm = _dot(umcat, w_ref[0])
    rowmod = lax.broadcasted_iota(jnp.int32, s.shape, 0) % SUBLANES
    for k in (1, 2, 4):
        t = pltpu.roll(s, k, 0)
        t = t * pr[k - 1:k] + swap(t) * ps[k - 1:k]
        s = s + jnp.where(rowmod >= k, t, 0.0)
    s_scr[...] = s
    sw_scr[...] = swap(s)
    x0 = sm[0:1]
    for mrow in range(1, meta_rows):
        x0 = x0 * pr[0:1] + swap(x0) * ps[0:1] + sm[mrow:mrow + 1]
    x0w = swap(x0)
    row8 = lax.broadcasted_iota(jnp.int32, (SUBLANES, sl), 0)

    def body(b, carry):
        start = b % (rows_per_seq // SUBLANES) == 0
        rows = pl.ds(pl.multiple_of(b * SUBLANES, SUBLANES), SUBLANES)
        c = jnp.broadcast_to(jnp.where(start, x0, carry[0]), (SUBLANES, sl))
        cw = jnp.broadcast_to(jnp.where(start, x0w, carry[1]), (SUBLANES, sl))
        xs = s_scr[rows, :] + (c * pr + cw * ps)
        xw = sw_scr[rows, :] + (cw * pr - c * ps)
        xp_scr[rows, :] = jnp.where(row8 == 0, c, pltpu.roll(xs, 1, 0))
        return xs[SUBLANES - 1:SUBLANES], xw[SUBLANES - 1:SUBLANES]

    lax.fori_loop(0, r // SUBLANES, body, (x0, x0w))

    y_all = _dot(xp_scr[...].astype(bf16), q_ref[0]) + _dot(ucat, k_ref[0])
    for j in range(blk):
        y = y_all[:, j * LANES:(j + 1) * LANES] + d_ref[...] * step_rows(j)
        yf_scr[pl.ds(j, r, stride=blk), :] = _gelu(y)
    y_ref[...] = yf_scr[...].astype(y_ref.dtype)


def _ssm(u_arr, u_col0, um, prep, d, rows_per_seq, n_state):
    w, q, k, pr, ps = prep
    ngt, blk, _, sl = w.shape
    w = w.reshape(ngt, blk * LANES, sl)
    t = u_arr.shape[0]
    n_meta = um.shape[0]
    spec3 = lambda a: pl.BlockSpec((1,) + a.shape[1:], lambda g: (g, 0, 0))
    return pl.pallas_call(
        functools.partial(_ssm_kernel, rows_per_seq=rows_per_seq, meta_rows=n_meta // blk, n_state=n_state),
        grid=(ngt,),
        in_specs=[pl.BlockSpec((t, LANES), lambda g: (0, u_col0 + g)),
                  pl.BlockSpec((n_meta, LANES), lambda g: (0, g)),
                  spec3(w), spec3(q), spec3(k), spec3(pr), spec3(ps),
                  pl.BlockSpec((1, LANES), lambda g: (0, g))],
        out_specs=pl.BlockSpec((t, LANES), lambda g: (0, g)),
        out_shape=jax.ShapeDtypeStruct((t, ngt * LANES), bf16),
        scratch_shapes=[pltpu.VMEM((t, LANES), f32), pltpu.VMEM((SUBLANES * blk, LANES), f32),
                        pltpu.VMEM((t // blk, sl), f32), pltpu.VMEM((t // blk, sl), f32),
                        pltpu.VMEM((t // blk, sl), f32), pltpu.VMEM((t, LANES), f32)],
        compiler_params=_cparams(("parallel",)),
        name="ssm_blocked",
    )(u_arr, um, w, q, k, pr, ps, d)


def _mix_kernel(a_ref, wattn_ref, s_ref, wa_ref, wb_ref, ga_ref, gb_ref, o_ref):
    ya = _dot(a_ref[...], wattn_ref[...])
    s = s_ref[...]
    yb = _dot(s, wa_ref[...]) * _sigmoid(_dot(s, wb_ref[...]))
    o_ref[...] = (_sigmoid(ga_ref[...].astype(f32)) * ya
                  + _sigmoid(gb_ref[...].astype(f32)) * yb).astype(o_ref.dtype)


def _transpose_cast_kernel(x_ref, o_ref):
    o_ref[...] = x_ref[...].T.astype(o_ref.dtype)


def _transpose_cast(x, tr, dtype, name):
    rows, cols = x.shape
    return pl.pallas_call(
        _transpose_cast_kernel,
        grid=(rows // tr,),
        in_specs=[pl.BlockSpec((tr, cols), lambda i: (i, 0))],
        out_specs=pl.BlockSpec((cols, tr), lambda i: (0, i)),
        out_shape=jax.ShapeDtypeStruct((cols, rows), dtype),
        compiler_params=_cparams(("parallel",)),
        name=name,
    )(x)


def _out_proj_kernel(m_ref, w_ref, x_ref, o_ref):
    o_ref[...] = x_ref[...] + _dot(m_ref[...], w_ref[...])


def _peer_query_kernel(h_ref, g_ref, wq_ref, kk_ref, hn_ref, st_ref):
    x = h_ref[...]
    ms = jnp.mean(x * x, axis=-1, keepdims=True)
    hn_f = x * lax.rsqrt(ms + EPS) * g_ref[...]
    hn_ref[...] = hn_f.T.astype(bf16)
    q = _dot(hn_f.astype(bf16), wq_ref[...]).astype(bf16)
    n_heads = kk_ref.shape[0]
    half = kk_ref.shape[3]
    for h in range(n_heads):
        for side in range(2):
            c0 = (2 * h + side) * half
            st_ref[h, side] = _dot_nt(kk_ref[h, side], q[:, c0:c0 + half])


def _peer_query(h1, g, wq, kk, tm):
    t, d = h1.shape
    nh, _, nk, half = kk.shape
    return pl.pallas_call(
        _peer_query_kernel,
        grid=(t // tm,),
        in_specs=[pl.BlockSpec((tm, d), lambda i: (i, 0)),
                  pl.BlockSpec((1, d), lambda i: (0, 0)),
                  pl.BlockSpec(wq.shape, lambda i: (0, 0)),
                  pl.BlockSpec(kk.shape, lambda i: (0, 0, 0, 0))],
        out_specs=[pl.BlockSpec((d, tm), lambda i: (0, i)),
                   pl.BlockSpec((nh, 2, nk, tm), lambda i: (0, 0, 0, i))],
        out_shape=[jax.ShapeDtypeStruct((d, t), bf16), jax.ShapeDtypeStruct((nh, 2, nk, t), f32)],
        compiler_params=_cparams(("parallel",)),
        name="peer_query",
    )(h1, g, wq, kk)


def _sort_pairs(n):
    pairs = []
    p = 1
    while p < n:
        k = p
        while k >= 1:
            for j in range(k % p, n - k, 2 * k):
                for i in range(min(k, n - j - k)):
                    if (i + j) // (2 * p) == (i + j + k) // (2 * p):
                        pairs.append((i + j, i + j + k))
            k //= 2
        p *= 2
    return pairs


def _top_desc(tiles, n):
    m = len(tiles)
    vs = list(tiles)
    for i, j in _sort_pairs(1 << (m - 1).bit_length()):
        if j < m:
            vs[i], vs[j] = jnp.maximum(vs[i], vs[j]), jnp.minimum(vs[i], vs[j])
    neg = jnp.full(vs[0].shape, -jnp.inf, f32)
    vs = vs[:n] + [neg]
    vals = []
    for a in range(n):
        v = jnp.max(vs[0], axis=0, keepdims=True)
        vals.append(v)
        hit = vs[0] == v
        keep = min(len(vs) - 1, n - a - 1)
        vs = [jnp.where(hit, vs[k + 1], vs[k]) for k in range(keep)] + [neg]
    return _stack_rows(vals)


def _peer_topk_kernel(st_ref, r2_ref, e2_ref, n1_ref, e1_ref):
    chunk = LANES
    sub8 = lambda x: [x[k * SUBLANES:(k + 1) * SUBLANES] for k in range(x.shape[0] // SUBLANES)]
    for c in range(st_ref.shape[3] // chunk):
        cols = slice(c * chunk, (c + 1) * chunk)
        s1 = st_ref[0, 0, :, cols]
        s2 = st_ref[0, 1, :, cols]
        v1 = _top_desc(sub8(s1), TOPK)
        v2 = _top_desc(sub8(s2), TOPK)
        half = TOPK // 2
        cand = ([v1[0:1] + v2[:half], v1[0:1] + v2[half:]] + [v1[a:a + 1] + v2[:half] for a in range(1, half)]
                + [v1[half:] + v2[0:1]])
        best = _top_desc(cand, TOPK)
        tau = best[TOPK - 1:TOPK]
        z = jnp.sum(jnp.exp(best - best[0:1]), axis=0, keepdims=True)
        n1 = jnp.zeros(s1.shape, f32)
        rank2 = jnp.zeros(s2.shape, f32)
        for a in range(TOPK):
            cnt = jnp.sum(jnp.where((v1[a:a + 1] + v2) >= tau, 1.0, 0.0), axis=0, keepdims=True)
            n1 = jnp.where(s1 == v1[a:a + 1], cnt, n1)
            rank2 = jnp.where(s2 < v2[a:a + 1], a + 1.0, rank2)
        r2_ref[0, :, cols] = rank2.astype(bf16)
        e2_ref[0, :, cols] = jnp.exp(s2 - v2[0:1]).astype(bf16)
        n1_ref[0, :, cols] = n1
        e1_ref[0, :, cols] = jnp.exp(s1 - v1[0:1]) / z


def _peer_topk(st, tl):
    nh, _, nk, t = st.shape
    spec = lambda: pl.BlockSpec((1, nk, tl), lambda h, i: (h, 0, i))
    shp = lambda dt: jax.ShapeDtypeStruct((nh, nk, t), dt)
    return pl.pallas_call(
        _peer_topk_kernel,
        grid=(nh, t // tl),
        in_specs=[pl.BlockSpec((1, 2, nk, tl), lambda h, i: (h, 0, 0, i))],
        out_specs=[spec(), spec(), spec(), spec()],
        out_shape=[shp(bf16), shp(bf16), shp(f32), shp(f32)],
        compiler_params=_cparams(("parallel", "parallel")),
        name="peer_topk",
    )(st)


def _peer_dense_kernel(hn_ref, u_ref, vt_ref, r2_ref, e2_ref, n1_ref, e1_ref, h1_ref, o_ref,
                       acc_scr, a0_scr, a1_scr, *, nb, tn):
    e = pl.program_id(1)
    n_heads, nk, tm = r2_ref.shape
    eb = u_ref.shape[0]
    pk = 2 * SUBLANES

    @pl.when(e == 0)
    def _():
        acc_scr[...] = jnp.zeros_like(acc_scr)
        a1_scr[...] = jnp.zeros_like(a1_scr)

    def step(a_cur, a_prev):
        a_cur[...] = _dot(u_ref[...], hn_ref[...])
        blk = jnp.maximum(e - 1, 0)
        for c in range(tm // tn):
            cols = slice(c * tn, (c + 1) * tn)
            w_rows = []
            for ii in range(eb // nk):
                i1 = blk * (eb // nk) + ii
                gate = None
                for h in range(n_heads):
                    n1 = jnp.broadcast_to(n1_ref[h, pl.ds(i1, 1), cols], (pk, tn)).astype(bf16)
                    e1 = jnp.broadcast_to(e1_ref[h, pl.ds(i1, 1), cols], (pk, tn)).astype(bf16)
                    r2 = r2_ref[h, :, cols].reshape(nk // pk, pk, tn)
                    e2 = e2_ref[h, :, cols].reshape(nk // pk, pk, tn)
                    term = jnp.where(r2 < n1[None], e2, jnp.zeros_like(e2)) * e1[None]
                    gate = term if gate is None else gate + term
                g = _gelu(a_prev[ii * nk:(ii + 1) * nk, cols]).astype(bf16)
                w_rows.append(gate.reshape(nk, tn) * g)
            acc_scr[:, cols] += _dot(vt_ref[...], jnp.concatenate(w_rows, axis=0))

    @pl.when(e % 2 == 0)
    def _():
        step(a0_scr, a1_scr)

    @pl.when(e % 2 == 1)
    def _():
        step(a1_scr, a0_scr)

    @pl.when(e == nb)
    def _():
        o_ref[...] = h1_ref[...] + acc_scr[...].T


def _peer_dense(hn_t, u_b, vt_b, r2, e2, n1, e1, h1, tm, eb):
    d, t = hn_t.shape
    nb = u_b.shape[0] // eb
    nh, nk, _ = r2.shape
    tok = lambda: pl.BlockSpec((nh, nk, tm), lambda i, e: (0, 0, i), pipeline_mode=pl.Buffered(1))
    return pl.pallas_call(
        functools.partial(_peer_dense_kernel, nb=nb, tn=_tile(tm, 2 * LANES)),
        grid=(t // tm, nb + 1),
        in_specs=[pl.BlockSpec((d, tm), lambda i, e: (0, i)),
                  pl.BlockSpec((eb, d), lambda i, e: (jnp.minimum(e, nb - 1), 0)),
                  pl.BlockSpec((d, eb), lambda i, e: (0, jnp.maximum(e - 1, 0))),
                  tok(), tok(), tok(), tok(),
                  pl.BlockSpec((tm, d), lambda i, e: (i, 0), pipeline_mode=pl.Buffered(1))],
        out_specs=pl.BlockSpec((tm, d), lambda i, e: (i, 0)),
        out_shape=jax.ShapeDtypeStruct((t, d), f32),
        scratch_shapes=[pltpu.VMEM((d, tm), f32), pltpu.VMEM((eb, tm), f32), pltpu.VMEM((eb, tm), f32)],
        compiler_params=_cparams(("parallel", "arbitrary")),
        name="peer_dense",
    )(hn_t, u_b, vt_b, r2, e2, n1, e1, h1)


def _tile(n, pref):
    if n <= pref:
        return n
    t = pref // LANES * LANES
    while n % t:
        t -= LANES
    assert t > 0, (n, pref)
    return t


def kernel(x, meta_tokens, norm1_g, w_in, q_norm_g, k_norm_g, lambda_q1, lambda_k1, lambda_q2, lambda_k2, subln_g, w_attn_branch, ssm_a_re, ssm_a_im, ssm_log_dt, ssm_b_re, ssm_b_im, ssm_c_re, ssm_c_im, ssm_d, w_glu, w_out, norm2_g, peer_w_q, peer_k1, peer_k2, peer_u, peer_v):
    assert norm1_g.shape[0] == 1, "single-layer block only"
    batch, seq, d = x.shape
    n_meta = meta_tokens.shape[0]
    sub = q_norm_g.shape[-1]
    aw = w_attn_branch.shape[1]
    heads = aw // (2 * sub)
    n_groups, n_state, n_chan = ssm_b_re.shape[1:]
    sw = n_groups * n_chan
    assert 2 * sub == LANES and LANES % n_chan == 0 and sw % LANES == 0
    assert n_meta % SUBLANES == 0 and n_meta <= LANES
    t = batch * seq
    n_in = w_in.shape[2]
    assert n_in == 3 * aw + sw + 2 * d

    x2 = x.reshape(t, d)
    w_in_b = w_in[0].astype(bf16)

    proj = _rms_matmul(x2, norm1_g, w_in_b, _tile(t, 1024), _tile(n_in, 2048), bf16, "in_proj")
    proj_m = _rms_matmul(meta_tokens, norm1_g, w_in_b, n_meta, _tile(n_in, 1024), bf16, "in_proj_meta")

    half = sub // 2
    inv_freq = jnp.power(ROPE_THETA, -jnp.arange(half, dtype=f32) / half)
    ang = jnp.arange(n_meta + seq, dtype=f32)[:, None] * inv_freq[None, :]
    cos_t = jnp.tile(jnp.cos(ang), (1, LANES // half))
    sin_t = jnp.tile(jnp.concatenate([-jnp.sin(ang), jnp.sin(ang)], axis=1), (1, LANES // sub))
    lane = jnp.arange(LANES)
    jmat = (lane[:, None] // sub == lane[None, :] // sub).astype(bf16)
    gq = jnp.tile(q_norm_g, (1, aw // sub))
    gk = jnp.tile(k_norm_g, (1, aw // sub))
    tr = _tile(seq, 512)
    q_rot = _normrot(proj, 0, aw, gq, cos_t[n_meta:], sin_t[n_meta:], jmat, tr, sub ** -0.5 * math.log2(math.e), sub,
                      "q_normrot")
    k_rot = _normrot(proj, 1, aw, gk, cos_t[n_meta:], sin_t[n_meta:], jmat, tr, 1.0, sub, "k_normrot")
    km_rot = _normrot(proj_m, 1, aw, gk, cos_t[:n_meta], sin_t[:n_meta], jmat, n_meta, 1.0, sub, "k_normrot_meta")
    km = jnp.pad(km_rot, ((0, LANES - n_meta), (0, 0)))
    vm = jnp.pad(proj_m[:, 2 * aw:3 * aw], ((0, LANES - n_meta), (0, 0)))

    att = _attention(q_rot, k_rot, proj, 2 * aw // LANES, km, vm, lambda_q1, lambda_k1, lambda_q2, lambda_k2,
                     subln_g, batch, seq, heads, _tile(seq, 512), n_meta, sub)

    blk = SSM_BLOCK
    gpt = LANES // n_chan
    ngt = sw // LANES
    assert 2 * n_state == LANES and seq % (blk * SUBLANES) == 0 and n_meta % blk == 0
    grp = lambda a: a.reshape(ngt, gpt, n_state)
    row_l = lambda a: jnp.tile(jnp.repeat(grp(a), n_chan, axis=1), (1, 1, 2))
    col_l = lambda a: jnp.repeat(grp(a).transpose(0, 2, 1), n_chan, axis=2)
    scan_l = lambda a: jnp.tile(grp(a), (1, 1, 2)).reshape(ngt, 1, gpt * LANES)
    ldt2 = jnp.broadcast_to(ssm_log_dt[0][:, None], (n_groups, n_state))
    lays = lambda f: (f(ssm_a_re[0]), f(ssm_a_im[0]), f(ldt2))
    b_l = lambda b: jnp.tile(b[0].transpose(0, 2, 1).reshape(ngt, LANES, n_state), (1, 1, 2))
    c_l = lambda c: c[0].reshape(ngt, gpt, n_chan, n_state).transpose(0, 3, 1, 2).reshape(ngt, n_state, LANES)
    prep = _ssm_prep(lays(row_l), lays(col_l), lays(scan_l), (b_l(ssm_b_re), b_l(ssm_b_im)),
                     (c_l(ssm_c_re), c_l(ssm_c_im)), n_state, n_chan)
    u0 = 3 * aw
    ys = _ssm(proj, u0 // LANES, proj_m[:, u0:u0 + sw], prep, ssm_d, seq // blk, n_state)

    tm = _tile(t, 1024)
    tn = _tile(d, 1024)
    nj = d // tn
    ga_col0 = (3 * aw + sw) // tn
    gb_col0 = (3 * aw + sw + d) // tn
    grid = (t // tm, nj)
    sem = _cparams(("parallel", "arbitrary"))
    row_full = lambda k: pl.BlockSpec((tm, k), lambda i, j: (i, 0))
    w_col = lambda k, off=0: pl.BlockSpec((k, tn), lambda i, j: (0, j + off))
    out_tile = pl.BlockSpec((tm, tn), lambda i, j: (i, j))
    w_glu_b = w_glu[0].astype(bf16)
    gate = lambda col0: pl.BlockSpec((tm, tn), lambda i, j: (i, col0 + j))
    mix = pl.pallas_call(
        _mix_kernel, grid=grid,
        in_specs=[row_full(aw), w_col(aw), row_full(sw), w_col(sw), w_col(sw, nj), gate(ga_col0), gate(gb_col0)],
        out_specs=out_tile, out_shape=jax.ShapeDtypeStruct((t, d), bf16), compiler_params=sem, name="gated_mix",
    )(att, w_attn_branch[0].astype(bf16), ys, w_glu_b, w_glu_b, proj, proj)
    h1 = pl.pallas_call(
        _out_proj_kernel, grid=grid,
        in_specs=[row_full(d), w_col(d), out_tile],
        out_specs=out_tile, out_shape=jax.ShapeDtypeStruct((t, d), f32), compiler_params=sem, name="out_proj",
    )(mix, w_out[0].astype(bf16), x2)

    kk = jnp.stack([peer_k1[0], peer_k2[0]], axis=1).astype(bf16)
    assert kk.shape[2] == LANES and kk.shape[3] == LANES
    hn2, st = _peer_query(h1, norm2_g, peer_w_q[0].astype(bf16), kk, _tile(t, 512))
    r2, e2, n1, e1 = _peer_topk(st, _tile(t, 1024))
    u_b = peer_u[0].astype(bf16)
    vt_b = _transpose_cast(peer_v[0], _tile(peer_v.shape[1], 512), bf16, "peer_v_transpose")
    out = _peer_dense(hn2, u_b, vt_b, r2, e2, n1, e1, h1, _tile(t, 512), _tile(u_b.shape[0], 1024))
    return out.reshape(batch, seq, d)
```

```python
import functools
import math

import jax
import jax.numpy as jnp
from jax import lax
from jax.experimental import pallas as pl
from jax.experimental.pallas import tpu as pltpu

EPS = 1e-6
ROPE_THETA = 10000.0
LAM_INIT = 0.8 - 0.6 * math.exp(-0.3 * 0)
TOPK = 16
LANES = 128
SUBLANES = 8
NEG_BIG = -1e30
VMEM_LIMIT = 56 * 1024 * 1024

bf16 = jnp.bfloat16
f32 = jnp.float32


def _cparams(sem):
    return pltpu.CompilerParams(dimension_semantics=sem, vmem_limit_bytes=VMEM_LIMIT)


def _dot(a, b):
    return jnp.dot(a, b, preferred_element_type=f32)


def _dot_nt(a, b):
    return lax.dot_general(a, b, (((1,), (1,)), ((), ())), preferred_element_type=f32)


def _sigmoid(x):
    return 1.0 / (1.0 + jnp.exp(-x))


def _gelu(x):
    return 0.5 * x * (1.0 + lax.erf(x * (1.0 / math.sqrt(2.0))))


def _stack_rows(rows):
    k = len(rows)
    idx = lax.broadcasted_iota(jnp.int32, (k, rows[0].shape[1]), 0)
    out = jnp.broadcast_to(rows[0], idx.shape)
    for j in range(1, k):
        out = jnp.where(idx == j, rows[j], out)
    return out


def _rms_matmul_kernel(x_ref, g_ref, w_ref, o_ref, xn_ref):
    @pl.when(pl.program_id(1) == 0)
    def _():
        x = x_ref[...]
        ms = jnp.mean(x * x, axis=-1, keepdims=True)
        xn_ref[...] = (x * lax.rsqrt(ms + EPS) * g_ref[...]).astype(bf16)

    o_ref[...] = _dot(xn_ref[...], w_ref[...]).astype(o_ref.dtype)


def _rms_matmul(x, g, w, tm, tn, out_dtype, name):
    m, k = x.shape
    n = w.shape[1]
    return pl.pallas_call(
        _rms_matmul_kernel,
        grid=(m // tm, n // tn),
        in_specs=[pl.BlockSpec((tm, k), lambda i, j: (i, 0)),
                  pl.BlockSpec((1, k), lambda i, j: (0, 0)),
                  pl.BlockSpec((k, tn), lambda i, j: (0, j))],
        out_specs=pl.BlockSpec((tm, tn), lambda i, j: (i, j)),
        out_shape=jax.ShapeDtypeStruct((m, n), out_dtype),
        scratch_shapes=[pltpu.VMEM((tm, k), bf16)],
        compiler_params=_cparams(("parallel", "arbitrary")),
        name=name,
    )(x, g, w)


def _normrot_kernel(x_ref, g_ref, cos_ref, sin_ref, j_ref, o_ref, *, scale, sub):
    width = x_ref.shape[1]
    cos = cos_ref[...]
    sin = sin_ref[...]
    jmat = j_ref[...]
    lane = lax.broadcasted_iota(jnp.int32, cos.shape, 1)
    first_half = (lane % sub) < (sub // 2)
    for t in range(width // LANES):
        sl = slice(t * LANES, (t + 1) * LANES)
        x = x_ref[:, sl].astype(f32)
        x2 = x * x
        hi = x2.astype(bf16)
        lo = (x2 - hi.astype(f32)).astype(bf16)
        ssq = _dot(hi, jmat) + _dot(lo, jmat)
        xn = x * lax.rsqrt(ssq * (1.0 / sub) + EPS) * g_ref[:, sl]
        partner = jnp.where(first_half, pltpu.roll(xn, LANES - sub // 2, 1), pltpu.roll(xn, sub // 2, 1))
        o_ref[:, sl] = ((xn * cos + partner * sin) * scale).astype(o_ref.dtype)


def _normrot(x, col_block, width, g_t, cos, sin, jmat, tm, scale, sub, name):
    m = x.shape[0]
    nt = cos.shape[0] // tm
    return pl.pallas_call(
        functools.partial(_normrot_kernel, scale=scale, sub=sub),
        grid=(m // tm,),
        in_specs=[pl.BlockSpec((tm, width), lambda i: (i, col_block)),
                  pl.BlockSpec((1, width), lambda i: (0, 0)),
                  pl.BlockSpec((tm, LANES), lambda i: (i % nt, 0)),
                  pl.BlockSpec((tm, LANES), lambda i: (i % nt, 0)),
                  pl.BlockSpec((LANES, LANES), lambda i: (0, 0))],
        out_specs=pl.BlockSpec((tm, width), lambda i: (i, 0)),
        out_shape=jax.ShapeDtypeStruct((m, width), bf16),
        compiler_params=_cparams(("parallel",)),
        name=name,
    )(x, g_t, cos, sin, jmat)


def _attn_kernel(q_ref, k_ref, v_ref, km_ref, vm_ref, lq1_ref, lk1_ref, lq2_ref, lk2_ref, sg_ref,
                 o_ref, qq_scr, m_scr, acc_scr, s_scr, *, n_meta, sub, hpb):
    i = pl.program_id(2)
    tq = q_ref.shape[0]
    tk = tq
    hw = 2 * sub
    mp = km_ref.shape[0]
    lane = lax.broadcasted_iota(jnp.int32, (tq, hw), 1)
    head = lambda hh: slice(hh * hw, (hh + 1) * hw)

    def softmax_step(hh, s, v_aug, first):
        smax = jnp.max(s, axis=1, keepdims=True)
        if first:
            m_new = jnp.broadcast_to(smax, (2 * tq, hw))
        else:
            m_prev = m_scr[hh]
            m_new = jnp.maximum(m_prev, smax)
        p = jnp.concatenate([jnp.exp2(s[:, c * hw:(c + 1) * hw] - m_new) for c in range(s.shape[1] // hw)], axis=1)
        pv = _dot(p.astype(bf16), v_aug)
        if first:
            acc_scr[hh] = pv
        else:
            alpha = jnp.exp2(m_prev - m_new)
            acc_scr[hh] = jnp.concatenate([alpha, alpha], axis=1) * acc_scr[hh] + pv
        m_scr[hh] = m_new

    for hh in range(hpb):
        q = q_ref[:, head(hh)]
        zero = jnp.zeros_like(q)
        qq_scr[hh] = jnp.concatenate([jnp.where(lane < sub, q, zero), jnp.where(lane >= sub, q, zero)], axis=0)

    def put_scores(j, slot):
        start = pl.multiple_of(j * tk, tk)
        for hh in range(hpb):
            s_scr[slot, hh] = _dot_nt(qq_scr[hh], k_ref[pl.ds(start, tk), head(hh)])

    def consume(j, slot, masked):
        start = pl.multiple_of(j * tk, tk)
        for hh in range(hpb):
            s = s_scr[slot, hh]
            if masked:
                row = lax.broadcasted_iota(jnp.int32, s.shape, 0) % tq
                col = lax.broadcasted_iota(jnp.int32, s.shape, 1)
                s = jnp.where(col <= row, s, NEG_BIG)
            vb = v_ref[pl.ds(start, tk), head(hh)]
            softmax_step(hh, s, jnp.concatenate([vb, jnp.ones((tk, hw), bf16)], axis=1), False)

    def stage(j, slot):
        put_scores(j + 1, 1 - slot)
        consume(j, slot, False)

    put_scores(0, 0)

    for hh in range(hpb):
        s = _dot_nt(qq_scr[hh], km_ref[:, head(hh)])
        col = lax.broadcasted_iota(jnp.int32, s.shape, 1)
        s = jnp.where(col < n_meta, s, NEG_BIG)
        softmax_step(hh, s, jnp.concatenate([vm_ref[:, head(hh)], jnp.ones((mp, hw), bf16)], axis=1), True)

    def body(p, c):
        stage(2 * p, 0)
        stage(2 * p + 1, 1)
        return c

    lax.fori_loop(0, i // 2, body, 0)

    @pl.when(i % 2 == 0)
    def _():
        consume(i, 0, True)

    @pl.when(i % 2 == 1)
    def _():
        stage(i - 1, 0)
        consume(i, 1, True)

    lam = (jnp.exp(jnp.sum(lq1_ref[...] * lk1_ref[...], axis=1, keepdims=True))
           - jnp.exp(jnp.sum(lq2_ref[...] * lk2_ref[...], axis=1, keepdims=True)) + LAM_INIT)
    for hh in range(hpb):
        acc = acc_scr[hh]
        o = acc[:, :hw] / acc[:, hw:]
        att = o[:tq] - lam * o[tq:]
        ms = jnp.mean(att * att, axis=-1, keepdims=True)
        att = att * lax.rsqrt(ms + EPS) * sg_ref[...] * (1.0 - LAM_INIT)
        o_ref[:, head(hh)] = att.astype(o_ref.dtype)


def _attention(q_rot, k_rot, proj, v_col0, km, vm, lq1, lk1, lq2, lk2, sg, batch, seq, heads, tq, n_meta, sub):
    t, aw = q_rot.shape
    nq = seq // tq
    hw = 2 * sub
    hpb = 2 if heads % 2 == 0 else 1
    bw = hpb * hw
    mp = km.shape[0]
    vec = lambda: pl.BlockSpec((1, sub), lambda b, h, i: (0, 0))
    return pl.pallas_call(
        functools.partial(_attn_kernel, n_meta=n_meta, sub=sub, hpb=hpb),
        grid=(batch, heads // hpb, nq),
        in_specs=[pl.BlockSpec((tq, bw), lambda b, h, i: (b * nq + i, h)),
                  pl.BlockSpec((seq, bw), lambda b, h, i: (b, h)),
                  pl.BlockSpec((seq, bw), lambda b, h, i: (b, v_col0 // hpb + h)),
                  pl.BlockSpec((mp, bw), lambda b, h, i: (0, h)),
                  pl.BlockSpec((mp, bw), lambda b, h, i: (0, h)),
                  vec(), vec(), vec(), vec(),
                  pl.BlockSpec((1, hw), lambda b, h, i: (0, 0))],
        out_specs=pl.BlockSpec((tq, bw), lambda b, h, i: (b * nq + i, h)),
        out_shape=jax.ShapeDtypeStruct((t, aw), bf16),
        scratch_shapes=[pltpu.VMEM((hpb, 2 * tq, hw), bf16), pltpu.VMEM((hpb, 2 * tq, hw), f32),
                        pltpu.VMEM((hpb, 2 * tq, 2 * hw), f32), pltpu.VMEM((2, hpb, 2 * tq, tq), f32)],
        compiler_params=_cparams(("parallel", "parallel", "arbitrary")),
        name="diff_attention",
    )(q_rot, k_rot, proj, km, vm, lq1, lk1, lq2, lk2, sg)


SSM_BLOCK = 8


def _cmul(ar, ai, br, bi):
    return ar * br - ai * bi, ar * bi + ai * br


def _dot3(a, b):
    ah = a.astype(bf16)
    al = (a - ah.astype(f32)).astype(bf16)
    bh = b.astype(bf16)
    bl = (b - bh.astype(f32)).astype(bf16)
    return _dot(ah, bh) + (_dot(ah, bl) + _dot(al, bh))


def _abar(a_re, a_im, log_dt):
    dt = jnp.exp(log_dt)
    er = jnp.exp(a_re * dt)
    return er * jnp.cos(a_im * dt), er * jnp.sin(a_im * dt)


def _powers(ar, ai, n):
    out = [(jnp.ones_like(ar), jnp.zeros_like(ai)), (ar, ai)]
    for _ in range(2, n + 1):
        out.append(_cmul(*out[-1], ar, ai))
    return out


def _ssm_prep_kernel(ar_ref, ai_ref, ld_ref, ac_ref, aic_ref, ldc_ref, as_ref, ais_ref, lds_ref,
                     br_ref, bi_ref, cr_ref, ci_ref, w_ref, q_ref, k_ref, pr_ref, ps_ref, *, n_state, n_chan):
    blk = SSM_BLOCK
    gpt = LANES // n_chan
    a_re, a_im = ar_ref[0], ai_ref[0]
    abr, abi = _abar(a_re, a_im, ld_ref[0])
    den = a_re * a_re + a_im * a_im
    nr, ni = abr - 1.0, abi
    f_re = (nr * a_re + ni * a_im) / den
    f_im = (ni * a_re - nr * a_im) / den
    bbr, bbi = _cmul(f_re, f_im, br_ref[0], bi_ref[0])
    pw = _powers(abr, abi, blk - 1)
    is_re = lax.broadcasted_iota(jnp.int32, a_re.shape, 1) < n_state
    rgrp = lax.broadcasted_iota(jnp.int32, a_re.shape, 0) // n_chan
    wf = []
    for i in range(blk):
        xr, xi = _cmul(bbr, bbi, *pw[blk - 1 - i])
        tile = jnp.where(is_re, xr, xi)
        wf.append(jnp.concatenate([jnp.where(rgrp == g, tile, 0.0) for g in range(gpt)], axis=1))
        w_ref[0, i] = wf[i].astype(bf16)
    c_re, c_im = cr_ref[0], ci_ref[0]
    acr, aci = _abar(ac_ref[0], aic_ref[0], ldc_ref[0])
    pwc = _powers(acr, aci, blk)
    cgrp = lax.broadcasted_iota(jnp.int32, c_re.shape, 1) // n_chan

    def by_group(tr, ti):
        rows = []
        for g in range(gpt):
            rows += [jnp.where(cgrp == g, tr, 0.0), jnp.where(cgrp == g, ti, 0.0)]
        return jnp.concatenate(rows, axis=0)

    for j in range(blk):
        qr, qi = _cmul(c_re, c_im, *pwc[j + 1])
        q_ref[0, :, j * LANES:(j + 1) * LANES] = by_group(qr, -qi).astype(bf16)
    ccat = by_group(c_re, -c_im)
    kd = [_dot3(wf[blk - 1 - dd], ccat).astype(bf16) for dd in range(blk)]
    for i in range(blk):
        for j in range(blk):
            tile = kd[j - i] if j >= i else jnp.zeros((LANES, LANES), bf16)
            k_ref[0, i * LANES:(i + 1) * LANES, j * LANES:(j + 1) * LANES] = tile
    asr, asi = _abar(as_ref[0], ais_ref[0], lds_ref[0])
    a8 = _powers(asr, asi, blk)[blk]
    pl_ = _powers(a8[0], a8[1], SUBLANES)[1:]
    re_lane = lax.broadcasted_iota(jnp.int32, asr.shape, 1) % LANES < n_state
    pr_ref[0] = _stack_rows([p[0] for p in pl_])
    ps_ref[0] = _stack_rows([jnp.where(re_lane, -p[1], p[1]) for p in pl_])


def _ssm_prep(rows, cols, scan, b_t, c_t, n_state, n_chan):
    ngt = b_t[0].shape[0]
    blk = SSM_BLOCK
    sl = scan[0].shape[2]
    spec3 = lambda a: pl.BlockSpec((1,) + a.shape[1:], lambda g: (g, 0, 0))
    ins = list(rows) + list(cols) + list(scan) + list(b_t) + list(c_t)
    w_s = jax.ShapeDtypeStruct((ngt, blk, LANES, sl), bf16)
    q_s = jax.ShapeDtypeStruct((ngt, sl, blk * LANES), bf16)
    k_s = jax.ShapeDtypeStruct((ngt, blk * LANES, blk * LANES), bf16)
    t_s = jax.ShapeDtypeStruct((ngt, SUBLANES, sl), f32)
    spec4 = lambda a: pl.BlockSpec((1,) + a.shape[1:], lambda g: (g, 0, 0, 0))
    return pl.pallas_call(
        functools.partial(_ssm_prep_kernel, n_state=n_state, n_chan=n_chan),
        grid=(ngt,),
        in_specs=[spec3(a) for a in ins],
        out_specs=[spec4(w_s), spec3(q_s), spec3(k_s), spec3(t_s), spec3(t_s)],
        out_shape=[w_s, q_s, k_s, t_s, t_s],
        compiler_params=_cparams(("parallel",)),
        name="ssm_prep",
    )(*ins)


def _ssm_kernel(u_ref, um_ref, w_ref, q_ref, k_ref, pr_ref, ps_ref, d_ref, y_ref,
                uf_scr, umf_scr, s_scr, sw_scr, xp_scr, yf_scr, *, rows_per_seq, meta_rows, n_state):
    blk = SSM_BLOCK
    r = u_ref.shape[0] // blk
    sl = s_scr.shape[1]
    tiles = sl // LANES
    pr, ps = pr_ref[0], ps_ref[0]

    def swap(x):
        return jnp.concatenate([pltpu.roll(x[:, c * LANES:(c + 1) * LANES], n_state, 1) for c in range(tiles)],
                               axis=1)

    uf_scr[...] = u_ref[...].astype(f32)
    umf_scr[...] = jnp.zeros_like(umf_scr)
    umf_scr[0:um_ref.shape[0], :] = um_ref[...].astype(f32)
    step_rows = lambda i: uf_scr[pl.ds(i, r, stride=blk), :]
    ucat = jnp.concatenate([step_rows(i).astype(bf16) for i in range(blk)], axis=1)
    umcat = jnp.concatenate([umf_scr[pl.ds(i, SUBLANES, stride=blk), :].astype(bf16) for i in range(blk)], axis=1)
    s = _dot(ucat, w_ref[0])
    sm = _dot(umcat, w_ref[0])
    rowmod = lax.broadcasted_iota(jnp.int32, s.shape, 0) % SUBLANES
    for k in (1, 2, 4):
        t = pltpu.roll(s, k, 0)
        t = t * pr[k - 1:k] + swap(t) * ps[k - 1:k]
        s = s + jnp.where(rowmod >= k, t, 0.0)
    s_scr[...] = s
    sw_scr[...] = swap(s)
    x0 = sm[0:1]
    for mrow in range(1, meta_rows):
        x0 = x0 * pr[0:1] + swap(x0) * ps[0:1] + sm[mrow:mrow + 1]
    x0w = swap(x0)
    row8 = lax.broadcasted_iota(jnp.int32, (SUBLANES, sl), 0)

    def body(b, carry):
        start = b % (rows_per_seq // SUBLANES) == 0
        rows = pl.ds(pl.multiple_of(b * SUBLANES, SUBLANES), SUBLANES)
        c = jnp.broadcast_to(jnp.where(start, x0, carry[0]), (SUBLANES, sl))
        cw = jnp.broadcast_to(jnp.where(start, x0w, carry[1]), (SUBLANES, sl))
        xs = s_scr[rows, :] + (c * pr + cw * ps)
        xw = sw_scr[rows, :] + (cw * pr - c * ps)
        xp_scr[rows, :] = jnp.where(row8 == 0, c, pltpu.roll(xs, 1, 0))
        return xs[SUBLANES - 1:SUBLANES], xw[SUBLANES - 1:SUBLANES]

    lax.fori_loop(0, r // SUBLANES, body, (x0, x0w))

    y_all = _dot(xp_scr[...].astype(bf16), q_ref[0]) + _dot(ucat, k_ref[0])
    for j in range(blk):
        y = y_all[:, j * LANES:(j + 1) * LANES] + d_ref[...] * step_rows(j)
        yf_scr[pl.ds(j, r, stride=blk), :] = _gelu(y)
    y_ref[...] = yf_scr[...].astype(y_ref.dtype)


def _ssm(u_arr, u_col0, um, prep, d, rows_per_seq, n_state):
    w, q, k, pr, ps = prep
    ngt, blk, _, sl = w.shape
    w = w.reshape(ngt, blk * LANES, sl)
    t = u_arr.shape[0]
    n_meta = um.shape[0]
    spec3 = lambda a: pl.BlockSpec((1,) + a.shape[1:], lambda g: (g, 0, 0))
    return pl.pallas_call(
        functools.partial(_ssm_kernel, rows_per_seq=rows_per_seq, meta_rows=n_meta // blk, n_state=n_state),
        grid=(ngt,),
        in_specs=[pl.BlockSpec((t, LANES), lambda g: (0, u_col0 + g)),
                  pl.BlockSpec((n_meta, LANES), lambda g: (0, g)),
                  spec3(w), spec3(q), spec3(k), spec3(pr), spec3(ps),
                  pl.BlockSpec((1, LANES), lambda g: (0, g))],
        out_specs=pl.BlockSpec((t, LANES), lambda g: (0, g)),
        out_shape=jax.ShapeDtypeStruct((t, ngt * LANES), bf16),
        scratch_shapes=[pltpu.VMEM((t, LANES), f32), pltpu.VMEM((SUBLANES * blk, LANES), f32),
                        pltpu.VMEM((t // blk, sl), f32), pltpu.VMEM((t // blk, sl), f32),
                        pltpu.VMEM((t // blk, sl), f32), pltpu.VMEM((t, LANES), f32)],
        compiler_params=_cparams(("parallel",)),
        name="ssm_blocked",
    )(u_arr, um, w, q, k, pr, ps, d)


def _mix_kernel(a_ref, wattn_ref, s_ref, wa_ref, wb_ref, ga_ref, gb_ref, o_ref):
    ya = _dot(a_ref[...], wattn_ref[...])
    s = s_ref[...]
    yb = _dot(s, wa_ref[...]) * _sigmoid(_dot(s, wb_ref[...]))
    o_ref[...] = (_sigmoid(ga_ref[...].astype(f32)) * ya
                  + _sigmoid(gb_ref[...].astype(f32)) * yb).astype(o_ref.dtype)


def _transpose_cast_kernel(x_ref, o_ref):
    o_ref[...] = x_ref[...].T.astype(o_ref.dtype)


def _transpose_cast(x, tr, dtype, name):
    rows, cols = x.shape
    return pl.pallas_call(
        _transpose_cast_kernel,
        grid=(rows // tr,),
        in_specs=[pl.BlockSpec((tr, cols), lambda i: (i, 0))],
        out_specs=pl.BlockSpec((cols, tr), lambda i: (0, i)),
        out_shape=jax.ShapeDtypeStruct((cols, rows), dtype),
        compiler_params=_cparams(("parallel",)),
        name=name,
    )(x)


def _out_proj_kernel(m_ref, w_ref, x_ref, o_ref):
    o_ref[...] = x_ref[...] + _dot(m_ref[...], w_ref[...])


def _peer_query_kernel(h_ref, g_ref, wq_ref, kk_ref, hn_ref, st_ref):
    x = h_ref[...]
    ms = jnp.mean(x * x, axis=-1, keepdims=True)
    hn_f = x * lax.rsqrt(ms + EPS) * g_ref[...]
    hn_ref[...] = hn_f.T.astype(bf16)
    q = _dot(hn_f.astype(bf16), wq_ref[...]).astype(bf16)
    n_heads = kk_ref.shape[0]
    half = kk_ref.shape[3]
    for h in range(n_heads):
        for side in range(2):
            c0 = (2 * h + side) * half
            st_ref[h, side] = _dot_nt(kk_ref[h, side], q[:, c0:c0 + half])


def _peer_query(h1, g, wq, kk, tm):
    t, d = h1.shape
    nh, _, nk, half = kk.shape
    return pl.pallas_call(
        _peer_query_kernel,
        grid=(t // tm,),
        in_specs=[pl.BlockSpec((tm, d), lambda i: (i, 0)),
                  pl.BlockSpec((1, d), lambda i: (0, 0)),
                  pl.BlockSpec(wq.shape, lambda i: (0, 0)),
                  pl.BlockSpec(kk.shape, lambda i: (0, 0, 0, 0))],
        out_specs=[pl.BlockSpec((d, tm), lambda i: (0, i)),
                   pl.BlockSpec((nh, 2, nk, tm), lambda i: (0, 0, 0, i))],
        out_shape=[jax.ShapeDtypeStruct((d, t), bf16), jax.ShapeDtypeStruct((nh, 2, nk, t), f32)],
        compiler_params=_cparams(("parallel",)),
        name="peer_query",
    )(h1, g, wq, kk)


def _sort_pairs(n):
    pairs = []
    p = 1
    while p < n:
        k = p
        while k >= 1:
            for j in range(k % p, n - k, 2 * k):
                for i in range(min(k, n - j - k)):
                    if (i + j) // (2 * p) == (i + j + k) // (2 * p):
                        pairs.append((i + j, i + j + k))
            k //= 2
        p *= 2
    return pairs


def _top_desc(tiles, n):
    m = len(tiles)
    vs = list(tiles)
    for i, j in _sort_pairs(1 << (m - 1).bit_length()):
        if j < m:
            vs[i], vs[j] = jnp.maximum(vs[i], vs[j]), jnp.minimum(vs[i], vs[j])
    neg = jnp.full(vs[0].shape, -jnp.inf, f32)
    vs = vs[:n] + [neg]
    vals = []
    for a in range(n):
        v = jnp.max(vs[0], axis=0, keepdims=True)
        vals.append(v)
        hit = vs[0] == v
        keep = min(len(vs) - 1, n - a - 1)
        vs = [jnp.where(hit, vs[k + 1], vs[k]) for k in range(keep)] + [neg]
    return _stack_rows(vals)


def _peer_topk_kernel(st_ref, r2_ref, e2_ref, n1_ref, e1_ref):
    chunk = LANES
    sub8 = lambda x: [x[k * SUBLANES:(k + 1) * SUBLANES] for k in range(x.shape[0] // SUBLANES)]
    for c in range(st_ref.shape[3] // chunk):
        cols = slice(c * chunk, (c + 1) * chunk)
        s1 = st_ref[0, 0, :, cols]
        s2 = st_ref[0, 1, :, cols]
        v1 = _top_desc(sub8(s1), TOPK)
        v2 = _top_desc(sub8(s2), TOPK)
        half = TOPK // 2
        cand = ([v1[0:1] + v2[:half], v1[0:1] + v2[half:]] + [v1[a:a + 1] + v2[:half] for a in range(1, half)]
                + [v1[half:] + v2[0:1]])
        best = _top_desc(cand, TOPK)
        tau = best[TOPK - 1:TOPK]
        z = jnp.sum(jnp.exp(best - best[0:1]), axis=0, keepdims=True)
        n1 = jnp.zeros(s1.shape, f32)
        rank2 = jnp.zeros(s2.shape, f32)
        for a in range(TOPK):
            cnt = jnp.sum(jnp.where((v1[a:a + 1] + v2) >= tau, 1.0, 0.0), axis=0, keepdims=True)
            n1 = jnp.where(s1 == v1[a:a + 1], cnt, n1)
            rank2 = jnp.where(s2 < v2[a:a + 1], a + 1.0, rank2)
        r2_ref[0, :, cols] = rank2.astype(bf16)
        e2_ref[0, :, cols] = jnp.exp(s2 - v2[0:1]).astype(bf16)
        n1_ref[0, :, cols] = n1
        e1_ref[0, :, cols] = jnp.exp(s1 - v1[0:1]) / z


def _peer_topk(st, tl):
    nh, _, nk, t = st.shape
    spec = lambda: pl.BlockSpec((1, nk, tl), lambda h, i: (h, 0, i))
    shp = lambda dt: jax.ShapeDtypeStruct((nh, nk, t), dt)
    return pl.pallas_call(
        _peer_topk_kernel,
        grid=(nh, t // tl),
        in_specs=[pl.BlockSpec((1, 2, nk, tl), lambda h, i: (h, 0, 0, i))],
        out_specs=[spec(), spec(), spec(), spec()],
        out_shape=[shp(bf16), shp(bf16), shp(f32), shp(f32)],
        compiler_params=_cparams(("parallel", "parallel")),
        name="peer_topk",
    )(st)


def _peer_dense_kernel(hn_ref, u_ref, vt_ref, r2_ref, e2_ref, n1_ref, e1_ref, h1_ref, o_ref,
                       acc_scr, a0_scr, a1_scr, *, nb, tn):
    e = pl.program_id(1)
    n_heads, nk, tm = r2_ref.shape
    eb = u_ref.shape[0]
    pk = 2 * SUBLANES

    @pl.when(e == 0)
    def _():
        acc_scr[...] = jnp.zeros_like(acc_scr)
        a0_scr[...] = _dot(u_ref[...], hn_ref[...])

    def step(a_cur, a_prev):
        if a_cur is not None:
            a_cur[...] = _dot(u_ref[...], hn_ref[...])
        blk = jnp.maximum(e - 1, 0)
        for c in range(tm // tn):
            cols = slice(c * tn, (c + 1) * tn)
            w_rows = []
            for ii in range(eb // nk):
                i1 = blk * (eb // nk) + ii
                gate = None
                for h in range(n_heads):
                    n1 = jnp.broadcast_to(n1_ref[h, pl.ds(i1, 1), cols], (pk, tn)).astype(bf16)
                    e1 = jnp.broadcast_to(e1_ref[h, pl.ds(i1, 1), cols], (pk, tn)).astype(bf16)
                    r2 = r2_ref[h, :, cols].reshape(nk // pk, pk, tn)
                    e2 = e2_ref[h, :, cols].reshape(nk // pk, pk, tn)
                    term = jnp.where(r2 < n1[None], e2, jnp.zeros_like(e2)) * e1[None]
                    gate = term if gate is None else gate + term
                g = _gelu(a_prev[ii * nk:(ii + 1) * nk, cols]).astype(bf16)
                w_rows.append(gate.reshape(nk, tn) * g)
            acc_scr[:, cols] += _dot(vt_ref[...], jnp.concatenate(w_rows, axis=0))

    middle = jnp.logical_and(e > 0, e < nb)

    @pl.when(jnp.logical_and(middle, e % 2 == 0))
    def _():
        step(a0_scr, a1_scr)

    @pl.when(jnp.logical_and(middle, e % 2 == 1))
    def _():
        step(a1_scr, a0_scr)

    @pl.when(e == nb)
    def _():
        step(None, a1_scr if (nb - 1) % 2 else a0_scr)
        o_ref[...] = h1_ref[...] + acc_scr[...].T


def _peer_dense(hn_t, u_b, vt_b, r2, e2, n1, e1, h1, tm, eb):
    d, t = hn_t.shape
    nb = u_b.shape[0] // eb
    nh, nk, _ = r2.shape
    tok = lambda: pl.BlockSpec((nh, nk, tm), lambda i, e: (0, 0, i), pipeline_mode=pl.Buffered(1))
    return pl.pallas_call(
        functools.partial(_peer_dense_kernel, nb=nb, tn=_tile(tm, 2 * LANES)),
        grid=(t // tm, nb + 1),
        in_specs=[pl.BlockSpec((d, tm), lambda i, e: (0, i)),
                  pl.BlockSpec((eb, d), lambda i, e: (jnp.minimum(e, nb - 1), 0)),
                  pl.BlockSpec((d, eb), lambda i, e: (0, jnp.maximum(e - 1, 0))),
                  tok(), tok(), tok(), tok(),
                  pl.BlockSpec((tm, d), lambda i, e: (i, 0), pipeline_mode=pl.Buffered(1))],
        out_specs=pl.BlockSpec((tm, d), lambda i, e: (i, 0)),
        out_shape=jax.ShapeDtypeStruct((t, d), f32),
        scratch_shapes=[pltpu.VMEM((d, tm), f32), pltpu.VMEM((eb, tm), f32), pltpu.VMEM((eb, tm), f32)],
        compiler_params=_cparams(("parallel", "arbitrary")),
        name="peer_dense",
    )(hn_t, u_b, vt_b, r2, e2, n1, e1, h1)


def _tile(n, pref):
    if n <= pref:
        return n
    t = pref // LANES * LANES
    while n % t:
        t -= LANES
    assert t > 0, (n, pref)
    return t


def kernel(x, meta_tokens, norm1_g, w_in, q_norm_g, k_norm_g, lambda_q1, lambda_k1, lambda_q2, lambda_k2, subln_g, w_attn_branch, ssm_a_re, ssm_a_im, ssm_log_dt, ssm_b_re, ssm_b_im, ssm_c_re, ssm_c_im, ssm_d, w_glu, w_out, norm2_g, peer_w_q, peer_k1, peer_k2, peer_u, peer_v):
    assert norm1_g.shape[0] == 1, "single-layer block only"
    batch, seq, d = x.shape
    n_meta = meta_tokens.shape[0]
    sub = q_norm_g.shape[-1]
    aw = w_attn_branch.shape[1]
    heads = aw // (2 * sub)
    n_groups, n_state, n_chan = ssm_b_re.shape[1:]
    sw = n_groups * n_chan
    assert 2 * sub == LANES and LANES % n_chan == 0 and sw % LANES == 0
    assert n_meta % SUBLANES == 0 and n_meta <= LANES
    t = batch * seq
    n_in = w_in.shape[2]
    assert n_in == 3 * aw + sw + 2 * d

    x2 = x.reshape(t, d)
    w_in_b = w_in[0].astype(bf16)

    proj = _rms_matmul(x2, norm1_g, w_in_b, _tile(t, 1024), _tile(n_in, 2048), bf16, "in_proj")
    proj_m = _rms_matmul(meta_tokens, norm1_g, w_in_b, n_meta, _tile(n_in, 1024), bf16, "in_proj_meta")

    half = sub // 2
    inv_freq = jnp.power(ROPE_THETA, -jnp.arange(half, dtype=f32) / half)
    ang = jnp.arange(n_meta + seq, dtype=f32)[:, None] * inv_freq[None, :]
    cos_t = jnp.tile(jnp.cos(ang), (1, LANES // half))
    sin_t = jnp.tile(jnp.concatenate([-jnp.sin(ang), jnp.sin(ang)], axis=1), (1, LANES // sub))
    lane = jnp.arange(LANES)
    jmat = (lane[:, None] // sub == lane[None, :] // sub).astype(bf16)
    gq = jnp.tile(q_norm_g, (1, aw // sub))
    gk = jnp.tile(k_norm_g, (1, aw // sub))
    tr = _tile(seq, 512)
    q_rot = _normrot(proj, 0, aw, gq, cos_t[n_meta:], sin_t[n_meta:], jmat, tr, sub ** -0.5 * math.log2(math.e), sub,
                      "q_normrot")
    k_rot = _normrot(proj, 1, aw, gk, cos_t[n_meta:], sin_t[n_meta:], jmat, tr, 1.0, sub, "k_normrot")
    km_rot = _normrot(proj_m, 1, aw, gk, cos_t[:n_meta], sin_t[:n_meta], jmat, n_meta, 1.0, sub, "k_normrot_meta")
    km = jnp.pad(km_rot, ((0, LANES - n_meta), (0, 0)))
    vm = jnp.pad(proj_m[:, 2 * aw:3 * aw], ((0, LANES - n_meta), (0, 0)))

    att = _attention(q_rot, k_rot, proj, 2 * aw // LANES, km, vm, lambda_q1, lambda_k1, lambda_q2, lambda_k2,
                     subln_g, batch, seq, heads, _tile(seq, 512), n_meta, sub)

    blk = SSM_BLOCK
    gpt = LANES // n_chan
    ngt = sw // LANES
    assert 2 * n_state == LANES and seq % (blk * SUBLANES) == 0 and n_meta % blk == 0
    grp = lambda a: a.reshape(ngt, gpt, n_state)
    row_l = lambda a: jnp.tile(jnp.repeat(grp(a), n_chan, axis=1), (1, 1, 2))
    col_l = lambda a: jnp.repeat(grp(a).transpose(0, 2, 1), n_chan, axis=2)
    scan_l = lambda a: jnp.tile(grp(a), (1, 1, 2)).reshape(ngt, 1, gpt * LANES)
    ldt2 = jnp.broadcast_to(ssm_log_dt[0][:, None], (n_groups, n_state))
    lays = lambda f: (f(ssm_a_re[0]), f(ssm_a_im[0]), f(ldt2))
    b_l = lambda b: jnp.tile(b[0].transpose(0, 2, 1).reshape(ngt, LANES, n_state), (1, 1, 2))
    c_l = lambda c: c[0].reshape(ngt, gpt, n_chan, n_state).transpose(0, 3, 1, 2).reshape(ngt, n_state, LANES)
    prep = _ssm_prep(lays(row_l), lays(col_l), lays(scan_l), (b_l(ssm_b_re), b_l(ssm_b_im)),
                     (c_l(ssm_c_re), c_l(ssm_c_im)), n_state, n_chan)
    u0 = 3 * aw
    ys = _ssm(proj, u0 // LANES, proj_m[:, u0:u0 + sw], prep, ssm_d, seq // blk, n_state)

    tm = _tile(t, 1024)
    tn = _tile(d, 1024)
    nj = d // tn
    ga_col0 = (3 * aw + sw) // tn
    gb_col0 = (3 * aw + sw + d) // tn
    grid = (t // tm, nj)
    sem = _cparams(("parallel", "arbitrary"))
    row_full = lambda k: pl.BlockSpec((tm, k), lambda i, j: (i, 0))
    w_col = lambda k, off=0: pl.BlockSpec((k, tn), lambda i, j: (0, j + off))
    out_tile = pl.BlockSpec((tm, tn), lambda i, j: (i, j))
    w_glu_b = w_glu[0].astype(bf16)
    gate = lambda col0: pl.BlockSpec((tm, tn), lambda i, j: (i, col0 + j))
    mix = pl.pallas_call(
        _mix_kernel, grid=grid,
        in_specs=[row_full(aw), w_col(aw), row_full(sw), w_col(sw), w_col(sw, nj), gate(ga_col0), gate(gb_col0)],
        out_specs=out_tile, out_shape=jax.ShapeDtypeStruct((t, d), bf16), compiler_params=sem, name="gated_mix",
    )(att, w_attn_branch[0].astype(bf16), ys, w_glu_b, w_glu_b, proj, proj)
    h1 = pl.pallas_call(
        _out_proj_kernel, grid=grid,
        in_specs=[row_full(d), w_col(d), out_tile],
        out_specs=out_tile, out_shape=jax.ShapeDtypeStruct((t, d), f32), compiler_params=sem, name="out_proj",
    )(mix, w_out[0].astype(bf16), x2)

    kk = jnp.stack([peer_k1[0], peer_k2[0]], axis=1).astype(bf16)
    assert kk.shape[2] == LANES and kk.shape[3] == LANES
    hn2, st = _peer_query(h1, norm2_g, peer_w_q[0].astype(bf16), kk, _tile(t, 512))
    r2, e2, n1, e1 = _peer_topk(st, _tile(t, 1024))
    u_b = peer_u[0].astype(bf16)
    vt_b = _transpose_cast(peer_v[0], _tile(peer_v.shape[1], 512), bf16, "peer_v_transpose")
    out = _peer_dense(hn2, u_b, vt_b, r2, e2, n1, e1, h1, _tile(t, 512), _tile(u_b.shape[0], 1024))
    return out.reshape(batch, seq, d)
```

```python
import functools
import math

import jax
import jax.numpy as jnp
from jax import lax
from jax.experimental import pallas as pl
from jax.experimental.pallas import tpu as pltpu

EPS = 1e-6
ROPE_THETA = 10000.0
LAM_INIT = 0.8 - 0.6 * math.exp(-0.3 * 0)
TOPK = 16
LANES = 128
SUBLANES = 8
NEG_BIG = -1e30
VMEM_LIMIT = 56 * 1024 * 1024

bf16 = jnp.bfloat16
f32 = jnp.float32


def _cparams(sem):
    return pltpu.CompilerParams(dimension_semantics=sem, vmem_limit_bytes=VMEM_LIMIT)


def _dot(a, b):
    return jnp.dot(a, b, preferred_element_type=f32)


def _dot_nt(a, b):
    return lax.dot_general(a, b, (((1,), (1,)), ((), ())), preferred_element_type=f32)


def _sigmoid(x):
    return 1.0 / (1.0 + jnp.exp(-x))


def _gelu(x):
    return 0.5 * x * (1.0 + lax.erf(x * (1.0 / math.sqrt(2.0))))


def _stack_rows(rows):
    k = len(rows)
    idx = lax.broadcasted_iota(jnp.int32, (k, rows[0].shape[1]), 0)
    out = jnp.broadcast_to(rows[0], idx.shape)
    for j in range(1, k):
        out = jnp.where(idx == j, rows[j], out)
    return out


def _rms_matmul_kernel(x_ref, g_ref, w_ref, o_ref, xn_ref):
    @pl.when(pl.program_id(1) == 0)
    def _():
        x = x_ref[...]
        ms = jnp.mean(x * x, axis=-1, keepdims=True)
        xn_ref[...] = (x * lax.rsqrt(ms + EPS) * g_ref[...]).astype(bf16)

    o_ref[...] = _dot(xn_ref[...], w_ref[...]).astype(o_ref.dtype)


def _rms_matmul(x, g, w, tm, tn, out_dtype, name):
    m, k = x.shape
    n = w.shape[1]
    return pl.pallas_call(
        _rms_matmul_kernel,
        grid=(m // tm, n // tn),
        in_specs=[pl.BlockSpec((tm, k), lambda i, j: (i, 0)),
                  pl.BlockSpec((1, k), lambda i, j: (0, 0)),
                  pl.BlockSpec((k, tn), lambda i, j: (0, j))],
        out_specs=pl.BlockSpec((tm, tn), lambda i, j: (i, j)),
        out_shape=jax.ShapeDtypeStruct((m, n), out_dtype),
        scratch_shapes=[pltpu.VMEM((tm, k), bf16)],
        compiler_params=_cparams(("parallel", "arbitrary")),
        name=name,
    )(x, g, w)


def _normrot_kernel(x_ref, g_ref, cos_ref, sin_ref, j_ref, o_ref, *, scale, sub):
    width = x_ref.shape[1]
    cos = cos_ref[...]
    sin = sin_ref[...]
    jmat = j_ref[...]
    lane = lax.broadcasted_iota(jnp.int32, cos.shape, 1)
    first_half = (lane % sub) < (sub // 2)
    for t in range(width // LANES):
        sl = slice(t * LANES, (t + 1) * LANES)
        x = x_ref[:, sl].astype(f32)
        x2 = x * x
        hi = x2.astype(bf16)
        lo = (x2 - hi.astype(f32)).astype(bf16)
        ssq = _dot(hi, jmat) + _dot(lo, jmat)
        xn = x * lax.rsqrt(ssq * (1.0 / sub) + EPS) * g_ref[:, sl]
        partner = jnp.where(first_half, pltpu.roll(xn, LANES - sub // 2, 1), pltpu.roll(xn, sub // 2, 1))
        o_ref[:, sl] = ((xn * cos + partner * sin) * scale).astype(o_ref.dtype)


def _normrot(x, col_block, width, g_t, cos, sin, jmat, tm, scale, sub, name):
    m = x.shape[0]
    nt = cos.shape[0] // tm
    return pl.pallas_call(
        functools.partial(_normrot_kernel, scale=scale, sub=sub),
        grid=(m // tm,),
        in_specs=[pl.BlockSpec((tm, width), lambda i: (i, col_block)),
                  pl.BlockSpec((1, width), lambda i: (0, 0)),
                  pl.BlockSpec((tm, LANES), lambda i: (i % nt, 0)),
                  pl.BlockSpec((tm, LANES), lambda i: (i % nt, 0)),
                  pl.BlockSpec((LANES, LANES), lambda i: (0, 0))],
        out_specs=pl.BlockSpec((tm, width), lambda i: (i, 0)),
        out_shape=jax.ShapeDtypeStruct((m, width), bf16),
        compiler_params=_cparams(("parallel",)),
        name=name,
    )(x, g_t, cos, sin, jmat)


def _attn_kernel(q_ref, k_ref, v_ref, km_ref, vm_ref, lq1_ref, lk1_ref, lq2_ref, lk2_ref, sg_ref,
                 o_ref, qq_scr, m_scr, acc_scr, s_scr, *, n_meta, sub, hpb):
    i = pl.program_id(2)
    tq = q_ref.shape[0]
    tk = tq
    hw = 2 * sub
    mp = km_ref.shape[0]
    lane = lax.broadcasted_iota(jnp.int32, (tq, hw), 1)
    head = lambda hh: slice(hh * hw, (hh + 1) * hw)

    def softmax_step(hh, s, v_aug, first):
        smax = jnp.max(s, axis=1, keepdims=True)
        if first:
            m_new = jnp.broadcast_to(smax, (2 * tq, hw))
        else:
            m_prev = m_scr[hh]
            m_new = jnp.maximum(m_prev, smax)
        p = jnp.concatenate([jnp.exp2(s[:, c * hw:(c + 1) * hw] - m_new) for c in range(s.shape[1] // hw)], axis=1)
        pv = _dot(p.astype(bf16), v_aug)
        if first:
            acc_scr[hh] = pv
        else:
            alpha = jnp.exp2(m_prev - m_new)
            acc_scr[hh] = jnp.concatenate([alpha, alpha], axis=1) * acc_scr[hh] + pv
        m_scr[hh] = m_new

    for hh in range(hpb):
        q = q_ref[:, head(hh)]
        zero = jnp.zeros_like(q)
        qq_scr[hh] = jnp.concatenate([jnp.where(lane < sub, q, zero), jnp.where(lane >= sub, q, zero)], axis=0)

    def put_scores(j, slot):
        start = pl.multiple_of(j * tk, tk)
        for hh in range(hpb):
            s_scr[slot, hh] = _dot_nt(qq_scr[hh], k_ref[pl.ds(start, tk), head(hh)])

    def consume(j, slot, masked):
        start = pl.multiple_of(j * tk, tk)
        for hh in range(hpb):
            s = s_scr[slot, hh]
            if masked:
                row = lax.broadcasted_iota(jnp.int32, s.shape, 0) % tq
                col = lax.broadcasted_iota(jnp.int32, s.shape, 1)
                s = jnp.where(col <= row, s, NEG_BIG)
            vb = v_ref[pl.ds(start, tk), head(hh)]
            softmax_step(hh, s, jnp.concatenate([vb, jnp.ones((tk, hw), bf16)], axis=1), False)

    def stage(j, slot):
        put_scores(j + 1, 1 - slot)
        consume(j, slot, False)

    put_scores(0, 0)

    for hh in range(hpb):
        s = _dot_nt(qq_scr[hh], km_ref[:, head(hh)])
        col = lax.broadcasted_iota(jnp.int32, s.shape, 1)
        s = jnp.where(col < n_meta, s, NEG_BIG)
        softmax_step(hh, s, jnp.concatenate([vm_ref[:, head(hh)], jnp.ones((mp, hw), bf16)], axis=1), True)

    def body(p, c):
        stage(2 * p, 0)
        stage(2 * p + 1, 1)
        return c

    lax.fori_loop(0, i // 2, body, 0)

    @pl.when(i % 2 == 0)
    def _():
        consume(i, 0, True)

    @pl.when(i % 2 == 1)
    def _():
        stage(i - 1, 0)
        consume(i, 1, True)

    lam = (jnp.exp(jnp.sum(lq1_ref[...] * lk1_ref[...], axis=1, keepdims=True))
           - jnp.exp(jnp.sum(lq2_ref[...] * lk2_ref[...], axis=1, keepdims=True)) + LAM_INIT)
    for hh in range(hpb):
        acc = acc_scr[hh]
        o = acc[:, :hw] / acc[:, hw:]
        att = o[:tq] - lam * o[tq:]
        ms = jnp.mean(att * att, axis=-1, keepdims=True)
        att = att * lax.rsqrt(ms + EPS) * sg_ref[...] * (1.0 - LAM_INIT)
        o_ref[:, head(hh)] = att.astype(o_ref.dtype)


def _attention(q_rot, k_rot, proj, v_col0, km, vm, lq1, lk1, lq2, lk2, sg, batch, seq, heads, tq, n_meta, sub):
    t, aw = q_rot.shape
    nq = seq // tq
    hw = 2 * sub
    hpb = 4 if heads % 4 == 0 else (2 if heads % 2 == 0 else 1)
    bw = hpb * hw
    mp = km.shape[0]
    vec = lambda: pl.BlockSpec((1, sub), lambda b, h, i: (0, 0))
    return pl.pallas_call(
        functools.partial(_attn_kernel, n_meta=n_meta, sub=sub, hpb=hpb),
        grid=(batch, heads // hpb, nq),
        in_specs=[pl.BlockSpec((tq, bw), lambda b, h, i: (b * nq + i, h)),
                  pl.BlockSpec((seq, bw), lambda b, h, i: (b, h)),
                  pl.BlockSpec((seq, bw), lambda b, h, i: (b, v_col0 // hpb + h)),
                  pl.BlockSpec((mp, bw), lambda b, h, i: (0, h)),
                  pl.BlockSpec((mp, bw), lambda b, h, i: (0, h)),
                  vec(), vec(), vec(), vec(),
                  pl.BlockSpec((1, hw), lambda b, h, i: (0, 0))],
        out_specs=pl.BlockSpec((tq, bw), lambda b, h, i: (b * nq + i, h)),
        out_shape=jax.ShapeDtypeStruct((t, aw), bf16),
        scratch_shapes=[pltpu.VMEM((hpb, 2 * tq, hw), bf16), pltpu.VMEM((hpb, 2 * tq, hw), f32),
                        pltpu.VMEM((hpb, 2 * tq, 2 * hw), f32), pltpu.VMEM((2, hpb, 2 * tq, tq), f32)],
        compiler_params=_cparams(("parallel", "parallel", "arbitrary")),
        name="diff_attention",
    )(q_rot, k_rot, proj, km, vm, lq1, lk1, lq2, lk2, sg)


SSM_BLOCK = 8


def _cmul(ar, ai, br, bi):
    return ar * br - ai * bi, ar * bi + ai * br


def _dot3(a, b):
    ah = a.astype(bf16)
    al = (a - ah.astype(f32)).astype(bf16)
    bh = b.astype(bf16)
    bl = (b - bh.astype(f32)).astype(bf16)
    return _dot(ah, bh) + (_dot(ah, bl) + _dot(al, bh))


def _abar(a_re, a_im, log_dt):
    dt = jnp.exp(log_dt)
    er = jnp.exp(a_re * dt)
    return er * jnp.cos(a_im * dt), er * jnp.sin(a_im * dt)


def _powers(ar, ai, n):
    out = [(jnp.ones_like(ar), jnp.zeros_like(ai)), (ar, ai)]
    for _ in range(2, n + 1):
        out.append(_cmul(*out[-1], ar, ai))
    return out


def _ssm_prep_kernel(ar_ref, ai_ref, ld_ref, ac_ref, aic_ref, ldc_ref, as_ref, ais_ref, lds_ref,
                     br_ref, bi_ref, cr_ref, ci_ref, w_ref, q_ref, k_ref, pr_ref, ps_ref, *, n_state, n_chan):
    blk = SSM_BLOCK
    gpt = LANES // n_chan
    a_re, a_im = ar_ref[0], ai_ref[0]
    abr, abi = _abar(a_re, a_im, ld_ref[0])
    den = a_re * a_re + a_im * a_im
    nr, ni = abr - 1.0, abi
    f_re = (nr * a_re + ni * a_im) / den
    f_im = (ni * a_re - nr * a_im) / den
    bbr, bbi = _cmul(f_re, f_im, br_ref[0], bi_ref[0])
    pw = _powers(abr, abi, blk - 1)
    is_re = lax.broadcasted_iota(jnp.int32, a_re.shape, 1) < n_state
    rgrp = lax.broadcasted_iota(jnp.int32, a_re.shape, 0) // n_chan
    wf = []
    for i in range(blk):
        xr, xi = _cmul(bbr, bbi, *pw[blk - 1 - i])
        tile = jnp.where(is_re, xr, xi)
        wf.append(jnp.concatenate([jnp.where(rgrp == g, tile, 0.0) for g in range(gpt)], axis=1))
        w_ref[0, i] = wf[i].astype(bf16)
    c_re, c_im = cr_ref[0], ci_ref[0]
    acr, aci = _abar(ac_ref[0], aic_ref[0], ldc_ref[0])
    pwc = _powers(acr, aci, blk)
    cgrp = lax.broadcasted_iota(jnp.int32, c_re.shape, 1) // n_chan

    def by_group(tr, ti):
        rows = []
        for g in range(gpt):
            rows += [jnp.where(cgrp == g, tr, 0.0), jnp.where(cgrp == g, ti, 0.0)]
        return jnp.concatenate(rows, axis=0)

    for j in range(blk):
        qr, qi = _cmul(c_re, c_im, *pwc[j + 1])
        q_ref[0, :, j * LANES:(j + 1) * LANES] = by_group(qr, -qi).astype(bf16)
    ccat = by_group(c_re, -c_im)
    kd = [_dot3(wf[blk - 1 - dd], ccat).astype(bf16) for dd in range(blk)]
    for i in range(blk):
        for j in range(blk):
            tile = kd[j - i] if j >= i else jnp.zeros((LANES, LANES), bf16)
            k_ref[0, i * LANES:(i + 1) * LANES, j * LANES:(j + 1) * LANES] = tile
    asr, asi = _abar(as_ref[0], ais_ref[0], lds_ref[0])
    a8 = _powers(asr, asi, blk)[blk]
    pl_ = _powers(a8[0], a8[1], SUBLANES)[1:]
    re_lane = lax.broadcasted_iota(jnp.int32, asr.shape, 1) % LANES < n_state
    pr_ref[0] = _stack_rows([p[0] for p in pl_])
    ps_ref[0] = _stack_rows([jnp.where(re_lane, -p[1], p[1]) for p in pl_])


def _ssm_prep(rows, cols, scan, b_t, c_t, n_state, n_chan):
    ngt = b_t[0].shape[0]
    blk = SSM_BLOCK
    sl = scan[0].shape[2]
    spec3 = lambda a: pl.BlockSpec((1,) + a.shape[1:], lambda g: (g, 0, 0))
    ins = list(rows) + list(cols) + list(scan) + list(b_t) + list(c_t)
    w_s = jax.ShapeDtypeStruct((ngt, blk, LANES, sl), bf16)
    q_s = jax.ShapeDtypeStruct((ngt, sl, blk * LANES), bf16)
    k_s = jax.ShapeDtypeStruct((ngt, blk * LANES, blk * LANES), bf16)
    t_s = jax.ShapeDtypeStruct((ngt, SUBLANES, sl), f32)
    spec4 = lambda a: pl.BlockSpec((1,) + a.shape[1:], lambda g: (g, 0, 0, 0))
    return pl.pallas_call(
        functools.partial(_ssm_prep_kernel, n_state=n_state, n_chan=n_chan),
        grid=(ngt,),
        in_specs=[spec3(a) for a in ins],
        out_specs=[spec4(w_s), spec3(q_s), spec3(k_s), spec3(t_s), spec3(t_s)],
        out_shape=[w_s, q_s, k_s, t_s, t_s],
        compiler_params=_cparams(("parallel",)),
        name="ssm_prep",
    )(*ins)


def _ssm_kernel(u_ref, um_ref, w_ref, q_ref, k_ref, pr_ref, ps_ref, d_ref, y_ref,
                uf_scr, umf_scr, s_scr, sw_scr, xp_scr, yf_scr, *, rows_per_seq, meta_rows, n_state):
    blk = SSM_BLOCK
    r = u_ref.shape[0] // blk
    sl = s_scr.shape[1]
    tiles = sl // LANES
    pr, ps = pr_ref[0], ps_ref[0]

    def swap(x):
        return jnp.concatenate([pltpu.roll(x[:, c * LANES:(c + 1) * LANES], n_state, 1) for c in range(tiles)],
                               axis=1)

    uf_scr[...] = u_ref[...].astype(f32)
    umf_scr[...] = jnp.zeros_like(umf_scr)
    umf_scr[0:um_ref.shape[0], :] = um_ref[...].astype(f32)
    step_rows = lambda i: uf_scr[pl.ds(i, r, stride=blk), :]
    ucat = jnp.concatenate([step_rows(i).astype(bf16) for i in range(blk)], axis=1)
    umcat = jnp.concatenate([umf_scr[pl.ds(i, SUBLANES, stride=blk), :].astype(bf16) for i in range(blk)], axis=1)
    s = _dot(ucat, w_ref[0])
    sm = _dot(umcat, w_ref[0])
    rowmod = lax.broadcasted_iota(jnp.int32, s.shape, 0) % SUBLANES
    for k in (1, 2, 4):
        t = pltpu.roll(s, k, 0)
        t = t * pr[k - 1:k] + swap(t) * ps[k - 1:k]
        s = s + jnp.where(rowmod >= k, t, 0.0)
    s_scr[...] = s
    sw_scr[...] = swap(s)
    x0 = sm[0:1]
    for mrow in range(1, meta_rows):
        x0 = x0 * pr[0:1] + swap(x0) * ps[0:1] + sm[mrow:mrow + 1]
    x0w = swap(x0)
    row8 = lax.broadcasted_iota(jnp.int32, (SUBLANES, sl), 0)

    def body(b, carry):
        start = b % (rows_per_seq // SUBLANES) == 0
        rows = pl.ds(pl.multiple_of(b * SUBLANES, SUBLANES), SUBLANES)
        c = jnp.broadcast_to(jnp.where(start, x0, carry[0]), (SUBLANES, sl))
        cw = jnp.broadcast_to(jnp.where(start, x0w, carry[1]), (SUBLANES, sl))
        xs = s_scr[rows, :] + (c * pr + cw * ps)
        xw = sw_scr[rows, :] + (cw * pr - c * ps)
        xp_scr[rows, :] = jnp.where(row8 == 0, c, pltpu.roll(xs, 1, 0))
        return xs[SUBLANES - 1:SUBLANES], xw[SUBLANES - 1:SUBLANES]

    lax.fori_loop(0, r // SUBLANES, body, (x0, x0w))

    y_all = _dot(xp_scr[...].astype(bf16), q_ref[0]) + _dot(ucat, k_ref[0])
    for j in range(blk):
        y = y_all[:, j * LANES:(j + 1) * LANES] + d_ref[...] * step_rows(j)
        yf_scr[pl.ds(j, r, stride=blk), :] = _gelu(y)
    y_ref[...] = yf_scr[...].astype(y_ref.dtype)


def _ssm(u_arr, u_col0, um, prep, d, rows_per_seq, n_state):
    w, q, k, pr, ps = prep
    ngt, blk, _, sl = w.shape
    w = w.reshape(ngt, blk * LANES, sl)
    t = u_arr.shape[0]
    n_meta = um.shape[0]
    spec3 = lambda a: pl.BlockSpec((1,) + a.shape[1:], lambda g: (g, 0, 0))
    return pl.pallas_call(
        functools.partial(_ssm_kernel, rows_per_seq=rows_per_seq, meta_rows=n_meta // blk, n_state=n_state),
        grid=(ngt,),
        in_specs=[pl.BlockSpec((t, LANES), lambda g: (0, u_col0 + g)),
                  pl.BlockSpec((n_meta, LANES), lambda g: (0, g)),
                  spec3(w), spec3(q), spec3(k), spec3(pr), spec3(ps),
                  pl.BlockSpec((1, LANES), lambda g: (0, g))],
        out_specs=pl.BlockSpec((t, LANES), lambda g: (0, g)),
        out_shape=jax.ShapeDtypeStruct((t, ngt * LANES), bf16),
        scratch_shapes=[pltpu.VMEM((t, LANES), f32), pltpu.VMEM((SUBLANES * blk, LANES), f32),
                        pltpu.VMEM((t // blk, sl), f32), pltpu.VMEM((t // blk, sl), f32),
                        pltpu.VMEM((t // blk, sl), f32), pltpu.VMEM((t, LANES), f32)],
        compiler_params=_cparams(("parallel",)),
        name="ssm_blocked",
    )(u_arr, um, w, q, k, pr, ps, d)


def _mix_kernel(a_ref, wattn_ref, s_ref, wa_ref, wb_ref, ga_ref, gb_ref, o_ref):
    ya = _dot(a_ref[...], wattn_ref[...])
    s = s_ref[...]
    yb = _dot(s, wa_ref[...]) * _sigmoid(_dot(s, wb_ref[...]))
    o_ref[...] = (_sigmoid(ga_ref[...].astype(f32)) * ya
                  + _sigmoid(gb_ref[...].astype(f32)) * yb).astype(o_ref.dtype)


def _transpose_cast_kernel(x_ref, o_ref):
    o_ref[...] = x_ref[...].T.astype(o_ref.dtype)


def _transpose_cast(x, tr, dtype, name):
    rows, cols = x.shape
    return pl.pallas_call(
        _transpose_cast_kernel,
        grid=(rows // tr,),
        in_specs=[pl.BlockSpec((tr, cols), lambda i: (i, 0))],
        out_specs=pl.BlockSpec((cols, tr), lambda i: (0, i)),
        out_shape=jax.ShapeDtypeStruct((cols, rows), dtype),
        compiler_params=_cparams(("parallel",)),
        name=name,
    )(x)


def _out_proj_kernel(m_ref, w_ref, x_ref, o_ref):
    o_ref[...] = x_ref[...] + _dot(m_ref[...], w_ref[...])


def _peer_query_kernel(h_ref, g_ref, wq_ref, kk_ref, hn_ref, st_ref):
    x = h_ref[...]
    ms = jnp.mean(x * x, axis=-1, keepdims=True)
    hn_f = x * lax.rsqrt(ms + EPS) * g_ref[...]
    hn_ref[...] = hn_f.T.astype(bf16)
    q = _dot(hn_f.astype(bf16), wq_ref[...]).astype(bf16)
    n_heads = kk_ref.shape[0]
    half = kk_ref.shape[3]
    for h in range(n_heads):
        for side in range(2):
            c0 = (2 * h + side) * half
            st_ref[h, side] = _dot_nt(kk_ref[h, side], q[:, c0:c0 + half])


def _peer_query(h1, g, wq, kk, tm):
    t, d = h1.shape
    nh, _, nk, half = kk.shape
    return pl.pallas_call(
        _peer_query_kernel,
        grid=(t // tm,),
        in_specs=[pl.BlockSpec((tm, d), lambda i: (i, 0)),
                  pl.BlockSpec((1, d), lambda i: (0, 0)),
                  pl.BlockSpec(wq.shape, lambda i: (0, 0)),
                  pl.BlockSpec(kk.shape, lambda i: (0, 0, 0, 0))],
        out_specs=[pl.BlockSpec((d, tm), lambda i: (0, i)),
                   pl.BlockSpec((nh, 2, nk, tm), lambda i: (0, 0, 0, i))],
        out_shape=[jax.ShapeDtypeStruct((d, t), bf16), jax.ShapeDtypeStruct((nh, 2, nk, t), f32)],
        compiler_params=_cparams(("parallel",)),
        name="peer_query",
    )(h1, g, wq, kk)


def _sort_pairs(n):
    pairs = []
    p = 1
    while p < n:
        k = p
        while k >= 1:
            for j in range(k % p, n - k, 2 * k):
                for i in range(min(k, n - j - k)):
                    if (i + j) // (2 * p) == (i + j + k) // (2 * p):
                        pairs.append((i + j, i + j + k))
            k //= 2
        p *= 2
    return pairs


def _top_desc(tiles, n):
    m = len(tiles)
    vs = list(tiles)
    for i, j in _sort_pairs(1 << (m - 1).bit_length()):
        if j < m:
            vs[i], vs[j] = jnp.maximum(vs[i], vs[j]), jnp.minimum(vs[i], vs[j])
    neg = jnp.full(vs[0].shape, -jnp.inf, f32)
    vs = vs[:n] + [neg]
    vals = []
    for a in range(n):
        v = jnp.max(vs[0], axis=0, keepdims=True)
        vals.append(v)
        hit = vs[0] == v
        keep = min(len(vs) - 1, n - a - 1)
        vs = [jnp.where(hit, vs[k + 1], vs[k]) for k in range(keep)] + [neg]
    return _stack_rows(vals)


def _peer_topk_kernel(st_ref, r2_ref, e2_ref, n1_ref, e1_ref):
    chunk = LANES
    sub8 = lambda x: [x[k * SUBLANES:(k + 1) * SUBLANES] for k in range(x.shape[0] // SUBLANES)]
    for c in range(st_ref.shape[3] // chunk):
        cols = slice(c * chunk, (c + 1) * chunk)
        s1 = st_ref[0, 0, :, cols]
        s2 = st_ref[0, 1, :, cols]
        v1 = _top_desc(sub8(s1), TOPK)
        v2 = _top_desc(sub8(s2), TOPK)
        half = TOPK // 2
        cand = ([v1[0:1] + v2[:half], v1[0:1] + v2[half:]] + [v1[a:a + 1] + v2[:half] for a in range(1, half)]
                + [v1[half:] + v2[0:1]])
        best = _top_desc(cand, TOPK)
        tau = best[TOPK - 1:TOPK]
        z = jnp.sum(jnp.exp(best - best[0:1]), axis=0, keepdims=True)
        n1 = jnp.zeros(s1.shape, f32)
        rank2 = jnp.zeros(s2.shape, f32)
        for a in range(TOPK):
            cnt = jnp.sum(jnp.where((v1[a:a + 1] + v2) >= tau, 1.0, 0.0), axis=0, keepdims=True)
            n1 = jnp.where(s1 == v1[a:a + 1], cnt, n1)
            rank2 = jnp.where(s2 < v2[a:a + 1], a + 1.0, rank2)
        r2_ref[0, :, cols] = rank2.astype(bf16)
        e2_ref[0, :, cols] = jnp.exp(s2 - v2[0:1]).astype(bf16)
        n1_ref[0, :, cols] = n1
        e1_ref[0, :, cols] = jnp.exp(s1 - v1[0:1]) / z


def _peer_topk(st, tl):
    nh, _, nk, t = st.shape
    spec = lambda: pl.BlockSpec((1, nk, tl), lambda h, i: (h, 0, i))
    shp = lambda dt: jax.ShapeDtypeStruct((nh, nk, t), dt)
    return pl.pallas_call(
        _peer_topk_kernel,
        grid=(nh, t // tl),
        in_specs=[pl.BlockSpec((1, 2, nk, tl), lambda h, i: (h, 0, 0, i))],
        out_specs=[spec(), spec(), spec(), spec()],
        out_shape=[shp(bf16), shp(bf16), shp(f32), shp(f32)],
        compiler_params=_cparams(("parallel", "parallel")),
        name="peer_topk",
    )(st)


def _peer_dense_kernel(hn_ref, u_ref, vt_ref, r2_ref, e2_ref, n1_ref, e1_ref, h1_ref, o_ref,
                       acc_scr, a0_scr, a1_scr, *, nb, tn):
    e = pl.program_id(1)
    n_heads, nk, tm = r2_ref.shape
    eb = u_ref.shape[0]
    pk = 2 * SUBLANES

    @pl.when(e == 0)
    def _():
        acc_scr[...] = jnp.zeros_like(acc_scr)
        a0_scr[...] = _dot(u_ref[...], hn_ref[...])

    def step(a_cur, a_prev):
        if a_cur is not None:
            a_cur[...] = _dot(u_ref[...], hn_ref[...])
        blk = jnp.maximum(e - 1, 0)
        for c in range(tm // tn):
            cols = slice(c * tn, (c + 1) * tn)
            w_rows = []
            for ii in range(eb // nk):
                i1 = blk * (eb // nk) + ii
                gate = None
                for h in range(n_heads):
                    n1 = jnp.broadcast_to(n1_ref[h, pl.ds(i1, 1), cols], (pk, tn)).astype(bf16)
                    e1 = jnp.broadcast_to(e1_ref[h, pl.ds(i1, 1), cols], (pk, tn)).astype(bf16)
                    r2 = r2_ref[h, :, cols].reshape(nk // pk, pk, tn)
                    e2 = e2_ref[h, :, cols].reshape(nk // pk, pk, tn)
                    term = jnp.where(r2 < n1[None], e2, jnp.zeros_like(e2)) * e1[None]
                    gate = term if gate is None else gate + term
                g = _gelu(a_prev[ii * nk:(ii + 1) * nk, cols]).astype(bf16)
                w_rows.append(gate.reshape(nk, tn) * g)
            acc_scr[:, cols] += _dot(vt_ref[...], jnp.concatenate(w_rows, axis=0))

    middle = jnp.logical_and(e > 0, e < nb)

    @pl.when(jnp.logical_and(middle, e % 2 == 0))
    def _():
        step(a0_scr, a1_scr)

    @pl.when(jnp.logical_and(middle, e % 2 == 1))
    def _():
        step(a1_scr, a0_scr)

    @pl.when(e == nb)
    def _():
        step(None, a1_scr if (nb - 1) % 2 else a0_scr)
        o_ref[...] = h1_ref[...] + acc_scr[...].T


def _peer_dense(hn_t, u_b, vt_b, r2, e2, n1, e1, h1, tm, eb):
    d, t = hn_t.shape
    nb = u_b.shape[0] // eb
    nh, nk, _ = r2.shape
    tok = lambda: pl.BlockSpec((nh, nk, tm), lambda i, e: (0, 0, i), pipeline_mode=pl.Buffered(1))
    return pl.pallas_call(
        functools.partial(_peer_dense_kernel, nb=nb, tn=_tile(tm, 2 * LANES)),
        grid=(t // tm, nb + 1),
        in_specs=[pl.BlockSpec((d, tm), lambda i, e: (0, i)),
                  pl.BlockSpec((eb, d), lambda i, e: (jnp.minimum(e, nb - 1), 0)),
                  pl.BlockSpec((d, eb), lambda i, e: (0, jnp.maximum(e - 1, 0))),
                  tok(), tok(), tok(), tok(),
                  pl.BlockSpec((tm, d), lambda i, e: (i, 0), pipeline_mode=pl.Buffered(1))],
        out_specs=pl.BlockSpec((tm, d), lambda i, e: (i, 0)),
        out_shape=jax.ShapeDtypeStruct((t, d), f32),
        scratch_shapes=[pltpu.VMEM((d, tm), f32), pltpu.VMEM((eb, tm), f32), pltpu.VMEM((eb, tm), f32)],
        compiler_params=_cparams(("parallel", "arbitrary")),
        name="peer_dense",
    )(hn_t, u_b, vt_b, r2, e2, n1, e1, h1)


def _tile(n, pref):
    if n <= pref:
        return n
    t = pref // LANES * LANES
    while n % t:
        t -= LANES
    assert t > 0, (n, pref)
    return t


def kernel(x, meta_tokens, norm1_g, w_in, q_norm_g, k_norm_g, lambda_q1, lambda_k1, lambda_q2, lambda_k2, subln_g, w_attn_branch, ssm_a_re, ssm_a_im, ssm_log_dt, ssm_b_re, ssm_b_im, ssm_c_re, ssm_c_im, ssm_d, w_glu, w_out, norm2_g, peer_w_q, peer_k1, peer_k2, peer_u, peer_v):
    assert norm1_g.shape[0] == 1, "single-layer block only"
    batch, seq, d = x.shape
    n_meta = meta_tokens.shape[0]
    sub = q_norm_g.shape[-1]
    aw = w_attn_branch.shape[1]
    heads = aw // (2 * sub)
    n_groups, n_state, n_chan = ssm_b_re.shape[1:]
    sw = n_groups * n_chan
    assert 2 * sub == LANES and LANES % n_chan == 0 and sw % LANES == 0
    assert n_meta % SUBLANES == 0 and n_meta <= LANES
    t = batch * seq
    n_in = w_in.shape[2]
    assert n_in == 3 * aw + sw + 2 * d

    x2 = x.reshape(t, d)
    w_in_b = w_in[0].astype(bf16)

    proj = _rms_matmul(x2, norm1_g, w_in_b, _tile(t, 1024), _tile(n_in, 2048), bf16, "in_proj")
    proj_m = _rms_matmul(meta_tokens, norm1_g, w_in_b, n_meta, _tile(n_in, 1024), bf16, "in_proj_meta")

    half = sub // 2
    inv_freq = jnp.power(ROPE_THETA, -jnp.arange(half, dtype=f32) / half)
    ang = jnp.arange(n_meta + seq, dtype=f32)[:, None] * inv_freq[None, :]
    cos_t = jnp.tile(jnp.cos(ang), (1, LANES // half))
    sin_t = jnp.tile(jnp.concatenate([-jnp.sin(ang), jnp.sin(ang)], axis=1), (1, LANES // sub))
    lane = jnp.arange(LANES)
    jmat = (lane[:, None] // sub == lane[None, :] // sub).astype(bf16)
    gq = jnp.tile(q_norm_g, (1, aw // sub))
    gk = jnp.tile(k_norm_g, (1, aw // sub))
    tr = _tile(seq, 512)
    q_rot = _normrot(proj, 0, aw, gq, cos_t[n_meta:], sin_t[n_meta:], jmat, tr, sub ** -0.5 * math.log2(math.e), sub,
                      "q_normrot")
    k_rot = _normrot(proj, 1, aw, gk, cos_t[n_meta:], sin_t[n_meta:], jmat, tr, 1.0, sub, "k_normrot")
    km_rot = _normrot(proj_m, 1, aw, gk, cos_t[:n_meta], sin_t[:n_meta], jmat, n_meta, 1.0, sub, "k_normrot_meta")
    km = jnp.pad(km_rot, ((0, LANES - n_meta), (0, 0)))
    vm = jnp.pad(proj_m[:, 2 * aw:3 * aw], ((0, LANES - n_meta), (0, 0)))

    att = _attention(q_rot, k_rot, proj, 2 * aw // LANES, km, vm, lambda_q1, lambda_k1, lambda_q2, lambda_k2,
                     subln_g, batch, seq, heads, _tile(seq, 512), n_meta, sub)

    blk = SSM_BLOCK
    gpt = LANES // n_chan
    ngt = sw // LANES
    assert 2 * n_state == LANES and seq % (blk * SUBLANES) == 0 and n_meta % blk == 0
    grp = lambda a: a.reshape(ngt, gpt, n_state)
    row_l = lambda a: jnp.tile(jnp.repeat(grp(a), n_chan, axis=1), (1, 1, 2))
    col_l = lambda a: jnp.repeat(grp(a).transpose(0, 2, 1), n_chan, axis=2)
    scan_l = lambda a: jnp.tile(grp(a), (1, 1, 2)).reshape(ngt, 1, gpt * LANES)
    ldt2 = jnp.broadcast_to(ssm_log_dt[0][:, None], (n_groups, n_state))
    lays = lambda f: (f(ssm_a_re[0]), f(ssm_a_im[0]), f(ldt2))
    b_l = lambda b: jnp.tile(b[0].transpose(0, 2, 1).reshape(ngt, LANES, n_state), (1, 1, 2))
    c_l = lambda c: c[0].reshape(ngt, gpt, n_chan, n_state).transpose(0, 3, 1, 2).reshape(ngt, n_state, LANES)
    prep = _ssm_prep(lays(row_l), lays(col_l), lays(scan_l), (b_l(ssm_b_re), b_l(ssm_b_im)),
                     (c_l(ssm_c_re), c_l(ssm_c_im)), n_state, n_chan)
    u0 = 3 * aw
    ys = _ssm(proj, u0 // LANES, proj_m[:, u0:u0 + sw], prep, ssm_d, seq // blk, n_state)

    tm = _tile(t, 1024)
    tn = _tile(d, 1024)
    nj = d // tn
    ga_col0 = (3 * aw + sw) // tn
    gb_col0 = (3 * aw + sw + d) // tn
    grid = (t // tm, nj)
    sem = _cparams(("parallel", "arbitrary"))
    row_full = lambda k: pl.BlockSpec((tm, k), lambda i, j: (i, 0))
    w_col = lambda k, off=0: pl.BlockSpec((k, tn), lambda i, j: (0, j + off))
    out_tile = pl.BlockSpec((tm, tn), lambda i, j: (i, j))
    w_glu_b = w_glu[0].astype(bf16)
    gate = lambda col0: pl.BlockSpec((tm, tn), lambda i, j: (i, col0 + j))
    mix = pl.pallas_call(
        _mix_kernel, grid=grid,
        in_specs=[row_full(aw), w_col(aw), row_full(sw), w_col(sw), w_col(sw, nj), gate(ga_col0), gate(gb_col0)],
        out_specs=out_tile, out_shape=jax.ShapeDtypeStruct((t, d), bf16), compiler_params=sem, name="gated_mix",
    )(att, w_attn_branch[0].astype(bf16), ys, w_glu_b, w_glu_b, proj, proj)
    h1 = pl.pallas_call(
        _out_proj_kernel, grid=grid,
        in_specs=[row_full(d), w_col(d), out_tile],
        out_specs=out_tile, out_shape=jax.ShapeDtypeStruct((t, d), f32), compiler_params=sem, name="out_proj",
    )(mix, w_out[0].astype(bf16), x2)

    kk = jnp.stack([peer_k1[0], peer_k2[0]], axis=1).astype(bf16)
    assert kk.shape[2] == LANES and kk.shape[3] == LANES
    hn2, st = _peer_query(h1, norm2_g, peer_w_q[0].astype(bf16), kk, _tile(t, 512))
    r2, e2, n1, e1 = _peer_topk(st, _tile(t, 1024))
    u_b = peer_u[0].astype(bf16)
    vt_b = _transpose_cast(peer_v[0], _tile(peer_v.shape[1], 512), bf16, "peer_v_transpose")
    out = _peer_dense(hn2, u_b, vt_b, r2, e2, n1, e1, h1, _tile(t, 512), _tile(u_b.shape[0], 1024))
    return out.reshape(batch, seq, d)
```

```python
import functools
import math

import jax
import jax.numpy as jnp
from jax import lax
from jax.experimental import pallas as pl
from jax.experimental.pallas import tpu as pltpu

EPS = 1e-6
ROPE_THETA = 10000.0
LAM_INIT = 0.8 - 0.6 * math.exp(-0.3 * 0)
TOPK = 16
LANES = 128
SUBLANES = 8
NEG_BIG = -1e30
VMEM_LIMIT = 56 * 1024 * 1024

bf16 = jnp.bfloat16
f32 = jnp.float32


def _cparams(sem):
    return pltpu.CompilerParams(dimension_semantics=sem, vmem_limit_bytes=VMEM_LIMIT)


def _dot(a, b):
    return jnp.dot(a, b, preferred_element_type=f32)


def _dot_nt(a, b):
    return lax.dot_general(a, b, (((1,), (1,)), ((), ())), preferred_element_type=f32)


def _sigmoid(x):
    return 1.0 / (1.0 + jnp.exp(-x))


def _gelu(x):
    return 0.5 * x * (1.0 + lax.erf(x * (1.0 / math.sqrt(2.0))))


def _stack_rows(rows):
    k = len(rows)
    idx = lax.broadcasted_iota(jnp.int32, (k, rows[0].shape[1]), 0)
    out = jnp.broadcast_to(rows[0], idx.shape)
    for j in range(1, k):
        out = jnp.where(idx == j, rows[j], out)
    return out


def _rms_matmul_kernel(x_ref, g_ref, w_ref, o_ref, xn_ref):
    @pl.when(pl.program_id(1) == 0)
    def _():
        x = x_ref[...]
        ms = jnp.mean(x * x, axis=-1, keepdims=True)
        xn_ref[...] = (x * lax.rsqrt(ms + EPS) * g_ref[...]).astype(bf16)

    o_ref[...] = _dot(xn_ref[...], w_ref[...]).astype(o_ref.dtype)


def _rms_matmul(x, g, w, tm, tn, out_dtype, name):
    m, k = x.shape
    n = w.shape[1]
    return pl.pallas_call(
        _rms_matmul_kernel,
        grid=(m // tm, n // tn),
        in_specs=[pl.BlockSpec((tm, k), lambda i, j: (i, 0)),
                  pl.BlockSpec((1, k), lambda i, j: (0, 0)),
                  pl.BlockSpec((k, tn), lambda i, j: (0, j))],
        out_specs=pl.BlockSpec((tm, tn), lambda i, j: (i, j)),
        out_shape=jax.ShapeDtypeStruct((m, n), out_dtype),
        scratch_shapes=[pltpu.VMEM((tm, k), bf16)],
        compiler_params=_cparams(("parallel", "arbitrary")),
        name=name,
    )(x, g, w)


def _normrot_kernel(x_ref, g_ref, cos_ref, sin_ref, j_ref, o_ref, *, scale, sub):
    width = x_ref.shape[1]
    cos = cos_ref[...]
    sin = sin_ref[...]
    jmat = j_ref[...]
    lane = lax.broadcasted_iota(jnp.int32, cos.shape, 1)
    first_half = (lane % sub) < (sub // 2)
    for t in range(width // LANES):
        sl = slice(t * LANES, (t + 1) * LANES)
        x = x_ref[:, sl].astype(f32)
        x2 = x * x
        hi = x2.astype(bf16)
        lo = (x2 - hi.astype(f32)).astype(bf16)
        ssq = _dot(hi, jmat) + _dot(lo, jmat)
        xn = x * lax.rsqrt(ssq * (1.0 / sub) + EPS) * g_ref[:, sl]
        partner = jnp.where(first_half, pltpu.roll(xn, LANES - sub // 2, 1), pltpu.roll(xn, sub // 2, 1))
        o_ref[:, sl] = ((xn * cos + partner * sin) * scale).astype(o_ref.dtype)


def _normrot(x, col_block, width, g_t, cos, sin, jmat, tm, scale, sub, name):
    m = x.shape[0]
    nt = cos.shape[0] // tm
    return pl.pallas_call(
        functools.partial(_normrot_kernel, scale=scale, sub=sub),
        grid=(m // tm,),
        in_specs=[pl.BlockSpec((tm, width), lambda i: (i, col_block)),
                  pl.BlockSpec((1, width), lambda i: (0, 0)),
                  pl.BlockSpec((tm, LANES), lambda i: (i % nt, 0)),
                  pl.BlockSpec((tm, LANES), lambda i: (i % nt, 0)),
                  pl.BlockSpec((LANES, LANES), lambda i: (0, 0))],
        out_specs=pl.BlockSpec((tm, width), lambda i: (i, 0)),
        out_shape=jax.ShapeDtypeStruct((m, width), bf16),
        compiler_params=_cparams(("parallel",)),
        name=name,
    )(x, g_t, cos, sin, jmat)


def _attn_kernel(q_ref, k_ref, v_ref, km_ref, vm_ref, lq1_ref, lk1_ref, lq2_ref, lk2_ref, sg_ref,
                 o_ref, qq_scr, m_scr, acc_scr, s_scr, *, n_meta, sub, hpb):
    i = pl.program_id(2)
    tq = q_ref.shape[0]
    tk = tq
    hw = 2 * sub
    mp = km_ref.shape[0]
    lane = lax.broadcasted_iota(jnp.int32, (tq, hw), 1)
    head = lambda hh: slice(hh * hw, (hh + 1) * hw)

    def softmax_step(hh, s, v_aug, first):
        smax = jnp.max(s, axis=1, keepdims=True)
        if first:
            m_new = jnp.broadcast_to(smax, (2 * tq, hw))
        else:
            m_prev = m_scr[hh]
            m_new = jnp.maximum(m_prev, smax)
        p = jnp.concatenate([jnp.exp2(s[:, c * hw:(c + 1) * hw] - m_new) for c in range(s.shape[1] // hw)], axis=1)
        pv = _dot(p.astype(bf16), v_aug)
        if first:
            acc_scr[hh] = pv
        else:
            alpha = jnp.exp2(m_prev - m_new)
            acc_scr[hh] = jnp.concatenate([alpha, alpha], axis=1) * acc_scr[hh] + pv
        m_scr[hh] = m_new

    for hh in range(hpb):
        q = q_ref[:, head(hh)]
        zero = jnp.zeros_like(q)
        qq_scr[hh] = jnp.concatenate([jnp.where(lane < sub, q, zero), jnp.where(lane >= sub, q, zero)], axis=0)

    def put_scores(j, slot):
        start = pl.multiple_of(j * tk, tk)
        for hh in range(hpb):
            s_scr[slot, hh] = _dot_nt(qq_scr[hh], k_ref[pl.ds(start, tk), head(hh)])

    def consume(j, slot, masked):
        start = pl.multiple_of(j * tk, tk)
        for hh in range(hpb):
            s = s_scr[slot, hh]
            if masked:
                row = lax.broadcasted_iota(jnp.int32, s.shape, 0) % tq
                col = lax.broadcasted_iota(jnp.int32, s.shape, 1)
                s = jnp.where(col <= row, s, NEG_BIG)
            vb = v_ref[pl.ds(start, tk), head(hh)]
            softmax_step(hh, s, jnp.concatenate([vb, jnp.ones((tk, hw), bf16)], axis=1), False)

    def stage(j, slot):
        put_scores(j + 1, 1 - slot)
        consume(j, slot, False)

    put_scores(0, 0)

    for hh in range(hpb):
        s = _dot_nt(qq_scr[hh], km_ref[:, head(hh)])
        col = lax.broadcasted_iota(jnp.int32, s.shape, 1)
        s = jnp.where(col < n_meta, s, NEG_BIG)
        softmax_step(hh, s, jnp.concatenate([vm_ref[:, head(hh)], jnp.ones((mp, hw), bf16)], axis=1), True)

    def body(p, c):
        stage(2 * p, 0)
        stage(2 * p + 1, 1)
        return c

    lax.fori_loop(0, i // 2, body, 0)

    @pl.when(i % 2 == 0)
    def _():
        consume(i, 0, True)

    @pl.when(i % 2 == 1)
    def _():
        stage(i - 1, 0)
        consume(i, 1, True)

    lam = (jnp.exp(jnp.sum(lq1_ref[...] * lk1_ref[...], axis=1, keepdims=True))
           - jnp.exp(jnp.sum(lq2_ref[...] * lk2_ref[...], axis=1, keepdims=True)) + LAM_INIT)
    for hh in range(hpb):
        acc = acc_scr[hh]
        o = acc[:, :hw] / acc[:, hw:]
        att = o[:tq] - lam * o[tq:]
        ms = jnp.mean(att * att, axis=-1, keepdims=True)
        att = att * lax.rsqrt(ms + EPS) * sg_ref[...] * (1.0 - LAM_INIT)
        o_ref[:, head(hh)] = att.astype(o_ref.dtype)


def _attention(q_rot, k_rot, proj, v_col0, km, vm, lq1, lk1, lq2, lk2, sg, batch, seq, heads, tq, n_meta, sub):
    t, aw = q_rot.shape
    nq = seq // tq
    hw = 2 * sub
    hpb = 4 if heads % 4 == 0 else (2 if heads % 2 == 0 else 1)
    bw = hpb * hw
    mp = km.shape[0]
    vec = lambda: pl.BlockSpec((1, sub), lambda b, h, i: (0, 0))
    return pl.pallas_call(
        functools.partial(_attn_kernel, n_meta=n_meta, sub=sub, hpb=hpb),
        grid=(batch, heads // hpb, nq),
        in_specs=[pl.BlockSpec((tq, bw), lambda b, h, i: (b * nq + i, h)),
                  pl.BlockSpec((seq, bw), lambda b, h, i: (b, h)),
                  pl.BlockSpec((seq, bw), lambda b, h, i: (b, v_col0 // hpb + h)),
                  pl.BlockSpec((mp, bw), lambda b, h, i: (0, h)),
                  pl.BlockSpec((mp, bw), lambda b, h, i: (0, h)),
                  vec(), vec(), vec(), vec(),
                  pl.BlockSpec((1, hw), lambda b, h, i: (0, 0))],
        out_specs=pl.BlockSpec((tq, bw), lambda b, h, i: (b * nq + i, h)),
        out_shape=jax.ShapeDtypeStruct((t, aw), bf16),
        scratch_shapes=[pltpu.VMEM((hpb, 2 * tq, hw), bf16), pltpu.VMEM((hpb, 2 * tq, hw), f32),
                        pltpu.VMEM((hpb, 2 * tq, 2 * hw), f32), pltpu.VMEM((2, hpb, 2 * tq, tq), f32)],
        compiler_params=_cparams(("parallel", "parallel", "arbitrary")),
        name="diff_attention",
    )(q_rot, k_rot, proj, km, vm, lq1, lk1, lq2, lk2, sg)


SSM_BLOCK = 8


def _cmul(ar, ai, br, bi):
    return ar * br - ai * bi, ar * bi + ai * br


def _dot3(a, b):
    ah = a.astype(bf16)
    al = (a - ah.astype(f32)).astype(bf16)
    bh = b.astype(bf16)
    bl = (b - bh.astype(f32)).astype(bf16)
    return _dot(ah, bh) + (_dot(ah, bl) + _dot(al, bh))


def _abar(a_re, a_im, log_dt):
    dt = jnp.exp(log_dt)
    er = jnp.exp(a_re * dt)
    return er * jnp.cos(a_im * dt), er * jnp.sin(a_im * dt)


def _powers(ar, ai, n):
    out = [(jnp.ones_like(ar), jnp.zeros_like(ai)), (ar, ai)]
    for _ in range(2, n + 1):
        out.append(_cmul(*out[-1], ar, ai))
    return out


def _ssm_prep_kernel(ar_ref, ai_ref, ld_ref, ac_ref, aic_ref, ldc_ref, as_ref, ais_ref, lds_ref,
                     br_ref, bi_ref, cr_ref, ci_ref, w_ref, q_ref, k_ref, pr_ref, ps_ref, *, n_state, n_chan):
    blk = SSM_BLOCK
    gpt = LANES // n_chan
    a_re, a_im = ar_ref[0], ai_ref[0]
    abr, abi = _abar(a_re, a_im, ld_ref[0])
    den = a_re * a_re + a_im * a_im
    nr, ni = abr - 1.0, abi
    f_re = (nr * a_re + ni * a_im) / den
    f_im = (ni * a_re - nr * a_im) / den
    bbr, bbi = _cmul(f_re, f_im, br_ref[0], bi_ref[0])
    pw = _powers(abr, abi, blk - 1)
    is_re = lax.broadcasted_iota(jnp.int32, a_re.shape, 1) < n_state
    rgrp = lax.broadcasted_iota(jnp.int32, a_re.shape, 0) // n_chan
    wf = []
    for i in range(blk):
        xr, xi = _cmul(bbr, bbi, *pw[blk - 1 - i])
        tile = jnp.where(is_re, xr, xi)
        wf.append(jnp.concatenate([jnp.where(rgrp == g, tile, 0.0) for g in range(gpt)], axis=1))
        w_ref[0, i] = wf[i].astype(bf16)
    c_re, c_im = cr_ref[0], ci_ref[0]
    acr, aci = _abar(ac_ref[0], aic_ref[0], ldc_ref[0])
    pwc = _powers(acr, aci, blk)
    cgrp = lax.broadcasted_iota(jnp.int32, c_re.shape, 1) // n_chan

    def by_group(tr, ti):
        rows = []
        for g in range(gpt):
            rows += [jnp.where(cgrp == g, tr, 0.0), jnp.where(cgrp == g, ti, 0.0)]
        return jnp.concatenate(rows, axis=0)

    for j in range(blk):
        qr, qi = _cmul(c_re, c_im, *pwc[j + 1])
        q_ref[0, :, j * LANES:(j + 1) * LANES] = by_group(qr, -qi).astype(bf16)
    ccat = by_group(c_re, -c_im)
    kd = [_dot3(wf[blk - 1 - dd], ccat).astype(bf16) for dd in range(blk)]
    for i in range(blk):
        for j in range(blk):
            tile = kd[j - i] if j >= i else jnp.zeros((LANES, LANES), bf16)
            k_ref[0, i * LANES:(i + 1) * LANES, j * LANES:(j + 1) * LANES] = tile
    asr, asi = _abar(as_ref[0], ais_ref[0], lds_ref[0])
    a8 = _powers(asr, asi, blk)[blk]
    pl_ = _powers(a8[0], a8[1], SUBLANES)[1:]
    re_lane = lax.broadcasted_iota(jnp.int32, asr.shape, 1) % LANES < n_state
    pr_ref[0] = _stack_rows([p[0] for p in pl_])
    ps_ref[0] = _stack_rows([jnp.where(re_lane, -p[1], p[1]) for p in pl_])


def _ssm_prep(rows, cols, scan, b_t, c_t, n_state, n_chan):
    ngt = b_t[0].shape[0]
    blk = SSM_BLOCK
    sl = scan[0].shape[2]
    spec3 = lambda a: pl.BlockSpec((1,) + a.shape[1:], lambda g: (g, 0, 0))
    ins = list(rows) + list(cols) + list(scan) + list(b_t) + list(c_t)
    w_s = jax.ShapeDtypeStruct((ngt, blk, LANES, sl), bf16)
    q_s = jax.ShapeDtypeStruct((ngt, sl, blk * LANES), bf16)
    k_s = jax.ShapeDtypeStruct((ngt, blk * LANES, blk * LANES), bf16)
    t_s = jax.ShapeDtypeStruct((ngt, SUBLANES, sl), f32)
    spec4 = lambda a: pl.BlockSpec((1,) + a.shape[1:], lambda g: (g, 0, 0, 0))
    return pl.pallas_call(
        functools.partial(_ssm_prep_kernel, n_state=n_state, n_chan=n_chan),
        grid=(ngt,),
        in_specs=[spec3(a) for a in ins],
        out_specs=[spec4(w_s), spec3(q_s), spec3(k_s), spec3(t_s), spec3(t_s)],
        out_shape=[w_s, q_s, k_s, t_s, t_s],
        compiler_params=_cparams(("parallel",)),
        name="ssm_prep",
    )(*ins)


def _ssm_kernel(u_ref, um_ref, w_ref, q_ref, k_ref, pr_ref, ps_ref, d_ref, y_ref,
                uf_scr, umf_scr, s_scr, sw_scr, xp_scr, yf_scr, *, rows_per_seq, meta_rows, n_state):
    blk = SSM_BLOCK
    r = u_ref.shape[0] // blk
    sl = s_scr.shape[1]
    tiles = sl // LANES
    pr, ps = pr_ref[0], ps_ref[0]

    def swap(x):
        return jnp.concatenate([pltpu.roll(x[:, c * LANES:(c + 1) * LANES], n_state, 1) for c in range(tiles)],
                               axis=1)

    uf_scr[...] = u_ref[...].astype(f32)
    umf_scr[...] = jnp.zeros_like(umf_scr)
    umf_scr[0:um_ref.shape[0], :] = um_ref[...].astype(f32)
    step_rows = lambda i: uf_scr[pl.ds(i, r, stride=blk), :]
    ucat = jnp.concatenate([step_rows(i).astype(bf16) for i in range(blk)], axis=1)
    umcat = jnp.concatenate([umf_scr[pl.ds(i, SUBLANES, stride=blk), :].astype(bf16) for i in range(blk)], axis=1)
    s = _dot(ucat, w_ref[0])
    sm = _dot(umcat, w_ref[0])
    rowmod = lax.broadcasted_iota(jnp.int32, s.shape, 0) % SUBLANES
    for k in (1, 2, 4):
        t = pltpu.roll(s, k, 0)
        t = t * pr[k - 1:k] + swap(t) * ps[k - 1:k]
        s = s + jnp.where(rowmod >= k, t, 0.0)
    s_scr[...] = s
    sw_scr[...] = swap(s)
    x0 = sm[0:1]
    for mrow in range(1, meta_rows):
        x0 = x0 * pr[0:1] + swap(x0) * ps[0:1] + sm[mrow:mrow + 1]
    x0w = swap(x0)
    row8 = lax.broadcasted_iota(jnp.int32, (SUBLANES, sl), 0)

    def body(b, carry):
        start = b % (rows_per_seq // SUBLANES) == 0
        rows = pl.ds(pl.multiple_of(b * SUBLANES, SUBLANES), SUBLANES)
        c = jnp.broadcast_to(jnp.where(start, x0, carry[0]), (SUBLANES, sl))
        cw = jnp.broadcast_to(jnp.where(start, x0w, carry[1]), (SUBLANES, sl))
        xs = s_scr[rows, :] + (c * pr + cw * ps)
        xw = sw_scr[rows, :] + (cw * pr - c * ps)
        xp_scr[rows, :] = jnp.where(row8 == 0, c, pltpu.roll(xs, 1, 0))
        return xs[SUBLANES - 1:SUBLANES], xw[SUBLANES - 1:SUBLANES]

    lax.fori_loop(0, r // SUBLANES, body, (x0, x0w))

    y_all = _dot(xp_scr[...].astype(bf16), q_ref[0]) + _dot(ucat, k_ref[0])
    for j in range(blk):
        y = y_all[:, j * LANES:(j + 1) * LANES] + d_ref[...] * step_rows(j)
        yf_scr[pl.ds(j, r, stride=blk), :] = _gelu(y)
    y_ref[...] = yf_scr[...].astype(y_ref.dtype)


def _ssm(u_arr, u_col0, um, prep, d, rows_per_seq, n_state):
    w, q, k, pr, ps = prep
    ngt, blk, _, sl = w.shape
    w = w.reshape(ngt, blk * LANES, sl)
    t = u_arr.shape[0]
    n_meta = um.shape[0]
    spec3 = lambda a: pl.BlockSpec((1,) + a.shape[1:], lambda g: (g, 0, 0))
    return pl.pallas_call(
        functools.partial(_ssm_kernel, rows_per_seq=rows_per_seq, meta_rows=n_meta // blk, n_state=n_state),
        grid=(ngt,),
        in_specs=[pl.BlockSpec((t, LANES), lambda g: (0, u_col0 + g)),
                  pl.BlockSpec((n_meta, LANES), lambda g: (0, g)),
                  spec3(w), spec3(q), spec3(k), spec3(pr), spec3(ps),
                  pl.BlockSpec((1, LANES), lambda g: (0, g))],
        out_specs=pl.BlockSpec((t, LANES), lambda g: (0, g)),
        out_shape=jax.ShapeDtypeStruct((t, ngt * LANES), bf16),
        scratch_shapes=[pltpu.VMEM((t, LANES), f32), pltpu.VMEM((SUBLANES * blk, LANES), f32),
                        pltpu.VMEM((t // blk, sl), f32), pltpu.VMEM((t // blk, sl), f32),
                        pltpu.VMEM((t // blk, sl), f32), pltpu.VMEM((t, LANES), f32)],
        compiler_params=_cparams(("parallel",)),
        name="ssm_blocked",
    )(u_arr, um, w, q, k, pr, ps, d)


def _mix_kernel(a_ref, wattn_ref, s_ref, wa_ref, wb_ref, ga_ref, gb_ref, o_ref):
    ya = _dot(a_ref[...], wattn_ref[...])
    s = s_ref[...]
    yb = _dot(s, wa_ref[...]) * _sigmoid(_dot(s, wb_ref[...]))
    o_ref[...] = (_sigmoid(ga_ref[...].astype(f32)) * ya
                  + _sigmoid(gb_ref[...].astype(f32)) * yb).astype(o_ref.dtype)


def _transpose_cast_kernel(x_ref, o_ref):
    o_ref[...] = x_ref[...].T.astype(o_ref.dtype)


def _transpose_cast(x, tr, dtype, name):
    rows, cols = x.shape
    return pl.pallas_call(
        _transpose_cast_kernel,
        grid=(rows // tr,),
        in_specs=[pl.BlockSpec((tr, cols), lambda i: (i, 0))],
        out_specs=pl.BlockSpec((cols, tr), lambda i: (0, i)),
        out_shape=jax.ShapeDtypeStruct((cols, rows), dtype),
        compiler_params=_cparams(("parallel",)),
        name=name,
    )(x)


def _out_proj_kernel(m_ref, w_ref, x_ref, o_ref):
    o_ref[...] = x_ref[...] + _dot(m_ref[...], w_ref[...])


def _peer_query_kernel(h_ref, g_ref, wq_ref, kk_ref, hn_ref, st_ref):
    x = h_ref[...]
    ms = jnp.mean(x * x, axis=-1, keepdims=True)
    hn_f = x * lax.rsqrt(ms + EPS) * g_ref[...]
    hn_ref[...] = hn_f.T.astype(bf16)
    q = _dot(hn_f.astype(bf16), wq_ref[...]).astype(bf16)
    n_heads = kk_ref.shape[0]
    half = kk_ref.shape[3]
    for h in range(n_heads):
        for side in range(2):
            c0 = (2 * h + side) * half
            st_ref[h, side] = _dot_nt(kk_ref[h, side], q[:, c0:c0 + half])


def _peer_query(h1, g, wq, kk, tm):
    t, d = h1.shape
    nh, _, nk, half = kk.shape
    return pl.pallas_call(
        _peer_query_kernel,
        grid=(t // tm,),
        in_specs=[pl.BlockSpec((tm, d), lambda i: (i, 0)),
                  pl.BlockSpec((1, d), lambda i: (0, 0)),
                  pl.BlockSpec(wq.shape, lambda i: (0, 0)),
                  pl.BlockSpec(kk.shape, lambda i: (0, 0, 0, 0))],
        out_specs=[pl.BlockSpec((d, tm), lambda i: (0, i)),
                   pl.BlockSpec((nh, 2, nk, tm), lambda i: (0, 0, 0, i))],
        out_shape=[jax.ShapeDtypeStruct((d, t), bf16), jax.ShapeDtypeStruct((nh, 2, nk, t), f32)],
        compiler_params=_cparams(("parallel",)),
        name="peer_query",
    )(h1, g, wq, kk)


def _sort_pairs(n):
    pairs = []
    p = 1
    while p < n:
        k = p
        while k >= 1:
            for j in range(k % p, n - k, 2 * k):
                for i in range(min(k, n - j - k)):
                    if (i + j) // (2 * p) == (i + j + k) // (2 * p):
                        pairs.append((i + j, i + j + k))
            k //= 2
        p *= 2
    return pairs


def _top_desc(tiles, n):
    m = len(tiles)
    vs = list(tiles)
    for i, j in _sort_pairs(1 << (m - 1).bit_length()):
        if j < m:
            vs[i], vs[j] = jnp.maximum(vs[i], vs[j]), jnp.minimum(vs[i], vs[j])
    neg = jnp.full(vs[0].shape, -jnp.inf, f32)
    vs = vs[:n] + [neg]
    vals = []
    for a in range(n):
        v = jnp.max(vs[0], axis=0, keepdims=True)
        vals.append(v)
        hit = vs[0] == v
        keep = min(len(vs) - 1, n - a - 1)
        vs = [jnp.where(hit, vs[k + 1], vs[k]) for k in range(keep)] + [neg]
    return _stack_rows(vals)


def _peer_topk_kernel(st_ref, r2_ref, e2_ref, n1_ref, e1_ref):
    chunk = LANES
    sub8 = lambda x: [x[k * SUBLANES:(k + 1) * SUBLANES] for k in range(x.shape[0] // SUBLANES)]
    for c in range(st_ref.shape[3] // chunk):
        cols = slice(c * chunk, (c + 1) * chunk)
        s1 = st_ref[0, 0, :, cols]
        s2 = st_ref[0, 1, :, cols]
        v1 = _top_desc(sub8(s1), TOPK)
        v2 = _top_desc(sub8(s2), TOPK)
        half = TOPK // 2
        cand = ([v1[0:1] + v2[:half], v1[0:1] + v2[half:]] + [v1[a:a + 1] + v2[:half] for a in range(1, half)]
                + [v1[half:] + v2[0:1]])
        best = _top_desc(cand, TOPK)
        tau = best[TOPK - 1:TOPK]
        z = jnp.sum(jnp.exp(best - best[0:1]), axis=0, keepdims=True)
        n1 = jnp.zeros(s1.shape, f32)
        rank2 = jnp.zeros(s2.shape, f32)
        for a in range(TOPK):
            cnt = jnp.sum(jnp.where((v1[a:a + 1] + v2) >= tau, 1.0, 0.0), axis=0, keepdims=True)
            n1 = jnp.where(s1 == v1[a:a + 1], cnt, n1)
            rank2 = jnp.where(s2 < v2[a:a + 1], a + 1.0, rank2)
        r2_ref[0, :, cols] = rank2.astype(bf16)
        e2_ref[0, :, cols] = jnp.exp(s2 - v2[0:1]).astype(bf16)
        n1_ref[0, :, cols] = n1
        e1_ref[0, :, cols] = jnp.exp(s1 - v1[0:1]) / z


def _peer_topk(st, tl):
    nh, _, nk, t = st.shape
    spec = lambda: pl.BlockSpec((1, nk, tl), lambda h, i: (h, 0, i))
    shp = lambda dt: jax.ShapeDtypeStruct((nh, nk, t), dt)
    return pl.pallas_call(
        _peer_topk_kernel,
        grid=(nh, t // tl),
        in_specs=[pl.BlockSpec((1, 2, nk, tl), lambda h, i: (h, 0, 0, i))],
        out_specs=[spec(), spec(), spec(), spec()],
        out_shape=[shp(bf16), shp(bf16), shp(f32), shp(f32)],
        compiler_params=_cparams(("parallel", "parallel")),
        name="peer_topk",
    )(st)


def _peer_dense_kernel(hn_ref, u_ref, vt_ref, r2_ref, e2_ref, n1_ref, e1_ref, h1_ref, o_ref,
                       acc_scr, a0_scr, a1_scr, *, nb, tn):
    e = pl.program_id(1)
    n_heads, nk, tm = r2_ref.shape
    eb = u_ref.shape[0]
    pk = 2 * SUBLANES

    @pl.when(e == 0)
    def _():
        acc_scr[...] = jnp.zeros_like(acc_scr)
        a0_scr[...] = _dot(u_ref[...], hn_ref[...])

    def step(a_cur, a_prev):
        if a_cur is not None:
            a_cur[...] = _dot(u_ref[...], hn_ref[...])
        blk = jnp.maximum(e - 1, 0)
        for c in range(tm // tn):
            cols = slice(c * tn, (c + 1) * tn)
            w_rows = []
            for ii in range(eb // nk):
                i1 = blk * (eb // nk) + ii
                gate = None
                for h in range(n_heads):
                    n1 = jnp.broadcast_to(n1_ref[h, pl.ds(i1, 1), cols], (pk, tn)).astype(bf16)
                    e1 = jnp.broadcast_to(e1_ref[h, pl.ds(i1, 1), cols], (pk, tn)).astype(bf16)
                    r2 = r2_ref[h, :, cols].reshape(nk // pk, pk, tn)
                    e2 = e2_ref[h, :, cols].reshape(nk // pk, pk, tn)
                    term = jnp.where(r2 < n1[None], e2, jnp.zeros_like(e2)) * e1[None]
                    gate = term if gate is None else gate + term
                g = _gelu(a_prev[ii * nk:(ii + 1) * nk, cols]).astype(bf16)
                w_rows.append(gate.reshape(nk, tn) * g)
            acc_scr[:, cols] += _dot(vt_ref[...], jnp.concatenate(w_rows, axis=0))

    middle = jnp.logical_and(e > 0, e < nb)

    @pl.when(jnp.logical_and(middle, e % 2 == 0))
    def _():
        step(a0_scr, a1_scr)

    @pl.when(jnp.logical_and(middle, e % 2 == 1))
    def _():
        step(a1_scr, a0_scr)

    @pl.when(e == nb)
    def _():
        step(None, a1_scr if (nb - 1) % 2 else a0_scr)
        o_ref[...] = h1_ref[...] + acc_scr[...].T


def _peer_dense(hn_t, u_b, vt_b, r2, e2, n1, e1, h1, tm, eb):
    d, t = hn_t.shape
    nb = u_b.shape[0] // eb
    nh, nk, _ = r2.shape
    tok = lambda: pl.BlockSpec((nh, nk, tm), lambda i, e: (0, 0, i), pipeline_mode=pl.Buffered(1))
    return pl.pallas_call(
        functools.partial(_peer_dense_kernel, nb=nb, tn=_tile(tm, 2 * LANES)),
        grid=(t // tm, nb + 1),
        in_specs=[pl.BlockSpec((d, tm), lambda i, e: (0, i)),
                  pl.BlockSpec((eb, d), lambda i, e: (jnp.minimum(e, nb - 1), 0)),
                  pl.BlockSpec((d, eb), lambda i, e: (0, jnp.maximum(e - 1, 0))),
                  tok(), tok(), tok(), tok(),
                  pl.BlockSpec((tm, d), lambda i, e: (i, 0), pipeline_mode=pl.Buffered(1))],
        out_specs=pl.BlockSpec((tm, d), lambda i, e: (i, 0)),
        out_shape=jax.ShapeDtypeStruct((t, d), f32),
        scratch_shapes=[pltpu.VMEM((d, tm), f32), pltpu.VMEM((eb, tm), f32), pltpu.VMEM((eb, tm), f32)],
        compiler_params=_cparams(("parallel", "arbitrary")),
        name="peer_dense",
    )(hn_t, u_b, vt_b, r2, e2, n1, e1, h1)


def _tile(n, pref):
    if n <= pref:
        return n
    t = pref // LANES * LANES
    while n % t:
        t -= LANES
    assert t > 0, (n, pref)
    return t


def kernel(x, meta_tokens, norm1_g, w_in, q_norm_g, k_norm_g, lambda_q1, lambda_k1, lambda_q2, lambda_k2, subln_g, w_attn_branch, ssm_a_re, ssm_a_im, ssm_log_dt, ssm_b_re, ssm_b_im, ssm_c_re, ssm_c_im, ssm_d, w_glu, w_out, norm2_g, peer_w_q, peer_k1, peer_k2, peer_u, peer_v):
    assert norm1_g.shape[0] == 1, "single-layer block only"
    batch, seq, d = x.shape
    n_meta = meta_tokens.shape[0]
    sub = q_norm_g.shape[-1]
    aw = w_attn_branch.shape[1]
    heads = aw // (2 * sub)
    n_groups, n_state, n_chan = ssm_b_re.shape[1:]
    sw = n_groups * n_chan
    assert 2 * sub == LANES and LANES % n_chan == 0 and sw % LANES == 0
    assert n_meta % SUBLANES == 0 and n_meta <= LANES
    t = batch * seq
    n_in = w_in.shape[2]
    assert n_in == 3 * aw + sw + 2 * d

    x2 = x.reshape(t, d)
    w_in_b = w_in[0].astype(bf16)

    proj = _rms_matmul(x2, norm1_g, w_in_b, _tile(t, 1024), _tile(n_in, 2048), bf16, "in_proj")
    proj_m = _rms_matmul(meta_tokens, norm1_g, w_in_b, n_meta, _tile(n_in, 1024), bf16, "in_proj_meta")

    half = sub // 2
    inv_freq = jnp.power(ROPE_THETA, -jnp.arange(half, dtype=f32) / half)
    ang = jnp.arange(n_meta + seq, dtype=f32)[:, None] * inv_freq[None, :]
    cos_t = jnp.tile(jnp.cos(ang), (1, LANES // half))
    sin_t = jnp.tile(jnp.concatenate([-jnp.sin(ang), jnp.sin(ang)], axis=1), (1, LANES // sub))
    lane = jnp.arange(LANES)
    jmat = (lane[:, None] // sub == lane[None, :] // sub).astype(bf16)
    gq = jnp.tile(q_norm_g, (1, aw // sub))
    gk = jnp.tile(k_norm_g, (1, aw // sub))
    tr = _tile(seq, 512)
    q_rot = _normrot(proj, 0, aw, gq, cos_t[n_meta:], sin_t[n_meta:], jmat, tr, sub ** -0.5 * math.log2(math.e), sub,
                      "q_normrot")
    k_rot = _normrot(proj, 1, aw, gk, cos_t[n_meta:], sin_t[n_meta:], jmat, tr, 1.0, sub, "k_normrot")
    km_rot = _normrot(proj_m, 1, aw, gk, cos_t[:n_meta], sin_t[:n_meta], jmat, n_meta, 1.0, sub, "k_normrot_meta")
    km = jnp.pad(km_rot, ((0, LANES - n_meta), (0, 0)))
    vm = jnp.pad(proj_m[:, 2 * aw:3 * aw], ((0, LANES - n_meta), (0, 0)))

    att = _attention(q_rot, k_rot, proj, 2 * aw // LANES, km, vm, lambda_q1, lambda_k1, lambda_q2, lambda_k2,
                     subln_g, batch, seq, heads, _tile(seq, 512), n_meta, sub)

    blk = SSM_BLOCK
    gpt = LANES // n_chan
    ngt = sw // LANES
    assert 2 * n_state == LANES and seq % (blk * SUBLANES) == 0 and n_meta % blk == 0
    grp = lambda a: a.reshape(ngt, gpt, n_state)
    row_l = lambda a: jnp.tile(jnp.repeat(grp(a), n_chan, axis=1), (1, 1, 2))
    col_l = lambda a: jnp.repeat(grp(a).transpose(0, 2, 1), n_chan, axis=2)
    scan_l = lambda a: jnp.tile(grp(a), (1, 1, 2)).reshape(ngt, 1, gpt * LANES)
    ldt2 = jnp.broadcast_to(ssm_log_dt[0][:, None], (n_groups, n_state))
    lays = lambda f: (f(ssm_a_re[0]), f(ssm_a_im[0]), f(ldt2))
    b_l = lambda b: jnp.tile(b[0].transpose(0, 2, 1).reshape(ngt, LANES, n_state), (1, 1, 2))
    c_l = lambda c: c[0].reshape(ngt, gpt, n_chan, n_state).transpose(0, 3, 1, 2).reshape(ngt, n_state, LANES)
    prep = _ssm_prep(lays(row_l), lays(col_l), lays(scan_l), (b_l(ssm_b_re), b_l(ssm_b_im)),
                     (c_l(ssm_c_re), c_l(ssm_c_im)), n_state, n_chan)
    u0 = 3 * aw
    ys = _ssm(proj, u0 // LANES, proj_m[:, u0:u0 + sw], prep, ssm_d, seq // blk, n_state)

    tm = _tile(t, 1024)
    tn = _tile(d, 1024)
    nj = d // tn
    ga_col0 = (3 * aw + sw) // tn
    gb_col0 = (3 * aw + sw + d) // tn
    grid = (nj, t // tm)
    sem = _cparams(("parallel", "arbitrary"))
    row_full = lambda k: pl.BlockSpec((tm, k), lambda j, i: (i, 0))
    w_col = lambda k, off=0: pl.BlockSpec((k, tn), lambda j, i: (0, j + off))
    out_tile = pl.BlockSpec((tm, tn), lambda j, i: (i, j))
    w_glu_b = w_glu[0].astype(bf16)
    gate = lambda col0: pl.BlockSpec((tm, tn), lambda j, i: (i, col0 + j))
    mix = pl.pallas_call(
        _mix_kernel, grid=grid,
        in_specs=[row_full(aw), w_col(aw), row_full(sw), w_col(sw), w_col(sw, nj), gate(ga_col0), gate(gb_col0)],
        out_specs=out_tile, out_shape=jax.ShapeDtypeStruct((t, d), bf16), compiler_params=sem, name="gated_mix",
    )(att, w_attn_branch[0].astype(bf16), ys, w_glu_b, w_glu_b, proj, proj)
    h1 = pl.pallas_call(
        _out_proj_kernel, grid=grid,
        in_specs=[row_full(d), w_col(d), out_tile],
        out_specs=out_tile, out_shape=jax.ShapeDtypeStruct((t, d), f32), compiler_params=sem, name="out_proj",
    )(mix, w_out[0].astype(bf16), x2)

    kk = jnp.stack([peer_k1[0], peer_k2[0]], axis=1).astype(bf16)
    assert kk.shape[2] == LANES and kk.shape[3] == LANES
    hn2, st = _peer_query(h1, norm2_g, peer_w_q[0].astype(bf16), kk, _tile(t, 512))
    r2, e2, n1, e1 = _peer_topk(st, _tile(t, 1024))
    u_b = peer_u[0].astype(bf16)
    vt_b = _transpose_cast(peer_v[0], _tile(peer_v.shape[1], 512), bf16, "peer_v_transpose")
    out = _peer_dense(hn2, u_b, vt_b, r2, e2, n1, e1, h1, _tile(t, 512), _tile(u_b.shape[0], 1024))
    return out.reshape(batch, seq, d)
```
